```python
import jax, jax.numpy as jnp
from jax import lax
import numpy as np

D_MODEL = 1024
BATCH = 8
SEQ = 8192
DEPTH = 1
DEC_BATCH = 128
DEC_SEQ = 8
PAST_LEN = 8192
PAGE_SIZE = 128

HEAD_DIM = 64
HEADS_PER_GROUP = 4
GROUPS = ((128, 1), (512, 4), (2048, 16))
N_GROUPS = 3
N_ATT_HEADS = N_GROUPS * HEADS_PER_GROUP
ATT_W = N_ATT_HEADS * HEAD_DIM
ATT_OUT_W = HEADS_PER_GROUP * HEAD_DIM
CONV_W = D_MODEL // 2
CONV_K = 3
D_FF = 4 * D_MODEL
N_BUCKETS = 32
MAX_DISTANCE = 2048
BLOCK = 128
EPS = 1e-6
SCALE = HEAD_DIM ** -0.5
SPLITS = (ATT_W, ATT_W, ATT_W, CONV_W, CONV_W, CONV_W, D_MODEL, D_MODEL)
PROJ_W = 3 * ATT_W + 3 * CONV_W + 2 * D_MODEL

kernel_name = "hybrid_dilated_attn_gated_conv_step"


def rmsnorm(x, g):
    xf = x.astype(jnp.float32)
    y = xf * lax.rsqrt(jnp.mean(xf * xf, axis=-1, keepdims=True) + EPS)
    return (y * g.astype(jnp.float32)).astype(x.dtype)


def t5_bucket(dist):
    n = np.asarray(dist)
    max_exact = N_BUCKETS // 2
    large = max_exact + (np.log(np.maximum(n, 1) / max_exact) / np.log(MAX_DISTANCE / max_exact)
                         * (N_BUCKETS - max_exact)).astype(np.int32)
    large = np.minimum(large, N_BUCKETS - 1)
    return np.where(n < max_exact, n, large).astype(np.int32)


def group_bias(rel_bias, g):
    win, dil = GROUPS[g]
    kb = win // dil
    buckets = t5_bucket(np.arange(kb + 1) * dil)
    sl = slice(g * HEADS_PER_GROUP, (g + 1) * HEADS_PER_GROUP)
    return rel_bias[buckets][:, sl].astype(jnp.float32)


def dilated_attn_prompt(q, k, v, bias, dil, kb):
    B, S, H, E = q.shape
    L = S // dil
    nb = -(-L // BLOCK)
    Lp = nb * BLOCK

    def to_res(t):
        t = t.reshape(B, L, dil, H, E).transpose(0, 2, 1, 3, 4).astype(jnp.float32)
        return jnp.pad(t, ((0, 0), (0, 0), (0, Lp - L), (0, 0), (0, 0)))

    def windows(t):
        t = jnp.pad(to_res(t), ((0, 0), (0, 0), (BLOCK, 0), (0, 0), (0, 0)))
        t = t.reshape(B, dil, nb + 1, BLOCK, H, E)
        return jnp.concatenate([t[:, :, :-1], t[:, :, 1:]], axis=3)

    qb = to_res(q).reshape(B, dil, nb, BLOCK, H, E)
    kw, vw = windows(k), windows(v)
    a = np.arange(BLOCK)[:, None]
    c = np.arange(2 * BLOCK)[None, :]
    rel = BLOCK + a - c
    kidx = (np.arange(nb)[:, None, None] - 1) * BLOCK + c[None]
    valid = (rel >= 0) & (rel <= kb) & (kidx >= 0)
    bias_rel = bias[np.clip(rel, 0, kb)].transpose(2, 0, 1)
    s = jnp.einsum('bdnqhe,bdnkhe->bdnhqk', qb, kw) * SCALE + bias_rel[None, None, None]
    s = jnp.where(valid[None, None, :, None], s, -jnp.inf)
    lse = jax.nn.logsumexp(s, axis=-1)
    p = jnp.exp(s - lse[..., None])
    o = jnp.einsum('bdnhqk,bdnkhe->bdnqhe', p, vw)
    o = o.reshape(B, dil, Lp, H, E)[:, :, :L].transpose(0, 2, 1, 3, 4).reshape(B, S, H, E)
    lse = lse.transpose(0, 1, 2, 4, 3).reshape(B, dil, Lp, H)[:, :, :L]
    lse = lse.transpose(0, 2, 1, 3).reshape(B, S, H)
    return o, lse


def dilated_attn_sample(q, k, v, kv_buf, bias, dil, kb):
    Bd, T, H, E = q.shape
    Wb = kv_buf.shape[1]
    k_all = jnp.concatenate([kv_buf[:, :, 0].astype(k.dtype), k], axis=1).astype(jnp.float32)
    v_all = jnp.concatenate([kv_buf[:, :, 1].astype(v.dtype), v], axis=1).astype(jnp.float32)
    idx = Wb + np.arange(T)[:, None] - np.arange(kb + 1)[None, :] * dil
    valid = idx >= 0
    idx = np.maximum(idx, 0)
    kg = k_all[:, idx]
    vg = v_all[:, idx]
    s = jnp.einsum('bthe,btjhe->bhtj', q.astype(jnp.float32), kg) * SCALE + bias.T[None, :, None, :]
    s = jnp.where(valid[None, None], s, -jnp.inf)
    lse = jax.nn.logsumexp(s, axis=-1)
    p = jnp.exp(s - lse[..., None])
    o = jnp.einsum('bhtj,btjhe->bthe', p, vg)
    return o, lse.transpose(0, 2, 1)


def dilated_attention(q, k, v, rel_bias, kv_bufs):
    S = q.shape[1]
    outs, lses, rows = [], [], []
    for g, (win, dil) in enumerate(GROUPS):
        sl = slice(g * HEADS_PER_GROUP, (g + 1) * HEADS_PER_GROUP)
        kb = win // dil
        qg, kg, vg = q[:, :, sl], k[:, :, sl], v[:, :, sl]
        bias = group_bias(rel_bias, g)
        kv = jnp.stack([kg, vg], axis=2)
        if kv_bufs is None:
            o, l = dilated_attn_prompt(qg, kg, vg, bias, dil, kb)
            rows.append(kv[:, -min(win, S):])
        else:
            o, l = dilated_attn_sample(qg, kg, vg, kv_bufs[g], bias, dil, kb)
            rows.append(kv)
        outs.append(o)
        lses.append(l)
    w = jax.nn.softmax(jnp.stack(lses, 0), axis=0)
    o = jnp.einsum('gbsh,gbshe->bshe', w, jnp.stack(outs, 0))
    B = q.shape[0]
    return o.reshape(B, S, ATT_OUT_W), rows


def layer(x, conv_prefix, kv_bufs, rel_bias, w_in, w_att_out, w_conv_out, w_o, conv_w, conv_b,
          norm_mix, norm_mlp, w_ff1, w_ff2):
    B, S, _ = x.shape
    xn = rmsnorm(x, norm_mix)
    proj = xn @ w_in
    q, k, v, cb, cc, ch, ga, gb = jnp.split(proj, [int(i) for i in np.cumsum(SPLITS)[:-1]], axis=-1)
    q = q.reshape(B, S, N_ATT_HEADS, HEAD_DIM)
    k = k.reshape(B, S, N_ATT_HEADS, HEAD_DIM)
    v = v.reshape(B, S, N_ATT_HEADS, HEAD_DIM)
    att, kv_rows = dilated_attention(q, k, v, rel_bias, kv_bufs)
    u = cc * ch
    u_pad = jnp.concatenate([conv_prefix.astype(u.dtype), u], axis=1)
    z = conv_b + conv_w[0] * u_pad[:, 0:S] + conv_w[1] * u_pad[:, 1:S + 1] + conv_w[2] * u_pad[:, 2:S + 2]
    conv_state = u_pad[:, -(CONV_K - 1):]
    branch_a = att.astype(x.dtype) @ w_att_out
    branch_b = (cb * z) @ w_conv_out
    m = jax.nn.sigmoid(ga) * branch_a + jax.nn.sigmoid(gb) * branch_b
    x = x + m @ w_o
    h = rmsnorm(x, norm_mlp)
    x = x + jnp.square(jax.nn.relu(h @ w_ff1)) @ w_ff2
    return x, kv_rows, conv_state


def setup_inputs(seed: int = 0) -> dict:
    key = jax.random.key(seed)
    ks = jax.random.split(key, 20)
    f32 = jnp.float32
    n = lambda k, shape, s: jax.random.normal(k, shape, f32) * s
    wb = [min(win, PAST_LEN) for win, _ in GROUPS]
    return {
        "x_prompt": n(ks[0], (BATCH, SEQ, D_MODEL), 1.0),
        "x_sample": n(ks[1], (DEC_BATCH, DEC_SEQ, D_MODEL), 1.0),
        "cache_kv_g0": n(ks[2], (DEPTH, DEC_BATCH, wb[0], 2, HEADS_PER_GROUP, HEAD_DIM), 1.0),
        "cache_kv_g1": n(ks[3], (DEPTH, DEC_BATCH, wb[1], 2, HEADS_PER_GROUP, HEAD_DIM), 1.0),
        "cache_kv_g2": n(ks[4], (DEPTH, DEC_BATCH, wb[2], 2, HEADS_PER_GROUP, HEAD_DIM), 1.0),
        "state_conv": n(ks[5], (DEPTH, DEC_BATCH, CONV_K - 1, CONV_W), 1.0),
        "w_in": n(ks[6], (DEPTH, D_MODEL, PROJ_W), D_MODEL ** -0.5),
        "w_att_out": n(ks[7], (DEPTH, ATT_OUT_W, D_MODEL), ATT_OUT_W ** -0.5),
        "w_conv_out": n(ks[8], (DEPTH, CONV_W, D_MODEL), CONV_W ** -0.5),
        "w_o": n(ks[9], (DEPTH, D_MODEL, D_MODEL), D_MODEL ** -0.5),
        "conv_w": n(ks[10], (DEPTH, CONV_K, CONV_W), CONV_K ** -0.5),
        "conv_b": n(ks[11], (DEPTH, CONV_W), 0.02),
        "rel_bias": n(ks[12], (N_BUCKETS, N_ATT_HEADS), 0.5),
        "norm_mix": 1.0 + n(ks[13], (DEPTH, D_MODEL), 0.05),
        "norm_mlp": 1.0 + n(ks[14], (DEPTH, D_MODEL), 0.05),
        "w_ff1": n(ks[15], (DEPTH, D_MODEL, D_FF), D_MODEL ** -0.5),
        "w_ff2": n(ks[16], (DEPTH, D_FF, D_MODEL), D_FF ** -0.5),
        "norm_final": 1.0 + n(ks[17], (D_MODEL,), 0.05),
    }


def reference(x_prompt, x_sample, cache_kv_g0, cache_kv_g1, cache_kv_g2, state_conv, w_in, w_att_out,
              w_conv_out, w_o, conv_w, conv_b, rel_bias, norm_mix, norm_mlp, w_ff1, w_ff2, norm_final):
    caches = (cache_kv_g0, cache_kv_g1, cache_kv_g2)
    xp, xs = x_prompt, x_sample
    kvp = [[] for _ in range(N_GROUPS)]
    kvs = [[] for _ in range(N_GROUPS)]
    convp, convs = [], []
    for l in range(DEPTH):
        wts = (w_in[l], w_att_out[l], w_conv_out[l], w_o[l], conv_w[l], conv_b[l],
               norm_mix[l], norm_mlp[l], w_ff1[l], w_ff2[l])
        prompt_prefix = jnp.zeros((xp.shape[0], CONV_K - 1, CONV_W), xp.dtype)
        xp, rows_p, cp = layer(xp, prompt_prefix, None, rel_bias, *wts)
        xs, rows_s, cs = layer(xs, state_conv[l], [c[l] for c in caches], rel_bias, *wts)
        for g in range(N_GROUPS):
            kvp[g].append(rows_p[g])
            kvs[g].append(rows_s[g])
        convp.append(cp)
        convs.append(cs)
    y_prompt = rmsnorm(xp, norm_final)
    y_sample = rmsnorm(xs, norm_final)
    return (y_prompt, y_sample,
            jnp.stack(kvp[0], 0), jnp.stack(kvp[1], 0), jnp.stack(kvp[2], 0), jnp.stack(convp, 0),
            jnp.stack(kvs[0], 0), jnp.stack(kvs[1], 0), jnp.stack(kvs[2], 0), jnp.stack(convs, 0))
```

```python
import functools

import jax
import jax.numpy as jnp
import numpy as np
from jax import lax
from jax.experimental import pallas as pl
from jax.experimental.pallas import tpu as pltpu

D_MODEL = 1024
HEAD_DIM = 64
HEADS_PER_GROUP = 4
GROUPS = ((128, 1), (512, 4), (2048, 16))
N_GROUPS = 3
GROUP_W = HEADS_PER_GROUP * HEAD_DIM
QKV_W = 3 * GROUP_W
ATT_W = N_GROUPS * GROUP_W
CONV_W = D_MODEL // 2
D_FF = 4 * D_MODEL
N_BUCKETS = 32
MAX_DISTANCE = 2048
KEYS_PER_QUERY = 128
BLOCK = 128
LANES = 128
EPS = 1e-6
SCALE = HEAD_DIM ** -0.5
NEG = -1e30

OFF_CONV = 3 * ATT_W
OFF_GA = OFF_CONV + 3 * CONV_W
OFF_GB = OFF_GA + D_MODEL
PROJ_W = OFF_GB + D_MODEL

VMEM_LIMIT_V7X = 56 * 1024 * 1024
F32 = jnp.float32
BF16 = jnp.bfloat16
_NT = (((1,), (1,)), ((), ()))


def _params(n_axes):
    return pltpu.CompilerParams(
        dimension_semantics=("arbitrary",) * n_axes, vmem_limit_bytes=VMEM_LIMIT_V7X)


def _resident(shape):
    return pl.BlockSpec(shape, lambda *_: (0,) * len(shape), pipeline_mode=pl.Buffered(1))


def _rmsnorm(x, g):
    y = x * lax.rsqrt(jnp.mean(x * x, axis=-1, keepdims=True) + EPS)
    return y * g


def _head_of_lane(shape):
    return lax.broadcasted_iota(jnp.int32, shape, len(shape) - 1) // HEAD_DIM


def _per_head_rows(x):
    head = _head_of_lane(x.shape)
    return jnp.concatenate([jnp.where(head == h, x, 0.0) for h in range(HEADS_PER_GROUP)], axis=0)


def _head_diagonal(x, t):
    head = _head_of_lane((t, GROUP_W))
    out = x[0:t, :]
    for h in range(1, HEADS_PER_GROUP):
        out = jnp.where(head == h, x[h * t:(h + 1) * t, :], out)
    return out


def _inproj_body(*refs, sample, kv_first=None):
    if sample:
        (x_ref, win_ref, wco_ref, nm_ref, cw_ref, cb_ref, p0_ref, p1_ref,
         qkv_ref, sga_ref, mb_ref, u_ref) = refs
    else:
        (x_ref, win_ref, wkvt_ref, wco_ref, nm_ref, cw_ref, cb_ref,
         qkv0_ref, qkv1_ref, qkv2_ref, kvt0_ref, kvt1_ref, kvt2_ref, sga_ref, mb_ref, u_ref,
         carry_ref, perm_ref) = refs
        qkv_refs = (qkv0_ref, qkv1_ref, qkv2_ref)
    x = x_ref[...]
    tm = x.shape[0]
    xn = _rmsnorm(x, nm_ref[...]).astype(BF16)

    def proj(lo, hi):
        return jnp.dot(xn, win_ref[:, lo:hi], preferred_element_type=F32)

    slab = 0
    for g, (_, dil) in enumerate(GROUPS):
        for which in range(3):
            col = which * ATT_W + g * GROUP_W
            p = proj(col, col + GROUP_W)
            if which == 0:
                p = p * SCALE
            dst = which * GROUP_W
            if sample:
                qkv_ref[:, g * QKV_W + dst:g * QKV_W + dst + GROUP_W] = p
            elif dil == 1:
                qkv_refs[g][0, :, dst:dst + GROUP_W] = p.astype(BF16)
            else:
                for half in range(GROUP_W // LANES):
                    perm_ref[slab] = p[:, half * LANES:(half + 1) * LANES]
                    for r in range(dil):
                        rows = perm_ref[slab, pl.ds(r, tm // dil, stride=dil), :]
                        lo = dst + half * LANES
                        qkv_refs[g][r, :, lo:lo + LANES] = rows.astype(BF16)
                    slab += 1

    if not sample:
        for g, kvt_ref in enumerate((kvt0_ref, kvt1_ref, kvt2_ref)):
            @pl.when(pl.program_id(1) >= kv_first[g])
            def _(g=g, kvt_ref=kvt_ref):
                kvt = lax.dot_general(wkvt_ref[g], xn, _NT, preferred_element_type=F32)
                kvt_ref[...] = kvt.reshape(kvt_ref.shape)

    pc = proj(OFF_CONV, OFF_GA)
    cb, cc, ch = pc[:, :CONV_W], pc[:, CONV_W:2 * CONV_W], pc[:, 2 * CONV_W:]
    u = cc * ch
    row = lax.broadcasted_iota(jnp.int32, (tm, 1), 0)
    if sample:
        p0, p1 = p0_ref[...], p1_ref[...]
        rs = jnp.bitwise_and(row, 7)
        u_ref[...] = u
    else:
        @pl.when(pl.program_id(1) == 0)
        def _():
            carry_ref[...] = jnp.zeros_like(carry_ref)
        p0, p1 = carry_ref[6:7, :], carry_ref[7:8, :]
        rs = row
        u_ref[...] = u[tm - 8:, :]
    um1 = jnp.where(rs == 0, p1, pltpu.roll(u, 1, 0))
    um2 = jnp.where(rs == 0, p0, jnp.where(rs == 1, p1, pltpu.roll(u, 2, 0)))
    if not sample:
        carry_ref[...] = u[tm - 8:, :]
    z = cb_ref[...] + cw_ref[0:1, :] * um2 + cw_ref[1:2, :] * um1 + cw_ref[2:3, :] * u
    branch_b = jnp.dot((cb * z).astype(BF16), wco_ref[...], preferred_element_type=F32)
    gb = proj(OFF_GB, PROJ_W)
    mb_ref[...] = (jax.nn.sigmoid(gb) * branch_b).astype(mb_ref.dtype)
    ga = proj(OFF_GA, OFF_GB)
    sga_ref[...] = jax.nn.sigmoid(ga).astype(sga_ref.dtype)


def _inproj_prompt(x, w_in, w_kvt, w_conv_out, norm_mix, conv_w, conv_b, tm):
    b, s, _ = x.shape
    nt = s // tm
    rows = [max(win, tm) for win, _ in GROUPS]
    kv_first = tuple((s - r) // tm for r in rows)

    def kvt_spec(first):
        return pl.BlockSpec((None, 2, GROUP_W, tm),
                            lambda bi, i: (bi, 0, 0, jnp.maximum(i - first, 0)))

    tok = lambda w: pl.BlockSpec((None, tm, w), lambda bi, i: (bi, i, 0))
    qkv_specs = [pl.BlockSpec((None, dil, tm // dil, QKV_W), lambda bi, i: (bi, 0, i, 0))
                 for _, dil in GROUPS]
    n_slabs = sum(3 * GROUP_W // LANES for _, dil in GROUPS if dil > 1)
    out_shape = (
        [jax.ShapeDtypeStruct((b, dil, s // dil, QKV_W), BF16) for _, dil in GROUPS]
        + [jax.ShapeDtypeStruct((b, 2, GROUP_W, r), F32) for r in rows]
        + [jax.ShapeDtypeStruct((b, s, D_MODEL), BF16),
           jax.ShapeDtypeStruct((b, s, D_MODEL), BF16),
           jax.ShapeDtypeStruct((b, 8, CONV_W), F32)])
    return pl.pallas_call(
        functools.partial(_inproj_body, sample=False, kv_first=kv_first),
        grid=(b, nt),
        in_specs=[tok(D_MODEL), _resident(w_in.shape), _resident(w_kvt.shape),
                  _resident(w_conv_out.shape), _resident(norm_mix.shape),
                  _resident(conv_w.shape), _resident(conv_b.shape)],
        out_specs=qkv_specs + [kvt_spec(f) for f in kv_first]
        + [tok(D_MODEL), tok(D_MODEL), pl.BlockSpec((None, 8, CONV_W), lambda bi, i: (bi, 0, 0))],
        out_shape=out_shape,
        scratch_shapes=[pltpu.VMEM((8, CONV_W), F32), pltpu.VMEM((n_slabs, tm, LANES), F32)],
        compiler_params=_params(2),
        name="inproj_prompt",
    )(x, w_in, w_kvt, w_conv_out, norm_mix, conv_w, conv_b)


def _inproj_sample(x, w_in, w_conv_out, norm_mix, conv_w, conv_b, p0, p1, tm):
    n = x.shape[0]
    tok = lambda w: pl.BlockSpec((tm, w), lambda i: (i, 0))
    out_shape = (
        jax.ShapeDtypeStruct((n, N_GROUPS * QKV_W), F32),
        jax.ShapeDtypeStruct((n, D_MODEL), BF16),
        jax.ShapeDtypeStruct((n, D_MODEL), BF16),
        jax.ShapeDtypeStruct((n, CONV_W), F32),
    )
    return pl.pallas_call(
        functools.partial(_inproj_body, sample=True),
        grid=(n // tm,),
        in_specs=[tok(D_MODEL), _resident(w_in.shape), _resident(w_conv_out.shape),
                  _resident(norm_mix.shape), _resident(conv_w.shape), _resident(conv_b.shape),
                  tok(CONV_W), tok(CONV_W)],
        out_specs=[tok(N_GROUPS * QKV_W), tok(D_MODEL), tok(D_MODEL), tok(CONV_W)],
        out_shape=out_shape,
        compiler_params=_params(1),
        name="inproj_sample",
    )(x, w_in, w_conv_out, norm_mix, conv_w, conv_b, p0, p1)


def _attn_prompt_body(q_ref, kc_ref, kp_ref, vc_ref, vp_ref, tab_ref, o_ref, l_ref,
                      kcat_ref, vcat_ref, *, tq):
    i = pl.program_id(2)
    kcat_ref[0:BLOCK, :] = kp_ref[...]
    kcat_ref[BLOCK:, :] = kc_ref[...]
    vcat_ref[0:BLOCK, :] = vp_ref[...]
    vcat_ref[BLOCK:, :] = vc_ref[...]
    first_tile = jnp.where(i == 0, 0, 1)
    for j in range(tq // BLOCK):
        q = q_ref[j * BLOCK:(j + 1) * BLOCK, :].astype(F32)
        qm = _per_head_rows(q).astype(BF16)
        k = kcat_ref[j * BLOCK:(j + 2) * BLOCK, :]
        v = vcat_ref[j * BLOCK:(j + 2) * BLOCK, :]
        s = lax.dot_general(qm, k, _NT, preferred_element_type=F32)
        s = s + (tab_ref[first_tile] if j == 0 else tab_ref[1])
        m = jnp.max(s, axis=-1, keepdims=True)
        p = jnp.exp(s - m)
        l = jnp.sum(p, axis=-1, keepdims=True)
        acc = jnp.dot(p.astype(BF16), v, preferred_element_type=F32)
        res = acc / l
        lse = jnp.broadcast_to(m + jnp.log(l), res.shape)
        o_ref[j * BLOCK:(j + 1) * BLOCK, :] = _head_diagonal(res, BLOCK).astype(o_ref.dtype)
        l_ref[j * BLOCK:(j + 1) * BLOCK, :] = _head_diagonal(lse, BLOCK)


def _attn_prompt(qkv, tab, tq, name):
    b, dil, L, _ = qkv.shape
    tq = min(tq, L)
    cur_spec = lambda which: pl.BlockSpec(
        (None, None, tq, GROUP_W), lambda bi, r, i: (bi, r, i, which))
    prev_spec = lambda which: pl.BlockSpec(
        (None, None, BLOCK, GROUP_W),
        lambda bi, r, i: (bi, r, jnp.maximum(i * (tq // BLOCK) - 1, 0), which))
    out_spec = pl.BlockSpec((None, None, tq, GROUP_W), lambda bi, r, i: (bi, r, i, 0))
    return pl.pallas_call(
        functools.partial(_attn_prompt_body, tq=tq),
        grid=(b, dil, L // tq),
        in_specs=[cur_spec(0), cur_spec(1), prev_spec(1), cur_spec(2), prev_spec(2),
                  _resident(tab.shape)],
        out_specs=[out_spec, out_spec],
        out_shape=(jax.ShapeDtypeStruct((b, dil, L, GROUP_W), BF16),
                   jax.ShapeDtypeStruct((b, dil, L, GROUP_W), F32)),
        scratch_shapes=[pltpu.VMEM((tq + BLOCK, GROUP_W), BF16),
                        pltpu.VMEM((tq + BLOCK, GROUP_W), BF16)],
        compiler_params=_params(3),
        name=name,
    )(qkv, qkv, qkv, qkv, qkv, tab)


DEC_T = 8
NEW_PAD = 16


def _attn_sample_body(qkv_ref, c0_ref, c1_ref, c2_ref, tc0_ref, tc1_ref, tc2_ref, tn_ref,
                      o0_ref, o1_ref, o2_ref, l0_ref, l1_ref, l2_ref):
    caches = (c0_ref, c1_ref, c2_ref)
    tcs = (tc0_ref, tc1_ref, tc2_ref)
    outs = ((o0_ref, l0_ref), (o1_ref, l1_ref), (o2_ref, l2_ref))
    pad = jnp.zeros((NEW_PAD - DEC_T, GROUP_W), F32)
    for g in range(N_GROUPS):
        c_ref = caches[g]
        base = g * QKV_W
        q = qkv_ref[:, base:base + GROUP_W]
        kn = jnp.concatenate([qkv_ref[:, base + GROUP_W:base + 2 * GROUP_W], pad], axis=0)
        vn = jnp.concatenate([qkv_ref[:, base + 2 * GROUP_W:base + QKV_W], pad], axis=0)
        qm = _per_head_rows(q).astype(BF16)
        s_n = lax.dot_general(qm, kn.astype(BF16), _NT, preferred_element_type=F32) + tn_ref[g]
        s_c = jnp.dot(qm, c_ref[0].astype(BF16), preferred_element_type=F32) + tcs[g][...]
        m = jnp.maximum(jnp.max(s_c, axis=-1, keepdims=True), jnp.max(s_n, axis=-1, keepdims=True))
        p_c = jnp.exp(s_c - m)
        p_n = jnp.exp(s_n - m)
        l = jnp.sum(p_c, axis=-1, keepdims=True) + jnp.sum(p_n, axis=-1, keepdims=True)
        acc = jnp.dot(p_n.astype(BF16), vn.astype(BF16), preferred_element_type=F32)
        acc = acc + lax.dot_general(p_c.astype(BF16), c_ref[1].astype(BF16), _NT,
                                    preferred_element_type=F32)
        res = acc / l
        lse = jnp.broadcast_to(m + jnp.log(l), res.shape)
        o_ref, l_ref = outs[g]
        o_ref[...] = _head_diagonal(res, DEC_T)
        l_ref[...] = _head_diagonal(lse, DEC_T)


def _attn_sample(qkv, caches_t, tcs, tn):
    n = qkv.shape[0]
    bd = n // DEC_T
    cache_specs = [pl.BlockSpec((None,) + c.shape[1:], lambda bi: (bi, 0, 0, 0)) for c in caches_t]
    out_spec = pl.BlockSpec((DEC_T, GROUP_W), lambda bi: (bi, 0))
    out = jax.ShapeDtypeStruct((n, GROUP_W), F32)
    res = pl.pallas_call(
        _attn_sample_body,
        grid=(bd,),
        in_specs=[pl.BlockSpec((DEC_T, N_GROUPS * QKV_W), lambda bi: (bi, 0))] + cache_specs
        + [_resident(tc.shape) for tc in tcs] + [_resident(tn.shape)],
        out_specs=[out_spec] * 6,
        out_shape=(out,) * 6,
        compiler_params=_params(1),
        name="attn_sample",
    )(qkv, *caches_t, *tcs, tn)
    return res[:3], res[3:]


FF_CHUNK = 1024


def _natural_rows(ref, scratch_ref, slab):
    dil, rows, _ = ref.shape
    if dil == 1:
        return ref[0].astype(F32)
    halves = []
    for half in range(GROUP_W // LANES):
        for r in range(dil):
            scratch_ref[slab + half, pl.ds(r, rows, stride=dil), :] = (
                ref[r, :, half * LANES:(half + 1) * LANES].astype(F32))
        halves.append(scratch_ref[slab + half])
    return jnp.concatenate(halves, axis=1)


def _outproj_body(x_ref, o0_ref, o1_ref, o2_ref, l0_ref, l1_ref, l2_ref, sga_ref, mb_ref,
                  wao_ref, wo_ref, w1_ref, w2_ref, nmlp_ref, nfin_ref, y_ref, *scratch):
    if scratch:
        slabs = GROUP_W // LANES
        os_ = [_natural_rows(ref, scratch[0], slabs * k) for k, ref in
               enumerate((o0_ref, o1_ref, o2_ref))]
        ls_ = [_natural_rows(ref, scratch[0], slabs * (3 + k)) for k, ref in
               enumerate((l0_ref, l1_ref, l2_ref))]
    else:
        os_ = [ref[...] for ref in (o0_ref, o1_ref, o2_ref)]
        ls_ = [ref[...] for ref in (l0_ref, l1_ref, l2_ref)]
    (o0, o1, o2), (l0, l1, l2) = os_, ls_
    m = jnp.maximum(l0, jnp.maximum(l1, l2))
    e0, e1, e2 = jnp.exp(l0 - m), jnp.exp(l1 - m), jnp.exp(l2 - m)
    att = (e0 * o0 + e1 * o1 + e2 * o2) / (e0 + e1 + e2)
    branch_a = jnp.dot(att.astype(BF16), wao_ref[...], preferred_element_type=F32)
    mix = sga_ref[...].astype(F32) * branch_a + mb_ref[...].astype(F32)
    x1 = x_ref[...] + jnp.dot(mix.astype(BF16), wo_ref[...], preferred_element_type=F32)
    h = _rmsnorm(x1, nmlp_ref[...]).astype(BF16)
    acc = x1
    for c in range(D_FF // FF_CHUNK):
        a = jnp.dot(h, w1_ref[:, c * FF_CHUNK:(c + 1) * FF_CHUNK], preferred_element_type=F32)
        a = jnp.square(jnp.maximum(a, 0.0)).astype(BF16)
        acc = acc + jnp.dot(a, w2_ref[c * FF_CHUNK:(c + 1) * FF_CHUNK, :],
                            preferred_element_type=F32)
    y_ref[...] = _rmsnorm(acc, nfin_ref[...])


def _outproj_prompt(x, os_, ls_, sga, mb, weights, tm):
    b, s, _ = x.shape
    tok = pl.BlockSpec((None, tm, D_MODEL), lambda bi, i: (bi, i, 0))
    grouped = [pl.BlockSpec((None, dil, tm // dil, GROUP_W), lambda bi, i: (bi, 0, i, 0))
               for _, dil in GROUPS]
    return pl.pallas_call(
        _outproj_body,
        grid=(b, s // tm),
        in_specs=[tok] + grouped + grouped + [tok, tok] + [_resident(w.shape) for w in weights],
        out_specs=tok,
        out_shape=jax.ShapeDtypeStruct((b, s, D_MODEL), F32),
        scratch_shapes=[pltpu.VMEM((2 * N_GROUPS * GROUP_W // LANES, tm, LANES), F32)],
        compiler_params=_params(2),
        name="outproj_ffn_prompt",
    )(x, *os_, *ls_, sga, mb, *weights)


def _outproj_sample(x, os_, ls_, sga, mb, weights, tm):
    n = x.shape[0]
    tok = lambda w: pl.BlockSpec((tm, w), lambda i: (i, 0))
    return pl.pallas_call(
        _outproj_body,
        grid=(n // tm,),
        in_specs=[tok(D_MODEL)] + [tok(GROUP_W)] * 6 + [tok(D_MODEL), tok(D_MODEL)]
        + [_resident(w.shape) for w in weights],
        out_specs=tok(D_MODEL),
        out_shape=jax.ShapeDtypeStruct((n, D_MODEL), F32),
        compiler_params=_params(1),
        name="outproj_ffn_sample",
    )(x, *os_, *ls_, sga, mb, *weights)


def _t5_bucket(dist):
    n = np.asarray(dist)
    max_exact = N_BUCKETS // 2
    large = max_exact + (np.log(np.maximum(n, 1) / max_exact) / np.log(MAX_DISTANCE / max_exact)
                         * (N_BUCKETS - max_exact)).astype(np.int32)
    large = np.minimum(large, N_BUCKETS - 1)
    return np.where(n < max_exact, n, large).astype(np.int32)


def _group_bias(rel_bias, g):
    dil = GROUPS[g][1]
    buckets = _t5_bucket(np.arange(KEYS_PER_QUERY + 1) * dil)
    return rel_bias[buckets][:, g * HEADS_PER_GROUP:(g + 1) * HEADS_PER_GROUP].astype(F32)


def _bias_table(bias, strides, valid):
    tab = bias[np.clip(strides, 0, KEYS_PER_QUERY)].transpose(2, 0, 1)
    tab = jnp.where(valid[None], tab, NEG)
    return tab.reshape(HEADS_PER_GROUP * strides.shape[0], strides.shape[1])


def _prompt_table(bias):
    a = np.arange(BLOCK)[:, None]
    c = np.arange(2 * BLOCK)[None, :]
    rel = BLOCK + a - c
    valid = (rel >= 0) & (rel <= KEYS_PER_QUERY)
    return jnp.stack([_bias_table(bias, rel, valid & (c >= BLOCK)), _bias_table(bias, rel, valid)])


def _sample_tables(bias, win, dil):
    t = np.arange(DEC_T)[:, None]
    dist = win + t - np.arange(win)[None, :]
    tc = _bias_table(bias, dist // dil, (dist % dil == 0) & (dist // dil <= KEYS_PER_QUERY))
    tp = np.arange(NEW_PAD)[None, :]
    dist = t - tp
    ok = (tp < DEC_T) & (dist >= 0) & (dist % dil == 0) & (dist // dil <= KEYS_PER_QUERY)
    return tc, _bias_table(bias, dist // dil, ok)


def _cache_transposed(cache):
    bd, wb = cache.shape[:2]
    return jnp.transpose(cache, (0, 2, 3, 4, 1)).reshape(bd, 2, GROUP_W, wb)


def _cache_rows(kvt, win):
    b = kvt.shape[0]
    kvt = kvt[..., kvt.shape[-1] - win:].reshape(b, 2, HEADS_PER_GROUP, HEAD_DIM, win)
    return jnp.transpose(kvt, (0, 4, 1, 2, 3))[None]


TM_INPROJ = 512
TM_OUTPROJ = 512
TQ_ATTN = 1024


def kernel(x_prompt, x_sample, cache_kv_g0, cache_kv_g1, cache_kv_g2, state_conv, w_in, w_att_out,
           w_conv_out, w_o, conv_w, conv_b, rel_bias, norm_mix, norm_mlp, w_ff1, w_ff2, norm_final):
    assert w_in.shape[0] == 1, "one layer"
    b, s, _ = x_prompt.shape
    bd, t, _ = x_sample.shape
    assert t == DEC_T
    wp = w_in[0].astype(BF16)
    wkvt = jnp.stack([
        jnp.concatenate([w_in[0][:, ATT_W + g * GROUP_W:ATT_W + (g + 1) * GROUP_W],
                         w_in[0][:, 2 * ATT_W + g * GROUP_W:2 * ATT_W + (g + 1) * GROUP_W]],
                        axis=1).T for g in range(N_GROUPS)]).astype(BF16)
    wco = w_conv_out[0].astype(BF16)
    wao, wo = w_att_out[0].astype(BF16), w_o[0].astype(BF16)
    w1, w2 = w_ff1[0].astype(BF16), w_ff2[0].astype(BF16)
    nmix, nmlp, nfin = norm_mix[0][None], norm_mlp[0][None], norm_final[None]
    cw, cb = conv_w[0], conv_b[0][None]
    biases = [_group_bias(rel_bias, g) for g in range(N_GROUPS)]

    weights = (wao, wo, w1, w2, nmlp, nfin)

    qkv0, qkv1, qkv2, kvt0, kvt1, kvt2, sga, mb, utail = _inproj_prompt(
        x_prompt, wp, wkvt, wco, nmix, cw, cb, TM_INPROJ)
    os_, ls_ = [], []
    for g, qkv in enumerate((qkv0, qkv1, qkv2)):
        o, lse = _attn_prompt(qkv, _prompt_table(biases[g]), TQ_ATTN, f"attn_prompt_g{g}")
        os_.append(o)
        ls_.append(lse)
    y_prompt = _outproj_prompt(x_prompt, os_, ls_, sga, mb, weights, TM_OUTPROJ)
    kv_prompt = [_cache_rows(kvt, win) for kvt, (win, _) in zip((kvt0, kvt1, kvt2), GROUPS)]
    conv_prompt = utail[:, 6:8][None]

    n = bd * t
    st = state_conv[0]
    p0 = jnp.repeat(st[:, 0], t, axis=0)
    p1 = jnp.repeat(st[:, 1], t, axis=0)
    qkv_s, sga_s, mb_s, u_s = _inproj_sample(
        x_sample.reshape(n, D_MODEL), wp, wco, nmix, cw, cb, p0, p1, min(TM_INPROJ, n))
    tabs = [_sample_tables(biases[g], *GROUPS[g]) for g in range(N_GROUPS)]
    caches_t = [_cache_transposed(c[0]) for c in (cache_kv_g0, cache_kv_g1, cache_kv_g2)]
    os_s, ls_s = _attn_sample(qkv_s, caches_t, [tb[0] for tb in tabs],
                              jnp.stack([tb[1] for tb in tabs]))
    y_sample = _outproj_sample(x_sample.reshape(n, D_MODEL), os_s, ls_s, sga_s, mb_s, weights,
                               min(TM_OUTPROJ, n))
    kv_sample = [qkv_s[:, g * QKV_W + GROUP_W:(g + 1) * QKV_W].reshape(
        1, bd, t, 2, HEADS_PER_GROUP, HEAD_DIM) for g in range(N_GROUPS)]
    conv_sample = u_s.reshape(bd, t, CONV_W)[:, t - 2:][None]

    return (y_prompt, y_sample.reshape(bd, t, D_MODEL),
            kv_prompt[0], kv_prompt[1], kv_prompt[2], conv_prompt,
            kv_sample[0], kv_sample[1], kv_sample[2], conv_sample)
```

```python
import functools

import jax
import jax.numpy as jnp
import numpy as np
from jax import lax
from jax.experimental import pallas as pl
from jax.experimental.pallas import tpu as pltpu

D_MODEL = 1024
HEAD_DIM = 64
HEADS_PER_GROUP = 4
GROUPS = ((128, 1), (512, 4), (2048, 16))
N_GROUPS = 3
GROUP_W = HEADS_PER_GROUP * HEAD_DIM
QKV_W = 3 * GROUP_W
ATT_W = N_GROUPS * GROUP_W
CONV_W = D_MODEL // 2
D_FF = 4 * D_MODEL
N_BUCKETS = 32
MAX_DISTANCE = 2048
KEYS_PER_QUERY = 128
BLOCK = 128
LANES = 128
EPS = 1e-6
SCALE = HEAD_DIM ** -0.5
NEG = -1e30

OFF_CONV = 3 * ATT_W
OFF_GA = OFF_CONV + 3 * CONV_W
OFF_GB = OFF_GA + D_MODEL
PROJ_W = OFF_GB + D_MODEL

VMEM_LIMIT_V7X = 56 * 1024 * 1024
F32 = jnp.float32
BF16 = jnp.bfloat16
_NT = (((1,), (1,)), ((), ()))


def _params(n_axes):
    return pltpu.CompilerParams(
        dimension_semantics=("arbitrary",) * n_axes, vmem_limit_bytes=VMEM_LIMIT_V7X)


def _resident(shape):
    return pl.BlockSpec(shape, lambda *_: (0,) * len(shape), pipeline_mode=pl.Buffered(1))


def _rmsnorm(x, g):
    y = x * lax.rsqrt(jnp.mean(x * x, axis=-1, keepdims=True) + EPS)
    return y * g


def _head_of_lane(shape):
    return lax.broadcasted_iota(jnp.int32, shape, len(shape) - 1) // HEAD_DIM


def _per_head_rows(x):
    head = _head_of_lane(x.shape)
    return jnp.concatenate([jnp.where(head == h, x, 0.0) for h in range(HEADS_PER_GROUP)], axis=0)


def _head_diagonal(x, t):
    head = _head_of_lane((t, GROUP_W))
    out = x[0:t, :]
    for h in range(1, HEADS_PER_GROUP):
        out = jnp.where(head == h, x[h * t:(h + 1) * t, :], out)
    return out


def _inproj_body(*refs, sample):
    if sample:
        (x_ref, win_ref, wco_ref, nm_ref, cw_ref, cb_ref, p0_ref, p1_ref,
         qkv_ref, sga_ref, mb_ref, u_ref) = refs
    else:
        (x_ref, win_ref, wco_ref, nm_ref, cw_ref, cb_ref,
         qkv0_ref, qkv1_ref, qkv2_ref, sga_ref, mb_ref, u_ref, carry_ref, perm_ref) = refs
        qkv_refs = (qkv0_ref, qkv1_ref, qkv2_ref)
    x = x_ref[...]
    tm = x.shape[0]
    xn = _rmsnorm(x, nm_ref[...]).astype(BF16)

    def proj(lo, hi):
        return jnp.dot(xn, win_ref[:, lo:hi], preferred_element_type=F32)

    slab = 0
    for g, (_, dil) in enumerate(GROUPS):
        for which in range(3):
            col = which * ATT_W + g * GROUP_W
            p = proj(col, col + GROUP_W)
            if which == 0:
                p = p * SCALE
            dst = which * GROUP_W
            if sample:
                qkv_ref[:, g * QKV_W + dst:g * QKV_W + dst + GROUP_W] = p
            elif dil == 1:
                qkv_refs[g][0, :, dst:dst + GROUP_W] = p.astype(BF16)
            else:
                for half in range(GROUP_W // LANES):
                    perm_ref[slab] = p[:, half * LANES:(half + 1) * LANES]
                    for r in range(dil):
                        rows = perm_ref[slab, pl.ds(r, tm // dil, stride=dil), :]
                        lo = dst + half * LANES
                        qkv_refs[g][r, :, lo:lo + LANES] = rows.astype(BF16)
                    slab += 1

    pc = proj(OFF_CONV, OFF_GA)
    cb, cc, ch = pc[:, :CONV_W], pc[:, CONV_W:2 * CONV_W], pc[:, 2 * CONV_W:]
    u = cc * ch
    row = lax.broadcasted_iota(jnp.int32, (tm, 1), 0)
    if sample:
        p0, p1 = p0_ref[...], p1_ref[...]
        rs = jnp.bitwise_and(row, 7)
        u_ref[...] = u
    else:
        seq_start = pl.program_id(1) == 0
        p0 = jnp.where(seq_start, 0.0, carry_ref[6:7, :])
        p1 = jnp.where(seq_start, 0.0, carry_ref[7:8, :])
        rs = row
        u_ref[...] = u[tm - 8:, :]
    um1 = jnp.where(rs == 0, p1, pltpu.roll(u, 1, 0))
    um2 = jnp.where(rs == 0, p0, jnp.where(rs == 1, p1, pltpu.roll(u, 2, 0)))
    if not sample:
        carry_ref[...] = u[tm - 8:, :]
    z = cb_ref[...] + cw_ref[0:1, :] * um2 + cw_ref[1:2, :] * um1 + cw_ref[2:3, :] * u
    branch_b = jnp.dot((cb * z).astype(BF16), wco_ref[...], preferred_element_type=F32)
    gb = proj(OFF_GB, PROJ_W)
    mb_ref[...] = (jax.nn.sigmoid(gb) * branch_b).astype(mb_ref.dtype)
    ga = proj(OFF_GA, OFF_GB)
    sga_ref[...] = jax.nn.sigmoid(ga).astype(sga_ref.dtype)


def _inproj_prompt(x, w_in, w_conv_out, norm_mix, conv_w, conv_b, tm):
    b, s, _ = x.shape
    nt = s // tm
    tok = lambda w: pl.BlockSpec((None, tm, w), lambda bi, i: (bi, i, 0))
    qkv_specs = [pl.BlockSpec((None, dil, tm // dil, QKV_W), lambda bi, i: (bi, 0, i, 0))
                 for _, dil in GROUPS]
    n_slabs = sum(3 * GROUP_W // LANES for _, dil in GROUPS if dil > 1)
    out_shape = (
        [jax.ShapeDtypeStruct((b, dil, s // dil, QKV_W), BF16) for _, dil in GROUPS]
        + [jax.ShapeDtypeStruct((b, s, D_MODEL), BF16),
           jax.ShapeDtypeStruct((b, s, D_MODEL), BF16),
           jax.ShapeDtypeStruct((b, 8, CONV_W), F32)])
    return pl.pallas_call(
        functools.partial(_inproj_body, sample=False),
        grid=(b, nt),
        in_specs=[tok(D_MODEL), _resident(w_in.shape), _resident(w_conv_out.shape),
                  _resident(norm_mix.shape), _resident(conv_w.shape), _resident(conv_b.shape)],
        out_specs=qkv_specs
        + [tok(D_MODEL), tok(D_MODEL), pl.BlockSpec((None, 8, CONV_W), lambda bi, i: (bi, 0, 0))],
        out_shape=out_shape,
        scratch_shapes=[pltpu.VMEM((8, CONV_W), F32), pltpu.VMEM((n_slabs, tm, LANES), F32)],
        compiler_params=_params(2),
        name="inproj_prompt",
    )(x, w_in, w_conv_out, norm_mix, conv_w, conv_b)


def _cache_rows_body(x_ref, wkvt_ref, nm_ref, kvt_ref):
    xn = _rmsnorm(x_ref[...], nm_ref[...]).astype(BF16)
    kvt = lax.dot_general(wkvt_ref[...], xn, _NT, preferred_element_type=F32)
    kvt_ref[...] = kvt.reshape(kvt_ref.shape)


def _cache_rows_prompt(x, w_kvt, norm_mix, rows, tm, name):
    b, s, _ = x.shape
    tm = min(tm, rows)
    first = (s - rows) // tm
    return pl.pallas_call(
        _cache_rows_body,
        grid=(b, rows // tm),
        in_specs=[pl.BlockSpec((None, tm, D_MODEL), lambda bi, i: (bi, first + i, 0)),
                  _resident(w_kvt.shape), _resident(norm_mix.shape)],
        out_specs=pl.BlockSpec((None, 2, GROUP_W, tm), lambda bi, i: (bi, 0, 0, i)),
        out_shape=jax.ShapeDtypeStruct((b, 2, GROUP_W, rows), F32),
        compiler_params=_params(2),
        name=name,
    )(x, w_kvt, norm_mix)


def _inproj_sample(x, w_in, w_conv_out, norm_mix, conv_w, conv_b, p0, p1, tm):
    n = x.shape[0]
    tok = lambda w: pl.BlockSpec((tm, w), lambda i: (i, 0))
    out_shape = (
        jax.ShapeDtypeStruct((n, N_GROUPS * QKV_W), F32),
        jax.ShapeDtypeStruct((n, D_MODEL), BF16),
        jax.ShapeDtypeStruct((n, D_MODEL), BF16),
        jax.ShapeDtypeStruct((n, CONV_W), F32),
    )
    return pl.pallas_call(
        functools.partial(_inproj_body, sample=True),
        grid=(n // tm,),
        in_specs=[tok(D_MODEL), _resident(w_in.shape), _resident(w_conv_out.shape),
                  _resident(norm_mix.shape), _resident(conv_w.shape), _resident(conv_b.shape),
                  tok(CONV_W), tok(CONV_W)],
        out_specs=[tok(N_GROUPS * QKV_W), tok(D_MODEL), tok(D_MODEL), tok(CONV_W)],
        out_shape=out_shape,
        compiler_params=_params(1),
        name="inproj_sample",
    )(x, w_in, w_conv_out, norm_mix, conv_w, conv_b, p0, p1)


def _attn_prompt_body(q_ref, kc_ref, kp_ref, vc_ref, vp_ref, tab_ref, o_ref, l_ref,
                      kcat_ref, vcat_ref, *, tq):
    i = pl.program_id(2)
    kcat_ref[0:BLOCK, :] = kp_ref[...]
    kcat_ref[BLOCK:, :] = kc_ref[...]
    vcat_ref[0:BLOCK, :] = vp_ref[...]
    vcat_ref[BLOCK:, :] = vc_ref[...]
    first_tile = jnp.where(i == 0, 0, 1)
    for j in range(tq // BLOCK):
        q = q_ref[j * BLOCK:(j + 1) * BLOCK, :].astype(F32)
        qm = _per_head_rows(q).astype(BF16)
        k = kcat_ref[j * BLOCK:(j + 2) * BLOCK, :]
        v = vcat_ref[j * BLOCK:(j + 2) * BLOCK, :]
        s = lax.dot_general(qm, k, _NT, preferred_element_type=F32)
        s = s + (tab_ref[first_tile] if j == 0 else tab_ref[1])
        m = jnp.max(s, axis=-1, keepdims=True)
        p = jnp.exp(s - m)
        l = jnp.sum(p, axis=-1, keepdims=True)
        acc = jnp.dot(p.astype(BF16), v, preferred_element_type=F32)
        res = acc / l
        lse = jnp.broadcast_to(m + jnp.log(l), res.shape)
        o_ref[j * BLOCK:(j + 1) * BLOCK, :] = _head_diagonal(res, BLOCK).astype(o_ref.dtype)
        l_ref[j * BLOCK:(j + 1) * BLOCK, :] = _head_diagonal(lse, BLOCK)


def _attn_prompt(qkv, tab, tq, name):
    b, dil, L, _ = qkv.shape
    tq = min(tq, L)
    cur_spec = lambda which: pl.BlockSpec(
        (None, None, tq, GROUP_W), lambda bi, r, i: (bi, r, i, which))
    prev_spec = lambda which: pl.BlockSpec(
        (None, None, BLOCK, GROUP_W),
        lambda bi, r, i: (bi, r, jnp.maximum(i * (tq // BLOCK) - 1, 0), which))
    out_spec = pl.BlockSpec((None, None, tq, GROUP_W), lambda bi, r, i: (bi, r, i, 0))
    return pl.pallas_call(
        functools.partial(_attn_prompt_body, tq=tq),
        grid=(b, dil, L // tq),
        in_specs=[cur_spec(0), cur_spec(1), prev_spec(1), cur_spec(2), prev_spec(2),
                  _resident(tab.shape)],
        out_specs=[out_spec, out_spec],
        out_shape=(jax.ShapeDtypeStruct((b, dil, L, GROUP_W), BF16),
                   jax.ShapeDtypeStruct((b, dil, L, GROUP_W), F32)),
        scratch_shapes=[pltpu.VMEM((tq + BLOCK, GROUP_W), BF16),
                        pltpu.VMEM((tq + BLOCK, GROUP_W), BF16)],
        compiler_params=_params(3),
        name=name,
    )(qkv, qkv, qkv, qkv, qkv, tab)


DEC_T = 8
NEW_PAD = 16


def _attn_sample_body(qkv_ref, c0_ref, c1_ref, c2_ref, tc0_ref, tc1_ref, tc2_ref, tn_ref,
                      o0_ref, o1_ref, o2_ref, l0_ref, l1_ref, l2_ref):
    caches = (c0_ref, c1_ref, c2_ref)
    tcs = (tc0_ref, tc1_ref, tc2_ref)
    outs = ((o0_ref, l0_ref), (o1_ref, l1_ref), (o2_ref, l2_ref))
    pad = jnp.zeros((NEW_PAD - DEC_T, GROUP_W), F32)
    for g in range(N_GROUPS):
        c_ref = caches[g]
        base = g * QKV_W
        q = qkv_ref[:, base:base + GROUP_W]
        kn = jnp.concatenate([qkv_ref[:, base + GROUP_W:base + 2 * GROUP_W], pad], axis=0)
        vn = jnp.concatenate([qkv_ref[:, base + 2 * GROUP_W:base + QKV_W], pad], axis=0)
        qm = _per_head_rows(q).astype(BF16)
        s_n = lax.dot_general(qm, kn.astype(BF16), _NT, preferred_element_type=F32) + tn_ref[g]
        s_c = jnp.dot(qm, c_ref[0].astype(BF16), preferred_element_type=F32) + tcs[g][...]
        m = jnp.maximum(jnp.max(s_c, axis=-1, keepdims=True), jnp.max(s_n, axis=-1, keepdims=True))
        p_c = jnp.exp(s_c - m)
        p_n = jnp.exp(s_n - m)
        l = jnp.sum(p_c, axis=-1, keepdims=True) + jnp.sum(p_n, axis=-1, keepdims=True)
        acc = jnp.dot(p_n.astype(BF16), vn.astype(BF16), preferred_element_type=F32)
        acc = acc + lax.dot_general(p_c.astype(BF16), c_ref[1].astype(BF16), _NT,
                                    preferred_element_type=F32)
        res = acc / l
        lse = jnp.broadcast_to(m + jnp.log(l), res.shape)
        o_ref, l_ref = outs[g]
        o_ref[...] = _head_diagonal(res, DEC_T)
        l_ref[...] = _head_diagonal(lse, DEC_T)


def _attn_sample(qkv, caches_t, tcs, tn):
    n = qkv.shape[0]
    bd = n // DEC_T
    cache_specs = [pl.BlockSpec((None,) + c.shape[1:], lambda bi: (bi, 0, 0, 0)) for c in caches_t]
    out_spec = pl.BlockSpec((DEC_T, GROUP_W), lambda bi: (bi, 0))
    out = jax.ShapeDtypeStruct((n, GROUP_W), F32)
    res = pl.pallas_call(
        _attn_sample_body,
        grid=(bd,),
        in_specs=[pl.BlockSpec((DEC_T, N_GROUPS * QKV_W), lambda bi: (bi, 0))] + cache_specs
        + [_resident(tc.shape) for tc in tcs] + [_resident(tn.shape)],
        out_specs=[out_spec] * 6,
        out_shape=(out,) * 6,
        compiler_params=_params(1),
        name="attn_sample",
    )(qkv, *caches_t, *tcs, tn)
    return res[:3], res[3:]


FF_CHUNK = 1024


def _natural_rows(ref, scratch_ref, slab):
    dil, rows, _ = ref.shape
    if dil == 1:
        return ref[0].astype(F32)
    halves = []
    for half in range(GROUP_W // LANES):
        for r in range(dil):
            scratch_ref[slab + half, pl.ds(r, rows, stride=dil), :] = (
                ref[r, :, half * LANES:(half + 1) * LANES].astype(F32))
        halves.append(scratch_ref[slab + half])
    return jnp.concatenate(halves, axis=1)


def _outproj_body(x_ref, o0_ref, o1_ref, o2_ref, l0_ref, l1_ref, l2_ref, sga_ref, mb_ref,
                  wao_ref, wo_ref, w1_ref, w2_ref, nmlp_ref, nfin_ref, y_ref, *scratch):
    if scratch:
        slabs = GROUP_W // LANES
        os_ = [_natural_rows(ref, scratch[0], slabs * k) for k, ref in
               enumerate((o0_ref, o1_ref, o2_ref))]
        ls_ = [_natural_rows(ref, scratch[0], slabs * (3 + k)) for k, ref in
               enumerate((l0_ref, l1_ref, l2_ref))]
    else:
        os_ = [ref[...] for ref in (o0_ref, o1_ref, o2_ref)]
        ls_ = [ref[...] for ref in (l0_ref, l1_ref, l2_ref)]
    (o0, o1, o2), (l0, l1, l2) = os_, ls_
    m = jnp.maximum(l0, jnp.maximum(l1, l2))
    e0, e1, e2 = jnp.exp(l0 - m), jnp.exp(l1 - m), jnp.exp(l2 - m)
    att = (e0 * o0 + e1 * o1 + e2 * o2) / (e0 + e1 + e2)
    branch_a = jnp.dot(att.astype(BF16), wao_ref[...], preferred_element_type=F32)
    mix = sga_ref[...].astype(F32) * branch_a + mb_ref[...].astype(F32)
    x1 = x_ref[...] + jnp.dot(mix.astype(BF16), wo_ref[...], preferred_element_type=F32)
    h = _rmsnorm(x1, nmlp_ref[...]).astype(BF16)
    acc = x1
    for c in range(D_FF // FF_CHUNK):
        a = jnp.dot(h, w1_ref[:, c * FF_CHUNK:(c + 1) * FF_CHUNK], preferred_element_type=F32)
        a = jnp.square(jnp.maximum(a, 0.0)).astype(BF16)
        acc = acc + jnp.dot(a, w2_ref[c * FF_CHUNK:(c + 1) * FF_CHUNK, :],
                            preferred_element_type=F32)
    y_ref[...] = _rmsnorm(acc, nfin_ref[...])


def _outproj_prompt(x, os_, ls_, sga, mb, weights, tm):
    b, s, _ = x.shape
    tok = pl.BlockSpec((None, tm, D_MODEL), lambda bi, i: (bi, i, 0))
    grouped = [pl.BlockSpec((None, dil, tm // dil, GROUP_W), lambda bi, i: (bi, 0, i, 0))
               for _, dil in GROUPS]
    return pl.pallas_call(
        _outproj_body,
        grid=(b, s // tm),
        in_specs=[tok] + grouped + grouped + [tok, tok] + [_resident(w.shape) for w in weights],
        out_specs=tok,
        out_shape=jax.ShapeDtypeStruct((b, s, D_MODEL), F32),
        scratch_shapes=[pltpu.VMEM((2 * N_GROUPS * GROUP_W // LANES, tm, LANES), F32)],
        compiler_params=_params(2),
        name="outproj_ffn_prompt",
    )(x, *os_, *ls_, sga, mb, *weights)


def _outproj_sample(x, os_, ls_, sga, mb, weights, tm):
    n = x.shape[0]
    tok = lambda w: pl.BlockSpec((tm, w), lambda i: (i, 0))
    return pl.pallas_call(
        _outproj_body,
        grid=(n // tm,),
        in_specs=[tok(D_MODEL)] + [tok(GROUP_W)] * 6 + [tok(D_MODEL), tok(D_MODEL)]
        + [_resident(w.shape) for w in weights],
        out_specs=tok(D_MODEL),
        out_shape=jax.ShapeDtypeStruct((n, D_MODEL), F32),
        compiler_params=_params(1),
        name="outproj_ffn_sample",
    )(x, *os_, *ls_, sga, mb, *weights)


def _t5_bucket(dist):
    n = np.asarray(dist)
    max_exact = N_BUCKETS // 2
    large = max_exact + (np.log(np.maximum(n, 1) / max_exact) / np.log(MAX_DISTANCE / max_exact)
                         * (N_BUCKETS - max_exact)).astype(np.int32)
    large = np.minimum(large, N_BUCKETS - 1)
    return np.where(n < max_exact, n, large).astype(np.int32)


def _group_bias(rel_bias, g):
    dil = GROUPS[g][1]
    buckets = _t5_bucket(np.arange(KEYS_PER_QUERY + 1) * dil)
    return rel_bias[buckets][:, g * HEADS_PER_GROUP:(g + 1) * HEADS_PER_GROUP].astype(F32)


def _toeplitz(vec, n_rows, n_cols, offset):
    rev = vec[:, ::-1]
    last = vec.shape[1] - 1
    rows = [rev[:, last - offset - a:last - offset - a + n_cols] for a in range(n_rows)]
    return jnp.stack(rows, axis=1).reshape(vec.shape[0] * n_rows, n_cols)


def _pad_neg(vec, before, after):
    h = vec.shape[0]
    return jnp.concatenate([jnp.full((h, before), NEG, F32), vec, jnp.full((h, after), NEG, F32)],
                           axis=1)


def _prompt_table(bias):
    by_stride = _pad_neg(bias.T, BLOCK - 1, BLOCK - 1)
    tab = _toeplitz(by_stride, BLOCK, 2 * BLOCK, 2 * BLOCK - 1)
    cur = (np.arange(2 * BLOCK) >= BLOCK)[None, :]
    return jnp.stack([jnp.where(cur, tab, NEG), tab])


def _sample_tables(bias, win, dil):
    h = bias.shape[1]
    spread = jnp.concatenate([bias.T[:, :, None], jnp.full((h, KEYS_PER_QUERY + 1, dil - 1), NEG)],
                             axis=2).reshape(h, -1)[:, :KEYS_PER_QUERY * dil + 1]
    tc = _toeplitz(_pad_neg(spread, 0, DEC_T - 1), DEC_T, win, win)
    tn = _toeplitz(_pad_neg(spread, NEW_PAD - 1, 0), DEC_T, NEW_PAD, NEW_PAD - 1)
    return tc, tn


def _cache_transposed(cache):
    bd, wb = cache.shape[:2]
    return jnp.transpose(cache, (0, 2, 3, 4, 1)).reshape(bd, 2, GROUP_W, wb)


def _cache_rows(kvt):
    b, _, _, win = kvt.shape
    kvt = kvt.reshape(b, 2, HEADS_PER_GROUP, HEAD_DIM, win)
    return jnp.transpose(kvt, (0, 4, 1, 2, 3))[None]


TM_INPROJ = 512
TM_OUTPROJ = 512
TM_CACHE_ROWS = 1024
TQ_ATTN = 1024


def kernel(x_prompt, x_sample, cache_kv_g0, cache_kv_g1, cache_kv_g2, state_conv, w_in, w_att_out,
           w_conv_out, w_o, conv_w, conv_b, rel_bias, norm_mix, norm_mlp, w_ff1, w_ff2, norm_final):
    assert w_in.shape[0] == 1, "one layer"
    b, s, _ = x_prompt.shape
    bd, t, _ = x_sample.shape
    assert t == DEC_T
    wp = w_in[0].astype(BF16)
    wkvt = jnp.stack([
        jnp.concatenate([w_in[0][:, ATT_W + g * GROUP_W:ATT_W + (g + 1) * GROUP_W],
                         w_in[0][:, 2 * ATT_W + g * GROUP_W:2 * ATT_W + (g + 1) * GROUP_W]],
                        axis=1).T for g in range(N_GROUPS)]).astype(BF16)
    wco = w_conv_out[0].astype(BF16)
    wao, wo = w_att_out[0].astype(BF16), w_o[0].astype(BF16)
    w1, w2 = w_ff1[0].astype(BF16), w_ff2[0].astype(BF16)
    nmix, nmlp, nfin = norm_mix[0][None], norm_mlp[0][None], norm_final[None]
    cw, cb = conv_w[0], conv_b[0][None]
    biases = [_group_bias(rel_bias, g) for g in range(N_GROUPS)]

    weights = (wao, wo, w1, w2, nmlp, nfin)

    qkv0, qkv1, qkv2, sga, mb, utail = _inproj_prompt(x_prompt, wp, wco, nmix, cw, cb, TM_INPROJ)
    kvts = [_cache_rows_prompt(x_prompt, wkvt[g], nmix, win, TM_CACHE_ROWS, f"cache_rows_g{g}")
            for g, (win, _) in enumerate(GROUPS)]
    os_, ls_ = [], []
    for g, qkv in enumerate((qkv0, qkv1, qkv2)):
        o, lse = _attn_prompt(qkv, _prompt_table(biases[g]), TQ_ATTN, f"attn_prompt_g{g}")
        os_.append(o)
        ls_.append(lse)
    y_prompt = _outproj_prompt(x_prompt, os_, ls_, sga, mb, weights, TM_OUTPROJ)
    kv_prompt = [_cache_rows(kvt) for kvt in kvts]
    conv_prompt = utail[:, 6:8][None]

    n = bd * t
    st = state_conv[0]
    p0 = jnp.repeat(st[:, 0], t, axis=0)
    p1 = jnp.repeat(st[:, 1], t, axis=0)
    qkv_s, sga_s, mb_s, u_s = _inproj_sample(
        x_sample.reshape(n, D_MODEL), wp, wco, nmix, cw, cb, p0, p1, min(TM_INPROJ, n))
    tabs = [_sample_tables(biases[g], *GROUPS[g]) for g in range(N_GROUPS)]
    caches_t = [_cache_transposed(c[0]) for c in (cache_kv_g0, cache_kv_g1, cache_kv_g2)]
    os_s, ls_s = _attn_sample(qkv_s, caches_t, [tb[0] for tb in tabs],
                              jnp.stack([tb[1] for tb in tabs]))
    y_sample = _outproj_sample(x_sample.reshape(n, D_MODEL), os_s, ls_s, sga_s, mb_s, weights,
                               min(TM_OUTPROJ, n))
    kv_sample = [qkv_s[:, g * QKV_W + GROUP_W:(g + 1) * QKV_W].reshape(
        1, bd, t, 2, HEADS_PER_GROUP, HEAD_DIM) for g in range(N_GROUPS)]
    conv_sample = u_s.reshape(bd, t, CONV_W)[:, t - 2:][None]

    return (y_prompt, y_sample.reshape(bd, t, D_MODEL),
            kv_prompt[0], kv_prompt[1], kv_prompt[2], conv_prompt,
            kv_sample[0], kv_sample[1], kv_sample[2], conv_sample)
```

```python
import functools

import jax
import jax.numpy as jnp
import numpy as np
from jax import lax
from jax.experimental import pallas as pl
from jax.experimental.pallas import tpu as pltpu

D_MODEL = 1024
HEAD_DIM = 64
HEADS_PER_GROUP = 4
GROUPS = ((128, 1), (512, 4), (2048, 16))
N_GROUPS = 3
GROUP_W = HEADS_PER_GROUP * HEAD_DIM
QKV_W = 3 * GROUP_W
ATT_W = N_GROUPS * GROUP_W
CONV_W = D_MODEL // 2
D_FF = 4 * D_MODEL
N_BUCKETS = 32
MAX_DISTANCE = 2048
KEYS_PER_QUERY = 128
BLOCK = 128
LANES = 128
EPS = 1e-6
SCALE = HEAD_DIM ** -0.5
LOG2E = float(np.log2(np.e))
NEG = -1e30

OFF_CONV = 3 * ATT_W
OFF_GA = OFF_CONV + 3 * CONV_W
OFF_GB = OFF_GA + D_MODEL
PROJ_W = OFF_GB + D_MODEL

VMEM_LIMIT_V7X = 56 * 1024 * 1024
F32 = jnp.float32
BF16 = jnp.bfloat16
_NT = (((1,), (1,)), ((), ()))


def _params(n_axes):
    return pltpu.CompilerParams(
        dimension_semantics=("arbitrary",) * n_axes, vmem_limit_bytes=VMEM_LIMIT_V7X)


def _resident(shape):
    return pl.BlockSpec(shape, lambda *_: (0,) * len(shape), pipeline_mode=pl.Buffered(1))


def _rmsnorm(x, g):
    y = x * lax.rsqrt(jnp.mean(x * x, axis=-1, keepdims=True) + EPS)
    return y * g


def _head_of_lane(shape):
    return lax.broadcasted_iota(jnp.int32, shape, len(shape) - 1) // HEAD_DIM


def _per_head_rows(x):
    head = _head_of_lane(x.shape)
    return jnp.concatenate([jnp.where(head == h, x, 0.0) for h in range(HEADS_PER_GROUP)], axis=0)


def _head_diagonal(x, t):
    head = _head_of_lane((t, GROUP_W))
    out = x[0:t, :]
    for h in range(1, HEADS_PER_GROUP):
        out = jnp.where(head == h, x[h * t:(h + 1) * t, :], out)
    return out


def _inproj_body(*refs, sample):
    if sample:
        (x_ref, win_ref, wco_ref, nm_ref, cw_ref, cb_ref, p0_ref, p1_ref,
         qkv_ref, sga_ref, mb_ref, u_ref) = refs
    else:
        (x_ref, win_ref, wco_ref, nm_ref, cw_ref, cb_ref,
         qkv0_ref, qkv1_ref, qkv2_ref, sga_ref, mb_ref, u_ref, carry_ref, perm_ref) = refs
        qkv_refs = (qkv0_ref, qkv1_ref, qkv2_ref)
    x = x_ref[...]
    tm = x.shape[0]
    xn = _rmsnorm(x, nm_ref[...]).astype(BF16)

    def proj(lo, hi):
        return jnp.dot(xn, win_ref[:, lo:hi], preferred_element_type=F32)

    ga = proj(OFF_GA, OFF_GB)
    sga_ref[...] = jax.nn.sigmoid(ga).astype(sga_ref.dtype)
    pc = proj(OFF_CONV, OFF_GA)
    cb, cc, ch = pc[:, :CONV_W], pc[:, CONV_W:2 * CONV_W], pc[:, 2 * CONV_W:]
    u = cc * ch
    row = lax.broadcasted_iota(jnp.int32, (tm, 1), 0)
    if sample:
        p0, p1 = p0_ref[...], p1_ref[...]
        rs = jnp.bitwise_and(row, 7)
        u_ref[...] = u
    else:
        seq_start = pl.program_id(1) == 0
        p0 = jnp.where(seq_start, 0.0, carry_ref[6:7, :])
        p1 = jnp.where(seq_start, 0.0, carry_ref[7:8, :])
        rs = row
        u_ref[...] = u[tm - 8:, :]
    um1 = jnp.where(rs == 0, p1, pltpu.roll(u, 1, 0))
    um2 = jnp.where(rs == 0, p0, jnp.where(rs == 1, p1, pltpu.roll(u, 2, 0)))
    if not sample:
        carry_ref[...] = u[tm - 8:, :]
    z = cb_ref[...] + cw_ref[0:1, :] * um2 + cw_ref[1:2, :] * um1 + cw_ref[2:3, :] * u
    branch_b = jnp.dot((cb * z).astype(BF16), wco_ref[...], preferred_element_type=F32)
    gb = proj(OFF_GB, PROJ_W)
    mb_ref[...] = (jax.nn.sigmoid(gb) * branch_b).astype(mb_ref.dtype)

    slab = 0
    pqkv = proj(0, OFF_CONV)
    for g, (_, dil) in reversed(list(enumerate(GROUPS))):
        for which in range(3):
            col = which * ATT_W + g * GROUP_W
            p = pqkv[:, col:col + GROUP_W]
            if which == 0:
                p = p * (SCALE * LOG2E)
            dst = which * GROUP_W
            if sample:
                qkv_ref[:, g * QKV_W + dst:g * QKV_W + dst + GROUP_W] = p
            elif dil == 1:
                qkv_refs[g][0, :, dst:dst + GROUP_W] = p.astype(BF16)
            else:
                for half in range(GROUP_W // LANES):
                    perm_ref[slab] = p[:, half * LANES:(half + 1) * LANES]
                    for r in range(dil):
                        rows = perm_ref[slab, pl.ds(r, tm // dil, stride=dil), :]
                        lo = dst + half * LANES
                        qkv_refs[g][r, :, lo:lo + LANES] = rows.astype(BF16)
                    slab += 1


def _inproj_prompt(x, w_in, w_conv_out, norm_mix, conv_w, conv_b, tm):
    b, s, _ = x.shape
    nt = s // tm
    tok = lambda w: pl.BlockSpec((None, tm, w), lambda bi, i: (bi, i, 0))
    qkv_specs = [pl.BlockSpec((None, dil, tm // dil, QKV_W), lambda bi, i: (bi, 0, i, 0))
                 for _, dil in GROUPS]
    n_slabs = sum(3 * GROUP_W // LANES for _, dil in GROUPS if dil > 1)
    out_shape = (
        [jax.ShapeDtypeStruct((b, dil, s // dil, QKV_W), BF16) for _, dil in GROUPS]
        + [jax.ShapeDtypeStruct((b, s, D_MODEL), BF16),
           jax.ShapeDtypeStruct((b, s, D_MODEL), BF16),
           jax.ShapeDtypeStruct((b, 8, CONV_W), F32)])
    return pl.pallas_call(
        functools.partial(_inproj_body, sample=False),
        grid=(b, nt),
        in_specs=[tok(D_MODEL), _resident(w_in.shape), _resident(w_conv_out.shape),
                  _resident(norm_mix.shape), _resident(conv_w.shape), _resident(conv_b.shape)],
        out_specs=qkv_specs
        + [tok(D_MODEL), tok(D_MODEL), pl.BlockSpec((None, 8, CONV_W), lambda bi, i: (bi, 0, 0))],
        out_shape=out_shape,
        scratch_shapes=[pltpu.VMEM((8, CONV_W), F32), pltpu.VMEM((n_slabs, tm, LANES), F32)],
        compiler_params=_params(2),
        name="inproj_prompt",
    )(x, w_in, w_conv_out, norm_mix, conv_w, conv_b)


def _cache_rows_body(x_ref, wkvt_ref, nm_ref, kvt_ref):
    xn = _rmsnorm(x_ref[...], nm_ref[...]).astype(BF16)
    kvt = lax.dot_general(wkvt_ref[...], xn, _NT, preferred_element_type=F32)
    kvt_ref[...] = kvt.reshape(kvt_ref.shape)


def _cache_rows_prompt(x, w_kvt, norm_mix, rows, tm, name):
    b, s, _ = x.shape
    tm = min(tm, rows)
    first = (s - rows) // tm
    return pl.pallas_call(
        _cache_rows_body,
        grid=(b, rows // tm),
        in_specs=[pl.BlockSpec((None, tm, D_MODEL), lambda bi, i: (bi, first + i, 0)),
                  _resident(w_kvt.shape), _resident(norm_mix.shape)],
        out_specs=pl.BlockSpec((None, 2, GROUP_W, tm), lambda bi, i: (bi, 0, 0, i)),
        out_shape=jax.ShapeDtypeStruct((b, 2, GROUP_W, rows), F32),
        compiler_params=_params(2),
        name=name,
    )(x, w_kvt, norm_mix)


def _inproj_sample(x, w_in, w_conv_out, norm_mix, conv_w, conv_b, p0, p1, tm):
    n = x.shape[0]
    tok = lambda w: pl.BlockSpec((tm, w), lambda i: (i, 0))
    out_shape = (
        jax.ShapeDtypeStruct((n, N_GROUPS * QKV_W), F32),
        jax.ShapeDtypeStruct((n, D_MODEL), BF16),
        jax.ShapeDtypeStruct((n, D_MODEL), BF16),
        jax.ShapeDtypeStruct((n, CONV_W), F32),
    )
    return pl.pallas_call(
        functools.partial(_inproj_body, sample=True),
        grid=(n // tm,),
        in_specs=[tok(D_MODEL), _resident(w_in.shape), _resident(w_conv_out.shape),
                  _resident(norm_mix.shape), _resident(conv_w.shape), _resident(conv_b.shape),
                  tok(CONV_W), tok(CONV_W)],
        out_specs=[tok(N_GROUPS * QKV_W), tok(D_MODEL), tok(D_MODEL), tok(CONV_W)],
        out_shape=out_shape,
        compiler_params=_params(1),
        name="inproj_sample",
    )(x, w_in, w_conv_out, norm_mix, conv_w, conv_b, p0, p1)


def _attn_prompt_body(q_ref, kc_ref, kp_ref, vc_ref, vp_ref, tab_ref, o_ref, l_ref, *, tq):
    i = pl.program_id(2)
    first_tile = jnp.where(i == 0, 0, 1)
    head = _head_of_lane((BLOCK, GROUP_W))
    low_head = lax.broadcasted_iota(jnp.int32, (BLOCK, LANES), 1) < HEAD_DIM
    ones = jnp.ones((2 * BLOCK, LANES), BF16)
    for j in range(tq // BLOCK):
        q = q_ref[j * BLOCK:(j + 1) * BLOCK, :]
        qm = jnp.concatenate(
            [jnp.where(head == h, q, jnp.zeros_like(q)) for h in range(HEADS_PER_GROUP)], axis=0)
        if j == 0:
            k = jnp.concatenate([kp_ref[...], kc_ref[0:BLOCK, :]], axis=0)
            v = jnp.concatenate([vp_ref[...], vc_ref[0:BLOCK, :]], axis=0)
            tab = tab_ref[first_tile]
        else:
            k = kc_ref[(j - 1) * BLOCK:(j + 1) * BLOCK, :]
            v = vc_ref[(j - 1) * BLOCK:(j + 1) * BLOCK, :]
            tab = tab_ref[1]
        s = lax.dot_general(qm, k, _NT, preferred_element_type=F32) + tab
        m = jnp.max(s, axis=-1, keepdims=True)
        p = jnp.exp2(s - m).astype(BF16)
        halves_o, halves_l = [], []
        for pair in range(HEADS_PER_GROUP // 2):
            w = jnp.concatenate([v[:, pair * LANES:(pair + 1) * LANES], ones], axis=1)
            r = jnp.dot(p[2 * pair * BLOCK:2 * (pair + 1) * BLOCK, :], w,
                        preferred_element_type=F32)
            m0 = jnp.broadcast_to(m[2 * pair * BLOCK:(2 * pair + 1) * BLOCK], (BLOCK, LANES))
            m1 = jnp.broadcast_to(m[(2 * pair + 1) * BLOCK:2 * (pair + 1) * BLOCK], (BLOCK, LANES))
            acc = jnp.where(low_head, r[:BLOCK, :LANES], r[BLOCK:, :LANES])
            l = jnp.where(low_head, r[:BLOCK, LANES:], r[BLOCK:, LANES:])
            halves_o.append(acc / l)
            halves_l.append(jnp.where(low_head, m0, m1) + jnp.log2(l))
        o_ref[j * BLOCK:(j + 1) * BLOCK, :] = jnp.concatenate(halves_o, axis=1).astype(o_ref.dtype)
        l_ref[j * BLOCK:(j + 1) * BLOCK, :] = jnp.concatenate(halves_l, axis=1)


def _attn_prompt(qkv, tab, tq, name):
    b, dil, L, _ = qkv.shape
    tq = min(tq, L)
    cur_spec = lambda which: pl.BlockSpec(
        (None, None, tq, GROUP_W), lambda bi, r, i: (bi, r, i, which))
    prev_spec = lambda which: pl.BlockSpec(
        (None, None, BLOCK, GROUP_W),
        lambda bi, r, i: (bi, r, jnp.maximum(i * (tq // BLOCK) - 1, 0), which))
    out_spec = pl.BlockSpec((None, None, tq, GROUP_W), lambda bi, r, i: (bi, r, i, 0))
    return pl.pallas_call(
        functools.partial(_attn_prompt_body, tq=tq),
        grid=(b, dil, L // tq),
        in_specs=[cur_spec(0), cur_spec(1), prev_spec(1), cur_spec(2), prev_spec(2),
                  _resident(tab.shape)],
        out_specs=[out_spec, out_spec],
        out_shape=(jax.ShapeDtypeStruct((b, dil, L, GROUP_W), BF16),
                   jax.ShapeDtypeStruct((b, dil, L, GROUP_W), F32)),
        compiler_params=_params(3),
        name=name,
    )(qkv, qkv, qkv, qkv, qkv, tab)


DEC_T = 8
NEW_PAD = 16


def _attn_sample_body(qkv_ref, c0_ref, c1_ref, c2_ref, tc0_ref, tc1_ref, tc2_ref, tn_ref,
                      o0_ref, o1_ref, o2_ref, l0_ref, l1_ref, l2_ref):
    caches = (c0_ref, c1_ref, c2_ref)
    tcs = (tc0_ref, tc1_ref, tc2_ref)
    outs = ((o0_ref, l0_ref), (o1_ref, l1_ref), (o2_ref, l2_ref))
    pad = jnp.zeros((NEW_PAD - DEC_T, GROUP_W), F32)
    for g in range(N_GROUPS):
        c_ref = caches[g]
        base = g * QKV_W
        q = qkv_ref[:, base:base + GROUP_W]
        kn = jnp.concatenate([qkv_ref[:, base + GROUP_W:base + 2 * GROUP_W], pad], axis=0)
        vn = jnp.concatenate([qkv_ref[:, base + 2 * GROUP_W:base + QKV_W], pad], axis=0)
        qm = _per_head_rows(q).astype(BF16)
        s_n = lax.dot_general(qm, kn.astype(BF16), _NT, preferred_element_type=F32) + tn_ref[g]
        s_c = jnp.dot(qm, c_ref[0].astype(BF16), preferred_element_type=F32) + tcs[g][...]
        m = jnp.maximum(jnp.max(s_c, axis=-1, keepdims=True), jnp.max(s_n, axis=-1, keepdims=True))
        p_c = jnp.exp2(s_c - m)
        p_n = jnp.exp2(s_n - m)
        l = jnp.sum(p_c, axis=-1, keepdims=True) + jnp.sum(p_n, axis=-1, keepdims=True)
        acc = jnp.dot(p_n.astype(BF16), vn.astype(BF16), preferred_element_type=F32)
        acc = acc + lax.dot_general(p_c.astype(BF16), c_ref[1].astype(BF16), _NT,
                                    preferred_element_type=F32)
        res = acc / l
        lse = jnp.broadcast_to(m + jnp.log2(l), res.shape)
        o_ref, l_ref = outs[g]
        o_ref[...] = _head_diagonal(res, DEC_T)
        l_ref[...] = _head_diagonal(lse, DEC_T)


def _attn_sample(qkv, caches_t, tcs, tn):
    n = qkv.shape[0]
    bd = n // DEC_T
    cache_specs = [pl.BlockSpec((None,) + c.shape[1:], lambda bi: (bi, 0, 0, 0)) for c in caches_t]
    out_spec = pl.BlockSpec((DEC_T, GROUP_W), lambda bi: (bi, 0))
    out = jax.ShapeDtypeStruct((n, GROUP_W), F32)
    res = pl.pallas_call(
        _attn_sample_body,
        grid=(bd,),
        in_specs=[pl.BlockSpec((DEC_T, N_GROUPS * QKV_W), lambda bi: (bi, 0))] + cache_specs
        + [_resident(tc.shape) for tc in tcs] + [_resident(tn.shape)],
        out_specs=[out_spec] * 6,
        out_shape=(out,) * 6,
        compiler_params=_params(1),
        name="attn_sample",
    )(qkv, *caches_t, *tcs, tn)
    return res[:3], res[3:]


FF_CHUNK = 1024


def _natural_rows(ref, scratch_ref, slab):
    dil, rows, _ = ref.shape
    if dil == 1:
        return ref[0].astype(F32)
    halves = []
    for half in range(GROUP_W // LANES):
        for r in range(dil):
            scratch_ref[slab + half, pl.ds(r, rows, stride=dil), :] = (
                ref[r, :, half * LANES:(half + 1) * LANES].astype(F32))
        halves.append(scratch_ref[slab + half])
    return jnp.concatenate(halves, axis=1)


def _outproj_body(x_ref, o0_ref, o1_ref, o2_ref, l0_ref, l1_ref, l2_ref, sga_ref, mb_ref,
                  wao_ref, wo_ref, w1_ref, w2_ref, nmlp_ref, nfin_ref, y_ref, *scratch):
    if scratch:
        slabs = GROUP_W // LANES
        os_ = [_natural_rows(ref, scratch[0], slabs * k) for k, ref in
               enumerate((o0_ref, o1_ref, o2_ref))]
        ls_ = [_natural_rows(ref, scratch[0], slabs * (3 + k)) for k, ref in
               enumerate((l0_ref, l1_ref, l2_ref))]
    else:
        os_ = [ref[...] for ref in (o0_ref, o1_ref, o2_ref)]
        ls_ = [ref[...] for ref in (l0_ref, l1_ref, l2_ref)]
    (o0, o1, o2), (l0, l1, l2) = os_, ls_
    m = jnp.maximum(l0, jnp.maximum(l1, l2))
    e0, e1, e2 = jnp.exp2(l0 - m), jnp.exp2(l1 - m), jnp.exp2(l2 - m)
    att = (e0 * o0 + e1 * o1 + e2 * o2) / (e0 + e1 + e2)
    branch_a = jnp.dot(att.astype(BF16), wao_ref[...], preferred_element_type=F32)
    mix = sga_ref[...].astype(F32) * branch_a + mb_ref[...].astype(F32)
    x1 = x_ref[...] + jnp.dot(mix.astype(BF16), wo_ref[...], preferred_element_type=F32)
    h = _rmsnorm(x1, nmlp_ref[...]).astype(BF16)
    acc = x1
    for c in range(D_FF // FF_CHUNK):
        a = jnp.dot(h, w1_ref[:, c * FF_CHUNK:(c + 1) * FF_CHUNK], preferred_element_type=F32)
        a = jnp.square(jnp.maximum(a, 0.0)).astype(BF16)
        acc = acc + jnp.dot(a, w2_ref[c * FF_CHUNK:(c + 1) * FF_CHUNK, :],
                            preferred_element_type=F32)
    y_ref[...] = _rmsnorm(acc, nfin_ref[...])


def _outproj_prompt(x, os_, ls_, sga, mb, weights, tm):
    b, s, _ = x.shape
    tok = pl.BlockSpec((None, tm, D_MODEL), lambda bi, i: (bi, i, 0))
    grouped = [pl.BlockSpec((None, dil, tm // dil, GROUP_W), lambda bi, i: (bi, 0, i, 0))
               for _, dil in GROUPS]
    return pl.pallas_call(
        _outproj_body,
        grid=(b, s // tm),
        in_specs=[tok] + grouped + grouped + [tok, tok] + [_resident(w.shape) for w in weights],
        out_specs=tok,
        out_shape=jax.ShapeDtypeStruct((b, s, D_MODEL), F32),
        scratch_shapes=[pltpu.VMEM((2 * N_GROUPS * GROUP_W // LANES, tm, LANES), F32)],
        compiler_params=_params(2),
        name="outproj_ffn_prompt",
    )(x, *os_, *ls_, sga, mb, *weights)


def _outproj_sample(x, os_, ls_, sga, mb, weights, tm):
    n = x.shape[0]
    tok = lambda w: pl.BlockSpec((tm, w), lambda i: (i, 0))
    return pl.pallas_call(
        _outproj_body,
        grid=(n // tm,),
        in_specs=[tok(D_MODEL)] + [tok(GROUP_W)] * 6 + [tok(D_MODEL), tok(D_MODEL)]
        + [_resident(w.shape) for w in weights],
        out_specs=tok(D_MODEL),
        out_shape=jax.ShapeDtypeStruct((n, D_MODEL), F32),
        compiler_params=_params(1),
        name="outproj_ffn_sample",
    )(x, *os_, *ls_, sga, mb, *weights)


def _t5_bucket(dist):
    n = np.asarray(dist)
    max_exact = N_BUCKETS // 2
    large = max_exact + (np.log(np.maximum(n, 1) / max_exact) / np.log(MAX_DISTANCE / max_exact)
                         * (N_BUCKETS - max_exact)).astype(np.int32)
    large = np.minimum(large, N_BUCKETS - 1)
    return np.where(n < max_exact, n, large).astype(np.int32)


def _group_bias(rel_bias, g):
    dil = GROUPS[g][1]
    buckets = _t5_bucket(np.arange(KEYS_PER_QUERY + 1) * dil)
    bias = rel_bias[buckets][:, g * HEADS_PER_GROUP:(g + 1) * HEADS_PER_GROUP].astype(F32)
    return bias * LOG2E


def _toeplitz(vec, n_rows, n_cols, offset):
    h, n = vec.shape
    start = n - 1 - offset
    assert start - (n_rows - 1) >= 0 and start + n_cols <= n
    w = max(n, start + n_cols + 1)
    rev = jnp.pad(vec[:, ::-1], ((0, 0), (0, w - n)))
    skew = jnp.tile(rev, (1, n_rows))[:, :n_rows * (w - 1)].reshape(h, n_rows, w - 1)
    return skew[:, :, start:start + n_cols].reshape(h * n_rows, n_cols)


def _pad_neg(vec, before, after):
    h = vec.shape[0]
    return jnp.concatenate([jnp.full((h, before), NEG, F32), vec, jnp.full((h, after), NEG, F32)],
                           axis=1)


def _prompt_table(bias):
    by_stride = _pad_neg(bias.T, BLOCK - 1, BLOCK - 1)
    tab = _toeplitz(by_stride, BLOCK, 2 * BLOCK, 2 * BLOCK - 1)
    cur = (np.arange(2 * BLOCK) >= BLOCK)[None, :]
    return jnp.stack([jnp.where(cur, tab, NEG), tab])


def _sample_tables(bias, win, dil):
    h = bias.shape[1]
    spread = jnp.concatenate([bias.T[:, :, None], jnp.full((h, KEYS_PER_QUERY + 1, dil - 1), NEG)],
                             axis=2).reshape(h, -1)[:, :KEYS_PER_QUERY * dil + 1]
    tc = _toeplitz(_pad_neg(spread, 0, DEC_T - 1), DEC_T, win, win)
    tn = _toeplitz(_pad_neg(spread, NEW_PAD - 1, 0), DEC_T, NEW_PAD, NEW_PAD - 1)
    return tc, tn


def _cache_transposed(cache):
    bd, wb = cache.shape[:2]
    return jnp.transpose(cache, (0, 2, 3, 4, 1)).reshape(bd, 2, GROUP_W, wb)


def _cache_rows(kvt):
    b, _, _, win = kvt.shape
    kvt = kvt.reshape(b, 2, HEADS_PER_GROUP, HEAD_DIM, win)
    return jnp.transpose(kvt, (0, 4, 1, 2, 3))[None]


TM_INPROJ = 512
TM_OUTPROJ = 512
TM_CACHE_ROWS = 1024
TQ_ATTN = 1024


def kernel(x_prompt, x_sample, cache_kv_g0, cache_kv_g1, cache_kv_g2, state_conv, w_in, w_att_out,
           w_conv_out, w_o, conv_w, conv_b, rel_bias, norm_mix, norm_mlp, w_ff1, w_ff2, norm_final):
    assert w_in.shape[0] == 1, "one layer"
    b, s, _ = x_prompt.shape
    bd, t, _ = x_sample.shape
    assert t == DEC_T
    wp = w_in[0].astype(BF16)
    wkvt = jnp.stack([
        jnp.concatenate([w_in[0][:, ATT_W + g * GROUP_W:ATT_W + (g + 1) * GROUP_W],
                         w_in[0][:, 2 * ATT_W + g * GROUP_W:2 * ATT_W + (g + 1) * GROUP_W]],
                        axis=1).T for g in range(N_GROUPS)]).astype(BF16)
    wco = w_conv_out[0].astype(BF16)
    wao, wo = w_att_out[0].astype(BF16), w_o[0].astype(BF16)
    w1, w2 = w_ff1[0].astype(BF16), w_ff2[0].astype(BF16)
    nmix, nmlp, nfin = norm_mix[0][None], norm_mlp[0][None], norm_final[None]
    cw, cb = conv_w[0], conv_b[0][None]
    biases = [_group_bias(rel_bias, g) for g in range(N_GROUPS)]

    weights = (wao, wo, w1, w2, nmlp, nfin)

    qkv0, qkv1, qkv2, sga, mb, utail = _inproj_prompt(x_prompt, wp, wco, nmix, cw, cb, TM_INPROJ)
    kvts = [_cache_rows_prompt(x_prompt, wkvt[g], nmix, win, TM_CACHE_ROWS, f"cache_rows_g{g}")
            for g, (win, _) in enumerate(GROUPS)]
    os_, ls_ = [], []
    for g, qkv in enumerate((qkv0, qkv1, qkv2)):
        o, lse = _attn_prompt(qkv, _prompt_table(biases[g]), TQ_ATTN, f"attn_prompt_g{g}")
        os_.append(o)
        ls_.append(lse)
    y_prompt = _outproj_prompt(x_prompt, os_, ls_, sga, mb, weights, TM_OUTPROJ)
    kv_prompt = [_cache_rows(kvt) for kvt in kvts]
    conv_prompt = utail[:, 6:8][None]

    n = bd * t
    st = state_conv[0]
    p0 = jnp.repeat(st[:, 0], t, axis=0)
    p1 = jnp.repeat(st[:, 1], t, axis=0)
    qkv_s, sga_s, mb_s, u_s = _inproj_sample(
        x_sample.reshape(n, D_MODEL), wp, wco, nmix, cw, cb, p0, p1, min(TM_INPROJ, n))
    tabs = [_sample_tables(biases[g], *GROUPS[g]) for g in range(N_GROUPS)]
    caches_t = [_cache_transposed(c[0]) for c in (cache_kv_g0, cache_kv_g1, cache_kv_g2)]
    os_s, ls_s = _attn_sample(qkv_s, caches_t, [tb[0] for tb in tabs],
                              jnp.stack([tb[1] for tb in tabs]))
    y_sample = _outproj_sample(x_sample.reshape(n, D_MODEL), os_s, ls_s, sga_s, mb_s, weights,
                               min(TM_OUTPROJ, n))
    kv_sample = [qkv_s[:, g * QKV_W + GROUP_W:(g + 1) * QKV_W].reshape(
        1, bd, t, 2, HEADS_PER_GROUP, HEAD_DIM) for g in range(N_GROUPS)]
    conv_sample = u_s.reshape(bd, t, CONV_W)[:, t - 2:][None]

    return (y_prompt, y_sample.reshape(bd, t, D_MODEL),
            kv_prompt[0], kv_prompt[1], kv_prompt[2], conv_prompt,
            kv_sample[0], kv_sample[1], kv_sample[2], conv_sample)
```

```python
import functools

import jax
import jax.numpy as jnp
import numpy as np
from jax import lax
from jax.experimental import pallas as pl
from jax.experimental.pallas import tpu as pltpu

D_MODEL = 1024
HEAD_DIM = 64
HEADS_PER_GROUP = 4
GROUPS = ((128, 1), (512, 4), (2048, 16))
N_GROUPS = 3
GROUP_W = HEADS_PER_GROUP * HEAD_DIM
QKV_W = 3 * GROUP_W
ATT_W = N_GROUPS * GROUP_W
CONV_W = D_MODEL // 2
D_FF = 4 * D_MODEL
N_BUCKETS = 32
MAX_DISTANCE = 2048
KEYS_PER_QUERY = 128
BLOCK = 128
LANES = 128
EPS = 1e-6
SCALE = HEAD_DIM ** -0.5
LOG2E = float(np.log2(np.e))
NEG = -1e30

OFF_CONV = 3 * ATT_W
OFF_GA = OFF_CONV + 3 * CONV_W
OFF_GB = OFF_GA + D_MODEL
PROJ_W = OFF_GB + D_MODEL

VMEM_LIMIT_V7X = 56 * 1024 * 1024
F32 = jnp.float32
BF16 = jnp.bfloat16
_NT = (((1,), (1,)), ((), ()))


def _params(n_axes):
    return pltpu.CompilerParams(
        dimension_semantics=("arbitrary",) * n_axes, vmem_limit_bytes=VMEM_LIMIT_V7X)


def _resident(shape):
    return pl.BlockSpec(shape, lambda *_: (0,) * len(shape), pipeline_mode=pl.Buffered(1))


def _rmsnorm(x, g):
    y = x * lax.rsqrt(jnp.mean(x * x, axis=-1, keepdims=True) + EPS)
    return y * g


def _head_of_lane(shape):
    return lax.broadcasted_iota(jnp.int32, shape, len(shape) - 1) // HEAD_DIM


def _per_head_rows(x):
    head = _head_of_lane(x.shape)
    return jnp.concatenate([jnp.where(head == h, x, 0.0) for h in range(HEADS_PER_GROUP)], axis=0)


def _head_diagonal(x, t):
    head = _head_of_lane((t, GROUP_W))
    out = x[0:t, :]
    for h in range(1, HEADS_PER_GROUP):
        out = jnp.where(head == h, x[h * t:(h + 1) * t, :], out)
    return out


def _inproj_body(*refs, sample):
    if sample:
        (x_ref, win_ref, wco_ref, nm_ref, cw_ref, cb_ref, p0_ref, p1_ref,
         qkv_ref, sga_ref, mb_ref, u_ref) = refs
    else:
        (x_ref, win_ref, wco_ref, nm_ref, cw_ref, cb_ref,
         qkv0_ref, qkv1_ref, qkv2_ref, sga_ref, mb_ref, u_ref, carry_ref, perm_ref) = refs
        qkv_refs = (qkv0_ref, qkv1_ref, qkv2_ref)
    x = x_ref[...]
    tm = x.shape[0]
    xn = _rmsnorm(x, nm_ref[...]).astype(BF16)

    def proj(lo, hi):
        return jnp.dot(xn, win_ref[:, lo:hi], preferred_element_type=F32)

    pc = proj(OFF_CONV, OFF_GA)
    ga = proj(OFF_GA, OFF_GB)
    sga_ref[...] = jax.nn.sigmoid(ga).astype(sga_ref.dtype)
    sgb = jax.nn.sigmoid(proj(OFF_GB, PROJ_W))
    dilated = {which: proj(which * ATT_W + GROUP_W, (which + 1) * ATT_W) for which in (0, 1)}
    cb, cc, ch = pc[:, :CONV_W], pc[:, CONV_W:2 * CONV_W], pc[:, 2 * CONV_W:]
    u = cc * ch
    row = lax.broadcasted_iota(jnp.int32, (tm, 1), 0)
    if sample:
        p0, p1 = p0_ref[...], p1_ref[...]
        rs = jnp.bitwise_and(row, 7)
        u_ref[...] = u
    else:
        seq_start = pl.program_id(1) == 0
        p0 = jnp.where(seq_start, 0.0, carry_ref[6:7, :])
        p1 = jnp.where(seq_start, 0.0, carry_ref[7:8, :])
        rs = row
        u_ref[...] = u[tm - 8:, :]
    um1 = jnp.where(rs == 0, p1, pltpu.roll(u, 1, 0))
    um2 = jnp.where(rs == 0, p0, jnp.where(rs == 1, p1, pltpu.roll(u, 2, 0)))
    if not sample:
        carry_ref[...] = u[tm - 8:, :]
    z = cb_ref[...] + cw_ref[0:1, :] * um2 + cw_ref[1:2, :] * um1 + cw_ref[2:3, :] * u
    branch_b = jnp.dot((cb * z).astype(BF16), wco_ref[...], preferred_element_type=F32)
    mb_ref[...] = (sgb * branch_b).astype(mb_ref.dtype)
    dilated[2] = proj(2 * ATT_W + GROUP_W, 3 * ATT_W)

    slab = 0
    for g, (_, dil) in reversed(list(enumerate(GROUPS))):
        for which in range(3):
            if dil == 1:
                p = proj(which * ATT_W, which * ATT_W + GROUP_W)
            else:
                p = dilated[which][:, (g - 1) * GROUP_W:g * GROUP_W]
            if which == 0:
                p = p * (SCALE * LOG2E)
            dst = which * GROUP_W
            if sample:
                qkv_ref[:, g * QKV_W + dst:g * QKV_W + dst + GROUP_W] = p
            elif dil == 1:
                qkv_refs[g][0, :, dst:dst + GROUP_W] = p.astype(BF16)
            else:
                for half in range(GROUP_W // LANES):
                    perm_ref[slab] = p[:, half * LANES:(half + 1) * LANES]
                    for r in range(dil):
                        rows = perm_ref[slab, pl.ds(r, tm // dil, stride=dil), :]
                        lo = dst + half * LANES
                        qkv_refs[g][r, :, lo:lo + LANES] = rows.astype(BF16)
                    slab += 1


def _inproj_prompt(x, w_in, w_conv_out, norm_mix, conv_w, conv_b, tm):
    b, s, _ = x.shape
    nt = s // tm
    tok = lambda w: pl.BlockSpec((None, tm, w), lambda bi, i: (bi, i, 0))
    qkv_specs = [pl.BlockSpec((None, dil, tm // dil, QKV_W), lambda bi, i: (bi, 0, i, 0))
                 for _, dil in GROUPS]
    n_slabs = sum(3 * GROUP_W // LANES for _, dil in GROUPS if dil > 1)
    out_shape = (
        [jax.ShapeDtypeStruct((b, dil, s // dil, QKV_W), BF16) for _, dil in GROUPS]
        + [jax.ShapeDtypeStruct((b, s, D_MODEL), BF16),
           jax.ShapeDtypeStruct((b, s, D_MODEL), BF16),
           jax.ShapeDtypeStruct((b, 8, CONV_W), F32)])
    return pl.pallas_call(
        functools.partial(_inproj_body, sample=False),
        grid=(b, nt),
        in_specs=[tok(D_MODEL), _resident(w_in.shape), _resident(w_conv_out.shape),
                  _resident(norm_mix.shape), _resident(conv_w.shape), _resident(conv_b.shape)],
        out_specs=qkv_specs
        + [tok(D_MODEL), tok(D_MODEL), pl.BlockSpec((None, 8, CONV_W), lambda bi, i: (bi, 0, 0))],
        out_shape=out_shape,
        scratch_shapes=[pltpu.VMEM((8, CONV_W), F32), pltpu.VMEM((n_slabs, tm, LANES), F32)],
        compiler_params=_params(2),
        name="inproj_prompt",
    )(x, w_in, w_conv_out, norm_mix, conv_w, conv_b)


def _cache_rows_body(x_ref, wk_ref, wv_ref, nm_ref, kvt_ref):
    xn = _rmsnorm(x_ref[...], nm_ref[...]).astype(BF16)
    for which, w_ref in enumerate((wk_ref, wv_ref)):
        kv = jnp.dot(xn, w_ref[...], preferred_element_type=F32)
        kvt_ref[which] = kv.T


def _cache_rows_prompt(x, w_in, norm_mix, g, tm, name):
    b, s, _ = x.shape
    rows = GROUPS[g][0]
    tm = min(tm, rows)
    first = (s - rows) // tm
    col_blocks = ATT_W // GROUP_W
    w_spec = lambda which: pl.BlockSpec((D_MODEL, GROUP_W),
                                        lambda bi, i: (0, which * col_blocks + g))
    return pl.pallas_call(
        _cache_rows_body,
        grid=(b, rows // tm),
        in_specs=[pl.BlockSpec((None, tm, D_MODEL), lambda bi, i: (bi, first + i, 0)),
                  w_spec(1), w_spec(2), _resident(norm_mix.shape)],
        out_specs=pl.BlockSpec((None, 2, GROUP_W, tm), lambda bi, i: (bi, 0, 0, i)),
        out_shape=jax.ShapeDtypeStruct((b, 2, GROUP_W, rows), F32),
        compiler_params=_params(2),
        name=name,
    )(x, w_in, w_in, norm_mix)


def _inproj_sample(x, w_in, w_conv_out, norm_mix, conv_w, conv_b, p0, p1, tm):
    n = x.shape[0]
    tok = lambda w: pl.BlockSpec((tm, w), lambda i: (i, 0))
    out_shape = (
        jax.ShapeDtypeStruct((n, N_GROUPS * QKV_W), F32),
        jax.ShapeDtypeStruct((n, D_MODEL), BF16),
        jax.ShapeDtypeStruct((n, D_MODEL), BF16),
        jax.ShapeDtypeStruct((n, CONV_W), F32),
    )
    return pl.pallas_call(
        functools.partial(_inproj_body, sample=True),
        grid=(n // tm,),
        in_specs=[tok(D_MODEL), _resident(w_in.shape), _resident(w_conv_out.shape),
                  _resident(norm_mix.shape), _resident(conv_w.shape), _resident(conv_b.shape),
                  tok(CONV_W), tok(CONV_W)],
        out_specs=[tok(N_GROUPS * QKV_W), tok(D_MODEL), tok(D_MODEL), tok(CONV_W)],
        out_shape=out_shape,
        compiler_params=_params(1),
        name="inproj_sample",
    )(x, w_in, w_conv_out, norm_mix, conv_w, conv_b, p0, p1)


def _attn_prompt_body(q_ref, kc_ref, kp_ref, vc_ref, vp_ref, tab_ref, o_ref, l_ref):
    i = pl.program_id(2)
    first_tile = jnp.where(i == 0, 0, 1)
    head = _head_of_lane((BLOCK, GROUP_W))
    low_head = lax.broadcasted_iota(jnp.int32, (BLOCK, LANES), 1) < HEAD_DIM
    ones = jnp.ones((2 * BLOCK, LANES), BF16)
    n_cls, tq, _ = q_ref.shape
    for c, j in [(c, j) for c in range(n_cls) for j in range(tq // BLOCK)]:
        q = q_ref[c, j * BLOCK:(j + 1) * BLOCK, :]
        qm = jnp.concatenate(
            [jnp.where(head == h, q, jnp.zeros_like(q)) for h in range(HEADS_PER_GROUP)], axis=0)
        if j == 0:
            k = jnp.concatenate([kp_ref[c], kc_ref[c, 0:BLOCK, :]], axis=0)
            v = jnp.concatenate([vp_ref[c], vc_ref[c, 0:BLOCK, :]], axis=0)
            tab = tab_ref[first_tile]
        else:
            k = kc_ref[c, (j - 1) * BLOCK:(j + 1) * BLOCK, :]
            v = vc_ref[c, (j - 1) * BLOCK:(j + 1) * BLOCK, :]
            tab = tab_ref[1]
        s = lax.dot_general(qm, k, _NT, preferred_element_type=F32) + tab
        m = jnp.max(s, axis=-1, keepdims=True)
        p = jnp.exp2(s - m).astype(BF16)
        halves_o, halves_l = [], []
        for pair in range(HEADS_PER_GROUP // 2):
            w = jnp.concatenate([v[:, pair * LANES:(pair + 1) * LANES], ones], axis=1)
            r = jnp.dot(p[2 * pair * BLOCK:2 * (pair + 1) * BLOCK, :], w,
                        preferred_element_type=F32)
            m0 = jnp.broadcast_to(m[2 * pair * BLOCK:(2 * pair + 1) * BLOCK], (BLOCK, LANES))
            m1 = jnp.broadcast_to(m[(2 * pair + 1) * BLOCK:2 * (pair + 1) * BLOCK], (BLOCK, LANES))
            acc = jnp.where(low_head, r[:BLOCK, :LANES], r[BLOCK:, :LANES])
            l = jnp.where(low_head, r[:BLOCK, LANES:], r[BLOCK:, LANES:])
            halves_o.append(acc / l)
            halves_l.append(jnp.where(low_head, m0, m1) + jnp.log2(l))
        o_ref[c, j * BLOCK:(j + 1) * BLOCK, :] = (
            jnp.concatenate(halves_o, axis=1).astype(o_ref.dtype))
        l_ref[c, j * BLOCK:(j + 1) * BLOCK, :] = jnp.concatenate(halves_l, axis=1)


def _attn_prompt(qkv, tab, rows_per_step, name):
    b, dil, L, _ = qkv.shape
    tq = min(rows_per_step, L)
    n_cls = min(rows_per_step // tq, dil)
    cur_spec = lambda which: pl.BlockSpec(
        (None, n_cls, tq, GROUP_W), lambda bi, r, i: (bi, r, i, which))
    prev_spec = lambda which: pl.BlockSpec(
        (None, n_cls, BLOCK, GROUP_W),
        lambda bi, r, i: (bi, r, jnp.maximum(i * (tq // BLOCK) - 1, 0), which))
    out_spec = pl.BlockSpec((None, n_cls, tq, GROUP_W), lambda bi, r, i: (bi, r, i, 0))
    return pl.pallas_call(
        _attn_prompt_body,
        grid=(b, dil // n_cls, L // tq),
        in_specs=[cur_spec(0), cur_spec(1), prev_spec(1), cur_spec(2), prev_spec(2),
                  _resident(tab.shape)],
        out_specs=[out_spec, out_spec],
        out_shape=(jax.ShapeDtypeStruct((b, dil, L, GROUP_W), BF16),
                   jax.ShapeDtypeStruct((b, dil, L, GROUP_W), F32)),
        compiler_params=_params(3),
        name=name,
    )(qkv, qkv, qkv, qkv, qkv, tab)


DEC_T = 8
NEW_PAD = 16
SEQS_PER_SAMPLE_STEP = 2


def _attn_sample_body(qkv_ref, c0_ref, c1_ref, c2_ref, tc0_ref, tc1_ref, tc2_ref, tn_ref,
                      o0_ref, o1_ref, o2_ref, l0_ref, l1_ref, l2_ref):
    caches = (c0_ref, c1_ref, c2_ref)
    tcs = (tc0_ref, tc1_ref, tc2_ref)
    outs = ((o0_ref, l0_ref), (o1_ref, l1_ref), (o2_ref, l2_ref))
    pad = jnp.zeros((NEW_PAD - DEC_T, GROUP_W), F32)
    work = [(i, g) for i in range(c0_ref.shape[0]) for g in range(N_GROUPS)]
    scores = []
    for i, g in work:
        rows = slice(i * DEC_T, (i + 1) * DEC_T)
        base = g * QKV_W
        q = qkv_ref[rows, base:base + GROUP_W]
        kn = jnp.concatenate([qkv_ref[rows, base + GROUP_W:base + 2 * GROUP_W], pad], axis=0)
        qm = _per_head_rows(q).astype(BF16)
        s_n = lax.dot_general(qm, kn.astype(BF16), _NT, preferred_element_type=F32) + tn_ref[g]
        s_c = jnp.dot(qm, caches[g][i, 0].astype(BF16), preferred_element_type=F32) + tcs[g][...]
        scores.append((s_c, s_n))
    for (i, g), (s_c, s_n) in zip(work, scores):
        rows = slice(i * DEC_T, (i + 1) * DEC_T)
        base = g * QKV_W
        vn = jnp.concatenate([qkv_ref[rows, base + 2 * GROUP_W:base + QKV_W], pad], axis=0)
        m = jnp.maximum(jnp.max(s_c, axis=-1, keepdims=True), jnp.max(s_n, axis=-1, keepdims=True))
        p_c = jnp.exp2(s_c - m)
        p_n = jnp.exp2(s_n - m)
        l = jnp.sum(p_c, axis=-1, keepdims=True) + jnp.sum(p_n, axis=-1, keepdims=True)
        acc = jnp.dot(p_n.astype(BF16), vn.astype(BF16), preferred_element_type=F32)
        acc = acc + lax.dot_general(p_c.astype(BF16), caches[g][i, 1].astype(BF16), _NT,
                                    preferred_element_type=F32)
        res = acc / l
        lse = jnp.broadcast_to(m + jnp.log2(l), res.shape)
        o_ref, l_ref = outs[g]
        o_ref[rows, :] = _head_diagonal(res, DEC_T)
        l_ref[rows, :] = _head_diagonal(lse, DEC_T)


def _attn_sample(qkv, caches_t, tcs, tn):
    n = qkv.shape[0]
    bd = n // DEC_T
    nb = SEQS_PER_SAMPLE_STEP
    cache_specs = [pl.BlockSpec((nb,) + c.shape[1:], lambda bi: (bi, 0, 0, 0)) for c in caches_t]
    out_spec = pl.BlockSpec((nb * DEC_T, GROUP_W), lambda bi: (bi, 0))
    out = jax.ShapeDtypeStruct((n, GROUP_W), F32)
    res = pl.pallas_call(
        _attn_sample_body,
        grid=(bd // nb,),
        in_specs=[pl.BlockSpec((nb * DEC_T, N_GROUPS * QKV_W), lambda bi: (bi, 0))] + cache_specs
        + [_resident(tc.shape) for tc in tcs] + [_resident(tn.shape)],
        out_specs=[out_spec] * 6,
        out_shape=(out,) * 6,
        compiler_params=_params(1),
        name="attn_sample",
    )(qkv, *caches_t, *tcs, tn)
    return res[:3], res[3:]


FF_CHUNK = 1024


def _natural_rows(ref, scratch_ref, slab):
    dil, rows, _ = ref.shape
    if dil == 1:
        return ref[0].astype(F32)
    halves = []
    for half in range(GROUP_W // LANES):
        for r in range(dil):
            scratch_ref[slab + half, pl.ds(r, rows, stride=dil), :] = (
                ref[r, :, half * LANES:(half + 1) * LANES].astype(F32))
        halves.append(scratch_ref[slab + half])
    return jnp.concatenate(halves, axis=1)


def _outproj_body(x_ref, o0_ref, o1_ref, o2_ref, l0_ref, l1_ref, l2_ref, sga_ref, mb_ref,
                  wao_ref, wo_ref, w1_ref, w2_ref, nmlp_ref, nfin_ref, y_ref, *scratch):
    if scratch:
        slabs = GROUP_W // LANES
        os_ = [_natural_rows(ref, scratch[0], slabs * k) for k, ref in
               enumerate((o0_ref, o1_ref, o2_ref))]
        ls_ = [_natural_rows(ref, scratch[0], slabs * (3 + k)) for k, ref in
               enumerate((l0_ref, l1_ref, l2_ref))]
    else:
        os_ = [ref[...] for ref in (o0_ref, o1_ref, o2_ref)]
        ls_ = [ref[...] for ref in (l0_ref, l1_ref, l2_ref)]
    (o0, o1, o2), (l0, l1, l2) = os_, ls_
    m = jnp.maximum(l0, jnp.maximum(l1, l2))
    e0, e1, e2 = jnp.exp2(l0 - m), jnp.exp2(l1 - m), jnp.exp2(l2 - m)
    att = (e0 * o0 + e1 * o1 + e2 * o2) / (e0 + e1 + e2)
    branch_a = jnp.dot(att.astype(BF16), wao_ref[...], preferred_element_type=F32)
    mix = sga_ref[...].astype(F32) * branch_a + mb_ref[...].astype(F32)
    x1 = x_ref[...] + jnp.dot(mix.astype(BF16), wo_ref[...], preferred_element_type=F32)
    h = _rmsnorm(x1, nmlp_ref[...]).astype(BF16)
    acc = x1
    for c in range(D_FF // FF_CHUNK):
        a = jnp.dot(h, w1_ref[:, c * FF_CHUNK:(c + 1) * FF_CHUNK], preferred_element_type=F32)
        a = jnp.square(jnp.maximum(a, 0.0)).astype(BF16)
        acc = acc + jnp.dot(a, w2_ref[c * FF_CHUNK:(c + 1) * FF_CHUNK, :],
                            preferred_element_type=F32)
    y_ref[...] = _rmsnorm(acc, nfin_ref[...])


def _outproj_prompt(x, os_, ls_, sga, mb, weights, tm):
    b, s, _ = x.shape
    tok = pl.BlockSpec((None, tm, D_MODEL), lambda bi, i: (bi, i, 0))
    grouped = [pl.BlockSpec((None, dil, tm // dil, GROUP_W), lambda bi, i: (bi, 0, i, 0))
               for _, dil in GROUPS]
    return pl.pallas_call(
        _outproj_body,
        grid=(b, s // tm),
        in_specs=[tok] + grouped + grouped + [tok, tok] + [_resident(w.shape) for w in weights],
        out_specs=tok,
        out_shape=jax.ShapeDtypeStruct((b, s, D_MODEL), F32),
        scratch_shapes=[pltpu.VMEM((2 * N_GROUPS * GROUP_W // LANES, tm, LANES), F32)],
        compiler_params=_params(2),
        name="outproj_ffn_prompt",
    )(x, *os_, *ls_, sga, mb, *weights)


def _outproj_sample(x, os_, ls_, sga, mb, weights, tm):
    n = x.shape[0]
    tok = lambda w: pl.BlockSpec((tm, w), lambda i: (i, 0))
    return pl.pallas_call(
        _outproj_body,
        grid=(n // tm,),
        in_specs=[tok(D_MODEL)] + [tok(GROUP_W)] * 6 + [tok(D_MODEL), tok(D_MODEL)]
        + [_resident(w.shape) for w in weights],
        out_specs=tok(D_MODEL),
        out_shape=jax.ShapeDtypeStruct((n, D_MODEL), F32),
        compiler_params=_params(1),
        name="outproj_ffn_sample",
    )(x, *os_, *ls_, sga, mb, *weights)


def _t5_bucket(dist):
    n = np.asarray(dist)
    max_exact = N_BUCKETS // 2
    large = max_exact + (np.log(np.maximum(n, 1) / max_exact) / np.log(MAX_DISTANCE / max_exact)
                         * (N_BUCKETS - max_exact)).astype(np.int32)
    large = np.minimum(large, N_BUCKETS - 1)
    return np.where(n < max_exact, n, large).astype(np.int32)


def _group_bias(rel_bias, g):
    dil = GROUPS[g][1]
    buckets = _t5_bucket(np.arange(KEYS_PER_QUERY + 1) * dil)
    bias = rel_bias[buckets][:, g * HEADS_PER_GROUP:(g + 1) * HEADS_PER_GROUP].astype(F32)
    return bias * LOG2E


def _toeplitz(vec, n_rows, n_cols, offset):
    h, n = vec.shape
    start = n - 1 - offset
    assert start - (n_rows - 1) >= 0 and start + n_cols <= n
    w = max(n, start + n_cols + 1)
    rev = jnp.pad(vec[:, ::-1], ((0, 0), (0, w - n)))
    skew = jnp.tile(rev, (1, n_rows))[:, :n_rows * (w - 1)].reshape(h, n_rows, w - 1)
    return skew[:, :, start:start + n_cols].reshape(h * n_rows, n_cols)


def _pad_neg(vec, before, after):
    h = vec.shape[0]
    return jnp.concatenate([jnp.full((h, before), NEG, F32), vec, jnp.full((h, after), NEG, F32)],
                           axis=1)


def _prompt_table(bias):
    by_stride = _pad_neg(bias.T, BLOCK - 1, BLOCK - 1)
    tab = _toeplitz(by_stride, BLOCK, 2 * BLOCK, 2 * BLOCK - 1)
    cur = (np.arange(2 * BLOCK) >= BLOCK)[None, :]
    return jnp.stack([jnp.where(cur, tab, NEG), tab])


def _sample_tables(bias, win, dil):
    h = bias.shape[1]
    spread = jnp.concatenate([bias.T[:, :, None], jnp.full((h, KEYS_PER_QUERY + 1, dil - 1), NEG)],
                             axis=2).reshape(h, -1)[:, :KEYS_PER_QUERY * dil + 1]
    tc = _toeplitz(_pad_neg(spread, 0, DEC_T - 1), DEC_T, win, win)
    tn = _toeplitz(_pad_neg(spread, NEW_PAD - 1, 0), DEC_T, NEW_PAD, NEW_PAD - 1)
    return tc, tn


def _cache_transposed(cache):
    bd, wb = cache.shape[:2]
    return jnp.transpose(cache, (0, 2, 3, 4, 1)).reshape(bd, 2, GROUP_W, wb)


def _cache_rows(kvt):
    b, _, _, win = kvt.shape
    kvt = kvt.reshape(b, 2, HEADS_PER_GROUP, HEAD_DIM, win)
    return jnp.transpose(kvt, (0, 4, 1, 2, 3))[None]


TM_INPROJ = 512
TM_OUTPROJ = 512
TM_CACHE_ROWS = 1024
TQ_ATTN = 4096


def kernel(x_prompt, x_sample, cache_kv_g0, cache_kv_g1, cache_kv_g2, state_conv, w_in, w_att_out,
           w_conv_out, w_o, conv_w, conv_b, rel_bias, norm_mix, norm_mlp, w_ff1, w_ff2, norm_final):
    assert w_in.shape[0] == 1, "one layer"
    b, s, _ = x_prompt.shape
    bd, t, _ = x_sample.shape
    assert t == DEC_T
    wp = w_in[0].astype(BF16)
    wco = w_conv_out[0].astype(BF16)
    wao, wo = w_att_out[0].astype(BF16), w_o[0].astype(BF16)
    w1, w2 = w_ff1[0].astype(BF16), w_ff2[0].astype(BF16)
    nmix, nmlp, nfin = norm_mix[0][None], norm_mlp[0][None], norm_final[None]
    cw, cb = conv_w[0], conv_b[0][None]
    biases = [_group_bias(rel_bias, g) for g in range(N_GROUPS)]

    weights = (wao, wo, w1, w2, nmlp, nfin)

    qkv0, qkv1, qkv2, sga, mb, utail = _inproj_prompt(x_prompt, wp, wco, nmix, cw, cb, TM_INPROJ)
    kvts = [_cache_rows_prompt(x_prompt, wp, nmix, g, TM_CACHE_ROWS, f"cache_rows_g{g}")
            for g in range(N_GROUPS)]
    os_, ls_ = [], []
    for g, qkv in enumerate((qkv0, qkv1, qkv2)):
        o, lse = _attn_prompt(qkv, _prompt_table(biases[g]), TQ_ATTN, f"attn_prompt_g{g}")
        os_.append(o)
        ls_.append(lse)
    y_prompt = _outproj_prompt(x_prompt, os_, ls_, sga, mb, weights, TM_OUTPROJ)
    kv_prompt = [_cache_rows(kvt) for kvt in kvts]
    conv_prompt = utail[:, 6:8][None]

    n = bd * t
    st = state_conv[0]
    p0 = jnp.repeat(st[:, 0], t, axis=0)
    p1 = jnp.repeat(st[:, 1], t, axis=0)
    qkv_s, sga_s, mb_s, u_s = _inproj_sample(
        x_sample.reshape(n, D_MODEL), wp, wco, nmix, cw, cb, p0, p1, min(TM_INPROJ, n))
    tabs = [_sample_tables(biases[g], *GROUPS[g]) for g in range(N_GROUPS)]
    caches_t = [_cache_transposed(c[0]) for c in (cache_kv_g0, cache_kv_g1, cache_kv_g2)]
    os_s, ls_s = _attn_sample(qkv_s, caches_t, [tb[0] for tb in tabs],
                              jnp.stack([tb[1] for tb in tabs]))
    y_sample = _outproj_sample(x_sample.reshape(n, D_MODEL), os_s, ls_s, sga_s, mb_s, weights,
                               min(TM_OUTPROJ, n))
    kv_sample = [qkv_s[:, g * QKV_W + GROUP_W:(g + 1) * QKV_W].reshape(
        1, bd, t, 2, HEADS_PER_GROUP, HEAD_DIM) for g in range(N_GROUPS)]
    conv_sample = u_s.reshape(bd, t, CONV_W)[:, t - 2:][None]

    return (y_prompt, y_sample.reshape(bd, t, D_MODEL),
            kv_prompt[0], kv_prompt[1], kv_prompt[2], conv_prompt,
            kv_sample[0], kv_sample[1], kv_sample[2], conv_sample)
```

```python
import functools

import jax
import jax.numpy as jnp
import numpy as np
from jax import lax
from jax.experimental import pallas as pl
from jax.experimental.pallas import tpu as pltpu

D_MODEL = 1024
HEAD_DIM = 64
HEADS_PER_GROUP = 4
GROUPS = ((128, 1), (512, 4), (2048, 16))
N_GROUPS = 3
GROUP_W = HEADS_PER_GROUP * HEAD_DIM
QKV_W = 3 * GROUP_W
ATT_W = N_GROUPS * GROUP_W
CONV_W = D_MODEL // 2
D_FF = 4 * D_MODEL
N_BUCKETS = 32
MAX_DISTANCE = 2048
KEYS_PER_QUERY = 128
BLOCK = 128
LANES = 128
EPS = 1e-6
SCALE = HEAD_DIM ** -0.5
LOG2E = float(np.log2(np.e))
NEG = -1e30

OFF_CONV = 3 * ATT_W
OFF_GA = OFF_CONV + 3 * CONV_W
OFF_GB = OFF_GA + D_MODEL
PROJ_W = OFF_GB + D_MODEL

VMEM_LIMIT_V7X = 56 * 1024 * 1024
F32 = jnp.float32
BF16 = jnp.bfloat16
_NT = (((1,), (1,)), ((), ()))


def _params(n_axes):
    return pltpu.CompilerParams(
        dimension_semantics=("arbitrary",) * n_axes, vmem_limit_bytes=VMEM_LIMIT_V7X)


def _resident(shape):
    return pl.BlockSpec(shape, lambda *_: (0,) * len(shape), pipeline_mode=pl.Buffered(1))


def _rmsnorm(x, g):
    y = x * lax.rsqrt(jnp.mean(x * x, axis=-1, keepdims=True) + EPS)
    return y * g


def _head_of_lane(shape):
    return lax.broadcasted_iota(jnp.int32, shape, len(shape) - 1) // HEAD_DIM


def _per_head_rows(x):
    head = _head_of_lane(x.shape)
    return jnp.concatenate([jnp.where(head == h, x, 0.0) for h in range(HEADS_PER_GROUP)], axis=0)


def _head_diagonal(x, t):
    head = _head_of_lane((t, GROUP_W))
    out = x[0:t, :]
    for h in range(1, HEADS_PER_GROUP):
        out = jnp.where(head == h, x[h * t:(h + 1) * t, :], out)
    return out


MAX_SUBLANE_STRIDE = 4


def _rows_by_residue(slab_ref, stage_ref, dil):
    t = slab_ref.shape[0]
    if dil <= MAX_SUBLANE_STRIDE:
        return [(r, slab_ref[pl.ds(r, t // dil, stride=dil), :]) for r in range(dil)]
    s1, s2 = MAX_SUBLANE_STRIDE, dil // MAX_SUBLANE_STRIDE
    assert s1 * s2 == dil and s2 <= MAX_SUBLANE_STRIDE
    for r1 in range(s1):
        stage_ref[pl.ds(r1 * (t // s1), t // s1), :] = slab_ref[pl.ds(r1, t // s1, stride=s1), :]
    out = {}
    for r1 in range(s1):
        for k in range(s2):
            out[s1 * k + r1] = stage_ref[pl.ds(r1 * (t // s1) + k, t // dil, stride=s2), :]
    return sorted(out.items())


def _inproj_body(*refs, sample):
    if sample:
        (x_ref, win_ref, wco_ref, nm_ref, cw_ref, cb_ref, p0_ref, p1_ref,
         qkv_ref, sga_ref, mb_ref, u_ref) = refs
    else:
        (x_ref, win_ref, wco_ref, nm_ref, cw_ref, cb_ref,
         qkv0_ref, qkv1_ref, qkv2_ref, sga_ref, mb_ref, u_ref,
         carry_ref, perm_ref, stage_ref) = refs
        qkv_refs = (qkv0_ref, qkv1_ref, qkv2_ref)
    x = x_ref[...]
    tm = x.shape[0]
    xn = _rmsnorm(x, nm_ref[...]).astype(BF16)

    def proj(lo, hi):
        return jnp.dot(xn, win_ref[:, lo:hi], preferred_element_type=F32)

    pc = proj(OFF_CONV, OFF_GA)
    ga = proj(OFF_GA, OFF_GB)
    sga_ref[...] = jax.nn.sigmoid(ga).astype(sga_ref.dtype)
    sgb = jax.nn.sigmoid(proj(OFF_GB, PROJ_W))
    dilated = {which: proj(which * ATT_W + GROUP_W, (which + 1) * ATT_W) for which in (0, 1)}
    cb, cc, ch = pc[:, :CONV_W], pc[:, CONV_W:2 * CONV_W], pc[:, 2 * CONV_W:]
    u = cc * ch
    row = lax.broadcasted_iota(jnp.int32, (tm, 1), 0)
    if sample:
        p0, p1 = p0_ref[...], p1_ref[...]
        rs = jnp.bitwise_and(row, 7)
        u_ref[...] = u
    else:
        seq_start = pl.program_id(1) == 0
        p0 = jnp.where(seq_start, 0.0, carry_ref[6:7, :])
        p1 = jnp.where(seq_start, 0.0, carry_ref[7:8, :])
        rs = row
        u_ref[...] = u[tm - 8:, :]
    um1 = jnp.where(rs == 0, p1, pltpu.roll(u, 1, 0))
    um2 = jnp.where(rs == 0, p0, jnp.where(rs == 1, p1, pltpu.roll(u, 2, 0)))
    if not sample:
        carry_ref[...] = u[tm - 8:, :]
    z = cb_ref[...] + cw_ref[0:1, :] * um2 + cw_ref[1:2, :] * um1 + cw_ref[2:3, :] * u
    branch_b = jnp.dot((cb * z).astype(BF16), wco_ref[...], preferred_element_type=F32)
    mb_ref[...] = (sgb * branch_b).astype(mb_ref.dtype)
    dilated[2] = proj(2 * ATT_W + GROUP_W, 3 * ATT_W)

    slab = 0
    for g, (_, dil) in reversed(list(enumerate(GROUPS))):
        for which in range(3):
            if dil == 1:
                p = proj(which * ATT_W, which * ATT_W + GROUP_W)
            else:
                p = dilated[which][:, (g - 1) * GROUP_W:g * GROUP_W]
            if which == 0:
                p = p * (SCALE * LOG2E)
            dst = which * GROUP_W
            if sample:
                qkv_ref[:, g * QKV_W + dst:g * QKV_W + dst + GROUP_W] = p
            elif dil == 1:
                qkv_refs[g][0, :, dst:dst + GROUP_W] = p.astype(BF16)
            else:
                for half in range(GROUP_W // LANES):
                    perm_ref[slab] = p[:, half * LANES:(half + 1) * LANES]
                    lo = dst + half * LANES
                    for r, rows in _rows_by_residue(perm_ref.at[slab], stage_ref.at[slab], dil):
                        qkv_refs[g][r, :, lo:lo + LANES] = rows.astype(BF16)
                    slab += 1


def _inproj_prompt(x, w_in, w_conv_out, norm_mix, conv_w, conv_b, tm):
    b, s, _ = x.shape
    nt = s // tm
    tok = lambda w: pl.BlockSpec((None, tm, w), lambda bi, i: (bi, i, 0))
    qkv_specs = [pl.BlockSpec((None, dil, tm // dil, QKV_W), lambda bi, i: (bi, 0, i, 0))
                 for _, dil in GROUPS]
    n_slabs = sum(3 * GROUP_W // LANES for _, dil in GROUPS if dil > 1)
    out_shape = (
        [jax.ShapeDtypeStruct((b, dil, s // dil, QKV_W), BF16) for _, dil in GROUPS]
        + [jax.ShapeDtypeStruct((b, s, D_MODEL), BF16),
           jax.ShapeDtypeStruct((b, s, D_MODEL), BF16),
           jax.ShapeDtypeStruct((b, 8, CONV_W), F32)])
    return pl.pallas_call(
        functools.partial(_inproj_body, sample=False),
        grid=(b, nt),
        in_specs=[tok(D_MODEL), _resident(w_in.shape), _resident(w_conv_out.shape),
                  _resident(norm_mix.shape), _resident(conv_w.shape), _resident(conv_b.shape)],
        out_specs=qkv_specs
        + [tok(D_MODEL), tok(D_MODEL), pl.BlockSpec((None, 8, CONV_W), lambda bi, i: (bi, 0, 0))],
        out_shape=out_shape,
        scratch_shapes=[pltpu.VMEM((8, CONV_W), F32), pltpu.VMEM((n_slabs, tm, LANES), F32),
                        pltpu.VMEM((n_slabs, tm, LANES), F32)],
        compiler_params=_params(2),
        name="inproj_prompt",
    )(x, w_in, w_conv_out, norm_mix, conv_w, conv_b)


def _cache_rows_body(x_ref, wk_ref, wv_ref, nm_ref, kvt_ref):
    xn = _rmsnorm(x_ref[...], nm_ref[...]).astype(BF16)
    for which, w_ref in enumerate((wk_ref, wv_ref)):
        kv = jnp.dot(xn, w_ref[...], preferred_element_type=F32)
        kvt_ref[which] = kv.T


def _cache_rows_prompt(x, w_in, norm_mix, g, tm, name):
    b, s, _ = x.shape
    rows = GROUPS[g][0]
    tm = min(tm, rows)
    first = (s - rows) // tm
    col_blocks = ATT_W // GROUP_W
    w_spec = lambda which: pl.BlockSpec((D_MODEL, GROUP_W),
                                        lambda bi, i: (0, which * col_blocks + g))
    return pl.pallas_call(
        _cache_rows_body,
        grid=(b, rows // tm),
        in_specs=[pl.BlockSpec((None, tm, D_MODEL), lambda bi, i: (bi, first + i, 0)),
                  w_spec(1), w_spec(2), _resident(norm_mix.shape)],
        out_specs=pl.BlockSpec((None, 2, GROUP_W, tm), lambda bi, i: (bi, 0, 0, i)),
        out_shape=jax.ShapeDtypeStruct((b, 2, GROUP_W, rows), F32),
        compiler_params=_params(2),
        name=name,
    )(x, w_in, w_in, norm_mix)


def _inproj_sample(x, w_in, w_conv_out, norm_mix, conv_w, conv_b, p0, p1, tm):
    n = x.shape[0]
    tok = lambda w: pl.BlockSpec((tm, w), lambda i: (i, 0))
    out_shape = (
        jax.ShapeDtypeStruct((n, N_GROUPS * QKV_W), F32),
        jax.ShapeDtypeStruct((n, D_MODEL), BF16),
        jax.ShapeDtypeStruct((n, D_MODEL), BF16),
        jax.ShapeDtypeStruct((n, CONV_W), F32),
    )
    return pl.pallas_call(
        functools.partial(_inproj_body, sample=True),
        grid=(n // tm,),
        in_specs=[tok(D_MODEL), _resident(w_in.shape), _resident(w_conv_out.shape),
                  _resident(norm_mix.shape), _resident(conv_w.shape), _resident(conv_b.shape),
                  tok(CONV_W), tok(CONV_W)],
        out_specs=[tok(N_GROUPS * QKV_W), tok(D_MODEL), tok(D_MODEL), tok(CONV_W)],
        out_shape=out_shape,
        compiler_params=_params(1),
        name="inproj_sample",
    )(x, w_in, w_conv_out, norm_mix, conv_w, conv_b, p0, p1)


def _attn_prompt_body(q_ref, kc_ref, kp_ref, vc_ref, vp_ref, tab_ref, o_ref, l_ref):
    i = pl.program_id(2)
    first_tile = jnp.where(i == 0, 0, 1)
    head = _head_of_lane((BLOCK, GROUP_W))
    low_head = lax.broadcasted_iota(jnp.int32, (BLOCK, LANES), 1) < HEAD_DIM
    ones = jnp.ones((2 * BLOCK, LANES), BF16)
    n_cls, tq, _ = q_ref.shape
    for c, j in [(c, j) for c in range(n_cls) for j in range(tq // BLOCK)]:
        q = q_ref[c, j * BLOCK:(j + 1) * BLOCK, :]
        qm = jnp.concatenate(
            [jnp.where(head == h, q, jnp.zeros_like(q)) for h in range(HEADS_PER_GROUP)], axis=0)
        if j == 0:
            k = jnp.concatenate([kp_ref[c], kc_ref[c, 0:BLOCK, :]], axis=0)
            v = jnp.concatenate([vp_ref[c], vc_ref[c, 0:BLOCK, :]], axis=0)
            tab = tab_ref[first_tile]
        else:
            k = kc_ref[c, (j - 1) * BLOCK:(j + 1) * BLOCK, :]
            v = vc_ref[c, (j - 1) * BLOCK:(j + 1) * BLOCK, :]
            tab = tab_ref[1]
        s = lax.dot_general(qm, k, _NT, preferred_element_type=F32) + tab
        m = jnp.max(s, axis=-1, keepdims=True)
        p = jnp.exp2(s - m).astype(BF16)
        halves_o, halves_l = [], []
        for pair in range(HEADS_PER_GROUP // 2):
            w = jnp.concatenate([v[:, pair * LANES:(pair + 1) * LANES], ones], axis=1)
            r = jnp.dot(p[2 * pair * BLOCK:2 * (pair + 1) * BLOCK, :], w,
                        preferred_element_type=F32)
            m0 = jnp.broadcast_to(m[2 * pair * BLOCK:(2 * pair + 1) * BLOCK], (BLOCK, LANES))
            m1 = jnp.broadcast_to(m[(2 * pair + 1) * BLOCK:2 * (pair + 1) * BLOCK], (BLOCK, LANES))
            acc = jnp.where(low_head, r[:BLOCK, :LANES], r[BLOCK:, :LANES])
            l = jnp.where(low_head, r[:BLOCK, LANES:], r[BLOCK:, LANES:])
            halves_o.append(acc / l)
            halves_l.append(jnp.where(low_head, m0, m1) + jnp.log2(l))
        o_ref[c, j * BLOCK:(j + 1) * BLOCK, :] = (
            jnp.concatenate(halves_o, axis=1).astype(o_ref.dtype))
        l_ref[c, j * BLOCK:(j + 1) * BLOCK, :] = jnp.concatenate(halves_l, axis=1)


def _attn_prompt(qkv, tab, rows_per_step, name):
    b, dil, L, _ = qkv.shape
    tq = min(rows_per_step, L)
    n_cls = min(rows_per_step // tq, dil)
    cur_spec = lambda which: pl.BlockSpec(
        (None, n_cls, tq, GROUP_W), lambda bi, r, i: (bi, r, i, which))
    prev_spec = lambda which: pl.BlockSpec(
        (None, n_cls, BLOCK, GROUP_W),
        lambda bi, r, i: (bi, r, jnp.maximum(i * (tq // BLOCK) - 1, 0), which))
    out_spec = pl.BlockSpec((None, n_cls, tq, GROUP_W), lambda bi, r, i: (bi, r, i, 0))
    return pl.pallas_call(
        _attn_prompt_body,
        grid=(b, dil // n_cls, L // tq),
        in_specs=[cur_spec(0), cur_spec(1), prev_spec(1), cur_spec(2), prev_spec(2),
                  _resident(tab.shape)],
        out_specs=[out_spec, out_spec],
        out_shape=(jax.ShapeDtypeStruct((b, dil, L, GROUP_W), BF16),
                   jax.ShapeDtypeStruct((b, dil, L, GROUP_W), F32)),
        compiler_params=_params(3),
        name=name,
    )(qkv, qkv, qkv, qkv, qkv, tab)


DEC_T = 8
NEW_PAD = 16
SEQS_PER_SAMPLE_STEP = 2


def _attn_sample_body(qkv_ref, c0_ref, c1_ref, c2_ref, tc0_ref, tc1_ref, tc2_ref, tn_ref,
                      o0_ref, o1_ref, o2_ref, l0_ref, l1_ref, l2_ref):
    caches = (c0_ref, c1_ref, c2_ref)
    tcs = (tc0_ref, tc1_ref, tc2_ref)
    outs = ((o0_ref, l0_ref), (o1_ref, l1_ref), (o2_ref, l2_ref))
    pad = jnp.zeros((NEW_PAD - DEC_T, GROUP_W), F32)
    work = [(i, g) for i in range(c0_ref.shape[0]) for g in range(N_GROUPS)]
    scores = []
    for i, g in work:
        rows = slice(i * DEC_T, (i + 1) * DEC_T)
        base = g * QKV_W
        q = qkv_ref[rows, base:base + GROUP_W]
        kn = jnp.concatenate([qkv_ref[rows, base + GROUP_W:base + 2 * GROUP_W], pad], axis=0)
        qm = _per_head_rows(q).astype(BF16)
        s_n = lax.dot_general(qm, kn.astype(BF16), _NT, preferred_element_type=F32) + tn_ref[g]
        s_c = jnp.dot(qm, caches[g][i, 0].astype(BF16), preferred_element_type=F32) + tcs[g][...]
        scores.append((s_c, s_n))
    for (i, g), (s_c, s_n) in zip(work, scores):
        rows = slice(i * DEC_T, (i + 1) * DEC_T)
        base = g * QKV_W
        vn = jnp.concatenate([qkv_ref[rows, base + 2 * GROUP_W:base + QKV_W], pad], axis=0)
        m = jnp.maximum(jnp.max(s_c, axis=-1, keepdims=True), jnp.max(s_n, axis=-1, keepdims=True))
        p_c = jnp.exp2(s_c - m)
        p_n = jnp.exp2(s_n - m)
        l = jnp.sum(p_c, axis=-1, keepdims=True) + jnp.sum(p_n, axis=-1, keepdims=True)
        acc = jnp.dot(p_n.astype(BF16), vn.astype(BF16), preferred_element_type=F32)
        acc = acc + lax.dot_general(p_c.astype(BF16), caches[g][i, 1].astype(BF16), _NT,
                                    preferred_element_type=F32)
        res = acc / l
        lse = jnp.broadcast_to(m + jnp.log2(l), res.shape)
        o_ref, l_ref = outs[g]
        o_ref[rows, :] = _head_diagonal(res, DEC_T)
        l_ref[rows, :] = _head_diagonal(lse, DEC_T)


def _attn_sample(qkv, caches_t, tcs, tn):
    n = qkv.shape[0]
    bd = n // DEC_T
    nb = SEQS_PER_SAMPLE_STEP
    cache_specs = [pl.BlockSpec((nb,) + c.shape[1:], lambda bi: (bi, 0, 0, 0)) for c in caches_t]
    out_spec = pl.BlockSpec((nb * DEC_T, GROUP_W), lambda bi: (bi, 0))
    out = jax.ShapeDtypeStruct((n, GROUP_W), F32)
    res = pl.pallas_call(
        _attn_sample_body,
        grid=(bd // nb,),
        in_specs=[pl.BlockSpec((nb * DEC_T, N_GROUPS * QKV_W), lambda bi: (bi, 0))] + cache_specs
        + [_resident(tc.shape) for tc in tcs] + [_resident(tn.shape)],
        out_specs=[out_spec] * 6,
        out_shape=(out,) * 6,
        compiler_params=_params(1),
        name="attn_sample",
    )(qkv, *caches_t, *tcs, tn)
    return res[:3], res[3:]


FF_CHUNK = 1024


def _natural_rows(ref, scratch_ref, slab):
    dil, rows, _ = ref.shape
    if dil == 1:
        return ref[0].astype(F32)
    halves = []
    for half in range(GROUP_W // LANES):
        for r in range(dil):
            scratch_ref[slab + half, pl.ds(r, rows, stride=dil), :] = (
                ref[r, :, half * LANES:(half + 1) * LANES].astype(F32))
        halves.append(scratch_ref[slab + half])
    return jnp.concatenate(halves, axis=1)


def _outproj_body(x_ref, o0_ref, o1_ref, o2_ref, l0_ref, l1_ref, l2_ref, sga_ref, mb_ref,
                  wao_ref, wo_ref, w1_ref, w2_ref, nmlp_ref, nfin_ref, y_ref, *scratch):
    if scratch:
        slabs = GROUP_W // LANES
        os_ = [_natural_rows(ref, scratch[0], slabs * k) for k, ref in
               enumerate((o0_ref, o1_ref, o2_ref))]
        ls_ = [_natural_rows(ref, scratch[0], slabs * (3 + k)) for k, ref in
               enumerate((l0_ref, l1_ref, l2_ref))]
    else:
        os_ = [ref[...] for ref in (o0_ref, o1_ref, o2_ref)]
        ls_ = [ref[...] for ref in (l0_ref, l1_ref, l2_ref)]
    (o0, o1, o2), (l0, l1, l2) = os_, ls_
    m = jnp.maximum(l0, jnp.maximum(l1, l2))
    e0, e1, e2 = jnp.exp2(l0 - m), jnp.exp2(l1 - m), jnp.exp2(l2 - m)
    att = (e0 * o0 + e1 * o1 + e2 * o2) / (e0 + e1 + e2)
    branch_a = jnp.dot(att.astype(BF16), wao_ref[...], preferred_element_type=F32)
    mix = sga_ref[...].astype(F32) * branch_a + mb_ref[...].astype(F32)
    x1 = x_ref[...] + jnp.dot(mix.astype(BF16), wo_ref[...], preferred_element_type=F32)
    h = _rmsnorm(x1, nmlp_ref[...]).astype(BF16)
    acc = x1
    for c in range(D_FF // FF_CHUNK):
        a = jnp.dot(h, w1_ref[:, c * FF_CHUNK:(c + 1) * FF_CHUNK], preferred_element_type=F32)
        a = jnp.square(jnp.maximum(a, 0.0)).astype(BF16)
        acc = acc + jnp.dot(a, w2_ref[c * FF_CHUNK:(c + 1) * FF_CHUNK, :],
                            preferred_element_type=F32)
    y_ref[...] = _rmsnorm(acc, nfin_ref[...])


def _outproj_prompt(x, os_, ls_, sga, mb, weights, tm):
    b, s, _ = x.shape
    tok = pl.BlockSpec((None, tm, D_MODEL), lambda bi, i: (bi, i, 0))
    grouped = [pl.BlockSpec((None, dil, tm // dil, GROUP_W), lambda bi, i: (bi, 0, i, 0))
               for _, dil in GROUPS]
    return pl.pallas_call(
        _outproj_body,
        grid=(b, s // tm),
        in_specs=[tok] + grouped + grouped + [tok, tok] + [_resident(w.shape) for w in weights],
        out_specs=tok,
        out_shape=jax.ShapeDtypeStruct((b, s, D_MODEL), F32),
        scratch_shapes=[pltpu.VMEM((2 * N_GROUPS * GROUP_W // LANES, tm, LANES), F32)],
        compiler_params=_params(2),
        name="outproj_ffn_prompt",
    )(x, *os_, *ls_, sga, mb, *weights)


def _outproj_sample(x, os_, ls_, sga, mb, weights, tm):
    n = x.shape[0]
    tok = lambda w: pl.BlockSpec((tm, w), lambda i: (i, 0))
    return pl.pallas_call(
        _outproj_body,
        grid=(n // tm,),
        in_specs=[tok(D_MODEL)] + [tok(GROUP_W)] * 6 + [tok(D_MODEL), tok(D_MODEL)]
        + [_resident(w.shape) for w in weights],
        out_specs=tok(D_MODEL),
        out_shape=jax.ShapeDtypeStruct((n, D_MODEL), F32),
        compiler_params=_params(1),
        name="outproj_ffn_sample",
    )(x, *os_, *ls_, sga, mb, *weights)


def _t5_bucket(dist):
    n = np.asarray(dist)
    max_exact = N_BUCKETS // 2
    large = max_exact + (np.log(np.maximum(n, 1) / max_exact) / np.log(MAX_DISTANCE / max_exact)
                         * (N_BUCKETS - max_exact)).astype(np.int32)
    large = np.minimum(large, N_BUCKETS - 1)
    return np.where(n < max_exact, n, large).astype(np.int32)


def _group_bias(rel_bias, g):
    dil = GROUPS[g][1]
    buckets = _t5_bucket(np.arange(KEYS_PER_QUERY + 1) * dil)
    bias = rel_bias[buckets][:, g * HEADS_PER_GROUP:(g + 1) * HEADS_PER_GROUP].astype(F32)
    return bias * LOG2E


def _toeplitz(vec, n_rows, n_cols, offset):
    h, n = vec.shape
    start = n - 1 - offset
    assert start - (n_rows - 1) >= 0 and start + n_cols <= n
    w = max(n, start + n_cols + 1)
    rev = jnp.pad(vec[:, ::-1], ((0, 0), (0, w - n)))
    skew = jnp.tile(rev, (1, n_rows))[:, :n_rows * (w - 1)].reshape(h, n_rows, w - 1)
    return skew[:, :, start:start + n_cols].reshape(h * n_rows, n_cols)


def _pad_neg(vec, before, after):
    h = vec.shape[0]
    return jnp.concatenate([jnp.full((h, before), NEG, F32), vec, jnp.full((h, after), NEG, F32)],
                           axis=1)


def _prompt_table(bias):
    by_stride = _pad_neg(bias.T, BLOCK - 1, BLOCK - 1)
    tab = _toeplitz(by_stride, BLOCK, 2 * BLOCK, 2 * BLOCK - 1)
    cur = (np.arange(2 * BLOCK) >= BLOCK)[None, :]
    return jnp.stack([jnp.where(cur, tab, NEG), tab])


def _sample_tables(bias, win, dil):
    h = bias.shape[1]
    spread = jnp.concatenate([bias.T[:, :, None], jnp.full((h, KEYS_PER_QUERY + 1, dil - 1), NEG)],
                             axis=2).reshape(h, -1)[:, :KEYS_PER_QUERY * dil + 1]
    tc = _toeplitz(_pad_neg(spread, 0, DEC_T - 1), DEC_T, win, win)
    tn = _toeplitz(_pad_neg(spread, NEW_PAD - 1, 0), DEC_T, NEW_PAD, NEW_PAD - 1)
    return tc, tn


def _cache_transposed(cache):
    bd, wb = cache.shape[:2]
    return jnp.transpose(cache, (0, 2, 3, 4, 1)).reshape(bd, 2, GROUP_W, wb)


def _cache_rows(kvt):
    b, _, _, win = kvt.shape
    kvt = kvt.reshape(b, 2, HEADS_PER_GROUP, HEAD_DIM, win)
    return jnp.transpose(kvt, (0, 4, 1, 2, 3))[None]


TM_INPROJ = 512
TM_OUTPROJ = 512
TM_CACHE_ROWS = 1024
TQ_ATTN = 4096


def kernel(x_prompt, x_sample, cache_kv_g0, cache_kv_g1, cache_kv_g2, state_conv, w_in, w_att_out,
           w_conv_out, w_o, conv_w, conv_b, rel_bias, norm_mix, norm_mlp, w_ff1, w_ff2, norm_final):
    assert w_in.shape[0] == 1, "one layer"
    b, s, _ = x_prompt.shape
    bd, t, _ = x_sample.shape
    assert t == DEC_T
    wp = w_in[0].astype(BF16)
    wco = w_conv_out[0].astype(BF16)
    wao, wo = w_att_out[0].astype(BF16), w_o[0].astype(BF16)
    w1, w2 = w_ff1[0].astype(BF16), w_ff2[0].astype(BF16)
    nmix, nmlp, nfin = norm_mix[0][None], norm_mlp[0][None], norm_final[None]
    cw, cb = conv_w[0], conv_b[0][None]
    biases = [_group_bias(rel_bias, g) for g in range(N_GROUPS)]

    weights = (wao, wo, w1, w2, nmlp, nfin)

    qkv0, qkv1, qkv2, sga, mb, utail = _inproj_prompt(x_prompt, wp, wco, nmix, cw, cb, TM_INPROJ)
    kvts = [_cache_rows_prompt(x_prompt, wp, nmix, g, TM_CACHE_ROWS, f"cache_rows_g{g}")
            for g in range(N_GROUPS)]
    os_, ls_ = [], []
    for g, qkv in enumerate((qkv0, qkv1, qkv2)):
        o, lse = _attn_prompt(qkv, _prompt_table(biases[g]), TQ_ATTN, f"attn_prompt_g{g}")
        os_.append(o)
        ls_.append(lse)
    y_prompt = _outproj_prompt(x_prompt, os_, ls_, sga, mb, weights, TM_OUTPROJ)
    kv_prompt = [_cache_rows(kvt) for kvt in kvts]
    conv_prompt = utail[:, 6:8][None]

    n = bd * t
    st = state_conv[0]
    p0 = jnp.repeat(st[:, 0], t, axis=0)
    p1 = jnp.repeat(st[:, 1], t, axis=0)
    qkv_s, sga_s, mb_s, u_s = _inproj_sample(
        x_sample.reshape(n, D_MODEL), wp, wco, nmix, cw, cb, p0, p1, min(TM_INPROJ, n))
    tabs = [_sample_tables(biases[g], *GROUPS[g]) for g in range(N_GROUPS)]
    caches_t = [_cache_transposed(c[0]) for c in (cache_kv_g0, cache_kv_g1, cache_kv_g2)]
    os_s, ls_s = _attn_sample(qkv_s, caches_t, [tb[0] for tb in tabs],
                              jnp.stack([tb[1] for tb in tabs]))
    y_sample = _outproj_sample(x_sample.reshape(n, D_MODEL), os_s, ls_s, sga_s, mb_s, weights,
                               min(TM_OUTPROJ, n))
    kv_sample = [qkv_s[:, g * QKV_W + GROUP_W:(g + 1) * QKV_W].reshape(
        1, bd, t, 2, HEADS_PER_GROUP, HEAD_DIM) for g in range(N_GROUPS)]
    conv_sample = u_s.reshape(bd, t, CONV_W)[:, t - 2:][None]

    return (y_prompt, y_sample.reshape(bd, t, D_MODEL),
            kv_prompt[0], kv_prompt[1], kv_prompt[2], conv_prompt,
            kv_sample[0], kv_sample[1], kv_sample[2], conv_sample)
```

```python
import functools

import jax
import jax.numpy as jnp
import numpy as np
from jax import lax
from jax.experimental import pallas as pl
from jax.experimental.pallas import tpu as pltpu

D_MODEL = 1024
HEAD_DIM = 64
HEADS_PER_GROUP = 4
GROUPS = ((128, 1), (512, 4), (2048, 16))
N_GROUPS = 3
GROUP_W = HEADS_PER_GROUP * HEAD_DIM
QKV_W = 3 * GROUP_W
ATT_W = N_GROUPS * GROUP_W
CONV_W = D_MODEL // 2
D_FF = 4 * D_MODEL
N_BUCKETS = 32
MAX_DISTANCE = 2048
KEYS_PER_QUERY = 128
BLOCK = 128
LANES = 128
EPS = 1e-6
SCALE = HEAD_DIM ** -0.5
LOG2E = float(np.log2(np.e))
NEG = -1e30

OFF_CONV = 3 * ATT_W
OFF_GA = OFF_CONV + 3 * CONV_W
OFF_GB = OFF_GA + D_MODEL
PROJ_W = OFF_GB + D_MODEL

VMEM_LIMIT_V7X = 56 * 1024 * 1024
F32 = jnp.float32
BF16 = jnp.bfloat16
_NT = (((1,), (1,)), ((), ()))


def _params(n_axes):
    return pltpu.CompilerParams(
        dimension_semantics=("arbitrary",) * n_axes, vmem_limit_bytes=VMEM_LIMIT_V7X)


def _resident(shape):
    return pl.BlockSpec(shape, lambda *_: (0,) * len(shape), pipeline_mode=pl.Buffered(1))


def _rmsnorm(x, g):
    y = x * lax.rsqrt(jnp.mean(x * x, axis=-1, keepdims=True) + EPS)
    return y * g


def _head_of_lane(shape):
    return lax.broadcasted_iota(jnp.int32, shape, len(shape) - 1) // HEAD_DIM


def _per_head_rows(x):
    head = _head_of_lane(x.shape)
    return jnp.concatenate([jnp.where(head == h, x, 0.0) for h in range(HEADS_PER_GROUP)], axis=0)


def _head_diagonal(x, t):
    head = _head_of_lane((t, GROUP_W))
    out = x[0:t, :]
    for h in range(1, HEADS_PER_GROUP):
        out = jnp.where(head == h, x[h * t:(h + 1) * t, :], out)
    return out


MAX_SUBLANE_STRIDE = 4


def _rows_by_residue(slab_ref, stage_ref, dil):
    t = slab_ref.shape[0]
    if dil <= MAX_SUBLANE_STRIDE:
        return [(r, slab_ref[pl.ds(r, t // dil, stride=dil), :]) for r in range(dil)]
    s1, s2 = MAX_SUBLANE_STRIDE, dil // MAX_SUBLANE_STRIDE
    assert s1 * s2 == dil and s2 <= MAX_SUBLANE_STRIDE
    for r1 in range(s1):
        stage_ref[pl.ds(r1 * (t // s1), t // s1), :] = slab_ref[pl.ds(r1, t // s1, stride=s1), :]
    out = {}
    for r1 in range(s1):
        for k in range(s2):
            out[s1 * k + r1] = stage_ref[pl.ds(r1 * (t // s1) + k, t // dil, stride=s2), :]
    return sorted(out.items())


def _inproj_body(*refs, sample):
    if sample:
        (x_ref, win_ref, wco_ref, nm_ref, cw_ref, cb_ref, p0_ref, p1_ref,
         qkv_ref, sga_ref, mb_ref, u_ref) = refs
    else:
        (x_ref, win_ref, wco_ref, nm_ref, cw_ref, cb_ref,
         qkv0_ref, qkv1_ref, qkv2_ref, kvt0_ref, kvt1_ref, kvt2_ref, sga_ref, mb_ref, u_ref,
         carry_ref, perm_ref, stage_ref) = refs
        qkv_refs = (qkv0_ref, qkv1_ref, qkv2_ref)
        kvt_refs = (kvt0_ref, kvt1_ref, kvt2_ref)
    x = x_ref[...]
    tm = x.shape[0]
    xn = _rmsnorm(x, nm_ref[...]).astype(BF16)

    def proj(lo, hi):
        return jnp.dot(xn, win_ref[:, lo:hi], preferred_element_type=F32)

    pc = proj(OFF_CONV, OFF_GA)
    ga = proj(OFF_GA, OFF_GB)
    sga_ref[...] = jax.nn.sigmoid(ga).astype(sga_ref.dtype)
    sgb = jax.nn.sigmoid(proj(OFF_GB, PROJ_W))
    dilated = {which: proj(which * ATT_W + GROUP_W, (which + 1) * ATT_W) for which in (0, 1)}
    cb, cc, ch = pc[:, :CONV_W], pc[:, CONV_W:2 * CONV_W], pc[:, 2 * CONV_W:]
    u = cc * ch
    row = lax.broadcasted_iota(jnp.int32, (tm, 1), 0)
    if sample:
        p0, p1 = p0_ref[...], p1_ref[...]
        rs = jnp.bitwise_and(row, 7)
        u_ref[...] = u
    else:
        seq_start = pl.program_id(1) == 0
        p0 = jnp.where(seq_start, 0.0, carry_ref[6:7, :])
        p1 = jnp.where(seq_start, 0.0, carry_ref[7:8, :])
        rs = row
        u_ref[...] = u[tm - 8:, :]
    um1 = jnp.where(rs == 0, p1, pltpu.roll(u, 1, 0))
    um2 = jnp.where(rs == 0, p0, jnp.where(rs == 1, p1, pltpu.roll(u, 2, 0)))
    if not sample:
        carry_ref[...] = u[tm - 8:, :]
    z = cb_ref[...] + cw_ref[0:1, :] * um2 + cw_ref[1:2, :] * um1 + cw_ref[2:3, :] * u
    branch_b = jnp.dot((cb * z).astype(BF16), wco_ref[...], preferred_element_type=F32)
    mb_ref[...] = (sgb * branch_b).astype(mb_ref.dtype)
    dilated[2] = proj(2 * ATT_W + GROUP_W, 3 * ATT_W)

    slab = 0
    for g, (_, dil) in reversed(list(enumerate(GROUPS))):
        for which in range(3):
            if dil == 1:
                p = proj(which * ATT_W, which * ATT_W + GROUP_W)
            else:
                p = dilated[which][:, (g - 1) * GROUP_W:g * GROUP_W]
            if which == 0:
                p = p * (SCALE * LOG2E)
            elif not sample:
                keep = kvt_refs[g].shape[-1]
                kvt_refs[g][which - 1] = p[tm - keep:, :].T
            dst = which * GROUP_W
            if sample:
                qkv_ref[:, g * QKV_W + dst:g * QKV_W + dst + GROUP_W] = p
            elif dil == 1:
                qkv_refs[g][0, :, dst:dst + GROUP_W] = p.astype(BF16)
            else:
                for half in range(GROUP_W // LANES):
                    perm_ref[slab] = p[:, half * LANES:(half + 1) * LANES]
                    lo = dst + half * LANES
                    for r, rows in _rows_by_residue(perm_ref.at[slab], stage_ref.at[slab], dil):
                        qkv_refs[g][r, :, lo:lo + LANES] = rows.astype(BF16)
                    slab += 1


def _inproj_prompt(x, w_in, w_conv_out, norm_mix, conv_w, conv_b, tm):
    b, s, _ = x.shape
    nt = s // tm
    tok = lambda w: pl.BlockSpec((None, tm, w), lambda bi, i: (bi, i, 0))
    qkv_specs = [pl.BlockSpec((None, dil, tm // dil, QKV_W), lambda bi, i: (bi, 0, i, 0))
                 for _, dil in GROUPS]
    n_slabs = sum(3 * GROUP_W // LANES for _, dil in GROUPS if dil > 1)

    def kvt_spec(win):
        keep = min(win, tm)
        first = (s - win) // tm
        return pl.BlockSpec((None, 2, GROUP_W, keep),
                            lambda bi, i: (bi, 0, 0, jnp.maximum(i - first, 0)))

    out_shape = (
        [jax.ShapeDtypeStruct((b, dil, s // dil, QKV_W), BF16) for _, dil in GROUPS]
        + [jax.ShapeDtypeStruct((b, 2, GROUP_W, win), F32) for win, _ in GROUPS]
        + [jax.ShapeDtypeStruct((b, s, D_MODEL), BF16),
           jax.ShapeDtypeStruct((b, s, D_MODEL), BF16),
           jax.ShapeDtypeStruct((b, 8, CONV_W), F32)])
    return pl.pallas_call(
        functools.partial(_inproj_body, sample=False),
        grid=(b, nt),
        in_specs=[tok(D_MODEL), _resident(w_in.shape), _resident(w_conv_out.shape),
                  _resident(norm_mix.shape), _resident(conv_w.shape), _resident(conv_b.shape)],
        out_specs=qkv_specs + [kvt_spec(win) for win, _ in GROUPS]
        + [tok(D_MODEL), tok(D_MODEL), pl.BlockSpec((None, 8, CONV_W), lambda bi, i: (bi, 0, 0))],
        out_shape=out_shape,
        scratch_shapes=[pltpu.VMEM((8, CONV_W), F32), pltpu.VMEM((n_slabs, tm, LANES), F32),
                        pltpu.VMEM((n_slabs, tm, LANES), F32)],
        compiler_params=_params(2),
        name="inproj_prompt",
    )(x, w_in, w_conv_out, norm_mix, conv_w, conv_b)


def _inproj_sample(x, w_in, w_conv_out, norm_mix, conv_w, conv_b, p0, p1, tm):
    n = x.shape[0]
    tok = lambda w: pl.BlockSpec((tm, w), lambda i: (i, 0))
    out_shape = (
        jax.ShapeDtypeStruct((n, N_GROUPS * QKV_W), F32),
        jax.ShapeDtypeStruct((n, D_MODEL), BF16),
        jax.ShapeDtypeStruct((n, D_MODEL), BF16),
        jax.ShapeDtypeStruct((n, CONV_W), F32),
    )
    return pl.pallas_call(
        functools.partial(_inproj_body, sample=True),
        grid=(n // tm,),
        in_specs=[tok(D_MODEL), _resident(w_in.shape), _resident(w_conv_out.shape),
                  _resident(norm_mix.shape), _resident(conv_w.shape), _resident(conv_b.shape),
                  tok(CONV_W), tok(CONV_W)],
        out_specs=[tok(N_GROUPS * QKV_W), tok(D_MODEL), tok(D_MODEL), tok(CONV_W)],
        out_shape=out_shape,
        compiler_params=_params(1),
        name="inproj_sample",
    )(x, w_in, w_conv_out, norm_mix, conv_w, conv_b, p0, p1)


def _attn_prompt_body(q_ref, kc_ref, kp_ref, vc_ref, vp_ref, tab_ref, o_ref, l_ref):
    i = pl.program_id(2)
    first_tile = jnp.where(i == 0, 0, 1)
    head = _head_of_lane((BLOCK, GROUP_W))
    low_head = lax.broadcasted_iota(jnp.int32, (BLOCK, LANES), 1) < HEAD_DIM
    ones = jnp.ones((2 * BLOCK, LANES), BF16)
    n_cls, tq, _ = q_ref.shape
    for c, j in [(c, j) for c in range(n_cls) for j in range(tq // BLOCK)]:
        q = q_ref[c, j * BLOCK:(j + 1) * BLOCK, :]
        qm = jnp.concatenate(
            [jnp.where(head == h, q, jnp.zeros_like(q)) for h in range(HEADS_PER_GROUP)], axis=0)
        if j == 0:
            k = jnp.concatenate([kp_ref[c], kc_ref[c, 0:BLOCK, :]], axis=0)
            v = jnp.concatenate([vp_ref[c], vc_ref[c, 0:BLOCK, :]], axis=0)
            tab = tab_ref[first_tile]
        else:
            k = kc_ref[c, (j - 1) * BLOCK:(j + 1) * BLOCK, :]
            v = vc_ref[c, (j - 1) * BLOCK:(j + 1) * BLOCK, :]
            tab = tab_ref[1]
        s = lax.dot_general(qm, k, _NT, preferred_element_type=F32) + tab
        m = jnp.max(s, axis=-1, keepdims=True)
        p = jnp.exp2(s - m).astype(BF16)
        halves_o, halves_l = [], []
        for pair in range(HEADS_PER_GROUP // 2):
            w = jnp.concatenate([v[:, pair * LANES:(pair + 1) * LANES], ones], axis=1)
            r = jnp.dot(p[2 * pair * BLOCK:2 * (pair + 1) * BLOCK, :], w,
                        preferred_element_type=F32)
            m0 = jnp.broadcast_to(m[2 * pair * BLOCK:(2 * pair + 1) * BLOCK], (BLOCK, LANES))
            m1 = jnp.broadcast_to(m[(2 * pair + 1) * BLOCK:2 * (pair + 1) * BLOCK], (BLOCK, LANES))
            acc = jnp.where(low_head, r[:BLOCK, :LANES], r[BLOCK:, :LANES])
            l = jnp.where(low_head, r[:BLOCK, LANES:], r[BLOCK:, LANES:])
            halves_o.append(acc / l)
            halves_l.append(jnp.where(low_head, m0, m1) + jnp.log2(l))
        o_ref[c, j * BLOCK:(j + 1) * BLOCK, :] = (
            jnp.concatenate(halves_o, axis=1).astype(o_ref.dtype))
        l_ref[c, j * BLOCK:(j + 1) * BLOCK, :] = jnp.concatenate(halves_l, axis=1)


def _attn_prompt(qkv, tab, rows_per_step, name):
    b, dil, L, _ = qkv.shape
    tq = min(rows_per_step, L)
    n_cls = min(rows_per_step // tq, dil)
    cur_spec = lambda which: pl.BlockSpec(
        (None, n_cls, tq, GROUP_W), lambda bi, r, i: (bi, r, i, which))
    prev_spec = lambda which: pl.BlockSpec(
        (None, n_cls, BLOCK, GROUP_W),
        lambda bi, r, i: (bi, r, jnp.maximum(i * (tq // BLOCK) - 1, 0), which))
    out_spec = pl.BlockSpec((None, n_cls, tq, GROUP_W), lambda bi, r, i: (bi, r, i, 0))
    return pl.pallas_call(
        _attn_prompt_body,
        grid=(b, dil // n_cls, L // tq),
        in_specs=[cur_spec(0), cur_spec(1), prev_spec(1), cur_spec(2), prev_spec(2),
                  _resident(tab.shape)],
        out_specs=[out_spec, out_spec],
        out_shape=(jax.ShapeDtypeStruct((b, dil, L, GROUP_W), BF16),
                   jax.ShapeDtypeStruct((b, dil, L, GROUP_W), F32)),
        compiler_params=_params(3),
        name=name,
    )(qkv, qkv, qkv, qkv, qkv, tab)


DEC_T = 8
NEW_PAD = 16
SEQS_PER_SAMPLE_STEP = 2


def _attn_sample_body(qkv_ref, c0_ref, c1_ref, c2_ref, tc0_ref, tc1_ref, tc2_ref, tn_ref,
                      o0_ref, o1_ref, o2_ref, l0_ref, l1_ref, l2_ref):
    caches = (c0_ref, c1_ref, c2_ref)
    tcs = (tc0_ref, tc1_ref, tc2_ref)
    outs = ((o0_ref, l0_ref), (o1_ref, l1_ref), (o2_ref, l2_ref))
    pad = jnp.zeros((NEW_PAD - DEC_T, GROUP_W), F32)
    work = [(i, g) for i in range(c0_ref.shape[0]) for g in range(N_GROUPS)]
    scores = []
    for i, g in work:
        rows = slice(i * DEC_T, (i + 1) * DEC_T)
        base = g * QKV_W
        q = qkv_ref[rows, base:base + GROUP_W]
        kn = jnp.concatenate([qkv_ref[rows, base + GROUP_W:base + 2 * GROUP_W], pad], axis=0)
        qm = _per_head_rows(q).astype(BF16)
        s_n = lax.dot_general(qm, kn.astype(BF16), _NT, preferred_element_type=F32) + tn_ref[g]
        s_c = jnp.dot(qm, caches[g][i, 0].astype(BF16), preferred_element_type=F32) + tcs[g][...]
        scores.append((s_c, s_n))
    for (i, g), (s_c, s_n) in zip(work, scores):
        rows = slice(i * DEC_T, (i + 1) * DEC_T)
        base = g * QKV_W
        vn = jnp.concatenate([qkv_ref[rows, base + 2 * GROUP_W:base + QKV_W], pad], axis=0)
        m = jnp.maximum(jnp.max(s_c, axis=-1, keepdims=True), jnp.max(s_n, axis=-1, keepdims=True))
        p_c = jnp.exp2(s_c - m)
        p_n = jnp.exp2(s_n - m)
        l = jnp.sum(p_c, axis=-1, keepdims=True) + jnp.sum(p_n, axis=-1, keepdims=True)
        acc = jnp.dot(p_n.astype(BF16), vn.astype(BF16), preferred_element_type=F32)
        acc = acc + lax.dot_general(p_c.astype(BF16), caches[g][i, 1].astype(BF16), _NT,
                                    preferred_element_type=F32)
        res = acc / l
        lse = jnp.broadcast_to(m + jnp.log2(l), res.shape)
        o_ref, l_ref = outs[g]
        o_ref[rows, :] = _head_diagonal(res, DEC_T)
        l_ref[rows, :] = _head_diagonal(lse, DEC_T)


def _attn_sample(qkv, caches_t, tcs, tn):
    n = qkv.shape[0]
    bd = n // DEC_T
    nb = SEQS_PER_SAMPLE_STEP
    cache_specs = [pl.BlockSpec((nb,) + c.shape[1:], lambda bi: (bi, 0, 0, 0)) for c in caches_t]
    out_spec = pl.BlockSpec((nb * DEC_T, GROUP_W), lambda bi: (bi, 0))
    out = jax.ShapeDtypeStruct((n, GROUP_W), F32)
    res = pl.pallas_call(
        _attn_sample_body,
        grid=(bd // nb,),
        in_specs=[pl.BlockSpec((nb * DEC_T, N_GROUPS * QKV_W), lambda bi: (bi, 0))] + cache_specs
        + [_resident(tc.shape) for tc in tcs] + [_resident(tn.shape)],
        out_specs=[out_spec] * 6,
        out_shape=(out,) * 6,
        compiler_params=_params(1),
        name="attn_sample",
    )(qkv, *caches_t, *tcs, tn)
    return res[:3], res[3:]


FF_CHUNK = 1024


def _natural_rows(ref, scratch_ref, slab):
    dil, rows, _ = ref.shape
    if dil == 1:
        return ref[0].astype(F32)
    halves = []
    for half in range(GROUP_W // LANES):
        for r in range(dil):
            scratch_ref[slab + half, pl.ds(r, rows, stride=dil), :] = (
                ref[r, :, half * LANES:(half + 1) * LANES].astype(F32))
        halves.append(scratch_ref[slab + half])
    return jnp.concatenate(halves, axis=1)


def _outproj_body(x_ref, o0_ref, o1_ref, o2_ref, l0_ref, l1_ref, l2_ref, sga_ref, mb_ref,
                  wao_ref, wo_ref, w1_ref, w2_ref, nmlp_ref, nfin_ref, y_ref, *scratch):
    if scratch:
        slabs = GROUP_W // LANES
        os_ = [_natural_rows(ref, scratch[0], slabs * k) for k, ref in
               enumerate((o0_ref, o1_ref, o2_ref))]
        ls_ = [_natural_rows(ref, scratch[0], slabs * (3 + k)) for k, ref in
               enumerate((l0_ref, l1_ref, l2_ref))]
    else:
        os_ = [ref[...] for ref in (o0_ref, o1_ref, o2_ref)]
        ls_ = [ref[...] for ref in (l0_ref, l1_ref, l2_ref)]
    (o0, o1, o2), (l0, l1, l2) = os_, ls_
    m = jnp.maximum(l0, jnp.maximum(l1, l2))
    e0, e1, e2 = jnp.exp2(l0 - m), jnp.exp2(l1 - m), jnp.exp2(l2 - m)
    att = (e0 * o0 + e1 * o1 + e2 * o2) / (e0 + e1 + e2)
    branch_a = jnp.dot(att.astype(BF16), wao_ref[...], preferred_element_type=F32)
    mix = sga_ref[...].astype(F32) * branch_a + mb_ref[...].astype(F32)
    x1 = x_ref[...] + jnp.dot(mix.astype(BF16), wo_ref[...], preferred_element_type=F32)
    h = _rmsnorm(x1, nmlp_ref[...]).astype(BF16)
    acc = x1
    for c in range(D_FF // FF_CHUNK):
        a = jnp.dot(h, w1_ref[:, c * FF_CHUNK:(c + 1) * FF_CHUNK], preferred_element_type=F32)
        a = jnp.square(jnp.maximum(a, 0.0)).astype(BF16)
        acc = acc + jnp.dot(a, w2_ref[c * FF_CHUNK:(c + 1) * FF_CHUNK, :],
                            preferred_element_type=F32)
    y_ref[...] = _rmsnorm(acc, nfin_ref[...])


def _outproj_prompt(x, os_, ls_, sga, mb, weights, tm):
    b, s, _ = x.shape
    tok = pl.BlockSpec((None, tm, D_MODEL), lambda bi, i: (bi, i, 0))
    grouped = [pl.BlockSpec((None, dil, tm // dil, GROUP_W), lambda bi, i: (bi, 0, i, 0))
               for _, dil in GROUPS]
    return pl.pallas_call(
        _outproj_body,
        grid=(b, s // tm),
        in_specs=[tok] + grouped + grouped + [tok, tok] + [_resident(w.shape) for w in weights],
        out_specs=tok,
        out_shape=jax.ShapeDtypeStruct((b, s, D_MODEL), F32),
        scratch_shapes=[pltpu.VMEM((2 * N_GROUPS * GROUP_W // LANES, tm, LANES), F32)],
        compiler_params=_params(2),
        name="outproj_ffn_prompt",
    )(x, *os_, *ls_, sga, mb, *weights)


def _outproj_sample(x, os_, ls_, sga, mb, weights, tm):
    n = x.shape[0]
    tok = lambda w: pl.BlockSpec((tm, w), lambda i: (i, 0))
    return pl.pallas_call(
        _outproj_body,
        grid=(n // tm,),
        in_specs=[tok(D_MODEL)] + [tok(GROUP_W)] * 6 + [tok(D_MODEL), tok(D_MODEL)]
        + [_resident(w.shape) for w in weights],
        out_specs=tok(D_MODEL),
        out_shape=jax.ShapeDtypeStruct((n, D_MODEL), F32),
        compiler_params=_params(1),
        name="outproj_ffn_sample",
    )(x, *os_, *ls_, sga, mb, *weights)


def _t5_bucket(dist):
    n = np.asarray(dist)
    max_exact = N_BUCKETS // 2
    large = max_exact + (np.log(np.maximum(n, 1) / max_exact) / np.log(MAX_DISTANCE / max_exact)
                         * (N_BUCKETS - max_exact)).astype(np.int32)
    large = np.minimum(large, N_BUCKETS - 1)
    return np.where(n < max_exact, n, large).astype(np.int32)


def _group_bias(rel_bias, g):
    dil = GROUPS[g][1]
    buckets = _t5_bucket(np.arange(KEYS_PER_QUERY + 1) * dil)
    bias = rel_bias[buckets][:, g * HEADS_PER_GROUP:(g + 1) * HEADS_PER_GROUP].astype(F32)
    return bias * LOG2E


def _toeplitz(vec, n_rows, n_cols, offset):
    h, n = vec.shape
    start = n - 1 - offset
    assert start - (n_rows - 1) >= 0 and start + n_cols <= n
    w = max(n, start + n_cols + 1)
    rev = jnp.pad(vec[:, ::-1], ((0, 0), (0, w - n)))
    skew = jnp.tile(rev, (1, n_rows))[:, :n_rows * (w - 1)].reshape(h, n_rows, w - 1)
    return skew[:, :, start:start + n_cols].reshape(h * n_rows, n_cols)


def _pad_neg(vec, before, after):
    h = vec.shape[0]
    return jnp.concatenate([jnp.full((h, before), NEG, F32), vec, jnp.full((h, after), NEG, F32)],
                           axis=1)


def _prompt_table(bias):
    by_stride = _pad_neg(bias.T, BLOCK - 1, BLOCK - 1)
    tab = _toeplitz(by_stride, BLOCK, 2 * BLOCK, 2 * BLOCK - 1)
    cur = (np.arange(2 * BLOCK) >= BLOCK)[None, :]
    return jnp.stack([jnp.where(cur, tab, NEG), tab])


def _sample_tables(bias, win, dil):
    h = bias.shape[1]
    spread = jnp.concatenate([bias.T[:, :, None], jnp.full((h, KEYS_PER_QUERY + 1, dil - 1), NEG)],
                             axis=2).reshape(h, -1)[:, :KEYS_PER_QUERY * dil + 1]
    tc = _toeplitz(_pad_neg(spread, 0, DEC_T - 1), DEC_T, win, win)
    tn = _toeplitz(_pad_neg(spread, NEW_PAD - 1, 0), DEC_T, NEW_PAD, NEW_PAD - 1)
    return tc, tn


def _cache_transposed(cache):
    bd, wb = cache.shape[:2]
    return jnp.transpose(cache, (0, 2, 3, 4, 1)).reshape(bd, 2, GROUP_W, wb)


def _cache_rows(kvt):
    b, _, _, win = kvt.shape
    kvt = kvt.reshape(b, 2, HEADS_PER_GROUP, HEAD_DIM, win)
    return jnp.transpose(kvt, (0, 4, 1, 2, 3))[None]


TM_INPROJ = 512
TM_OUTPROJ = 512
TQ_ATTN = 4096


def kernel(x_prompt, x_sample, cache_kv_g0, cache_kv_g1, cache_kv_g2, state_conv, w_in, w_att_out,
           w_conv_out, w_o, conv_w, conv_b, rel_bias, norm_mix, norm_mlp, w_ff1, w_ff2, norm_final):
    assert w_in.shape[0] == 1, "one layer"
    b, s, _ = x_prompt.shape
    bd, t, _ = x_sample.shape
    assert t == DEC_T
    wp = w_in[0].astype(BF16)
    wco = w_conv_out[0].astype(BF16)
    wao, wo = w_att_out[0].astype(BF16), w_o[0].astype(BF16)
    w1, w2 = w_ff1[0].astype(BF16), w_ff2[0].astype(BF16)
    nmix, nmlp, nfin = norm_mix[0][None], norm_mlp[0][None], norm_final[None]
    cw, cb = conv_w[0], conv_b[0][None]
    biases = [_group_bias(rel_bias, g) for g in range(N_GROUPS)]

    weights = (wao, wo, w1, w2, nmlp, nfin)

    qkv0, qkv1, qkv2, kvt0, kvt1, kvt2, sga, mb, utail = _inproj_prompt(
        x_prompt, wp, wco, nmix, cw, cb, TM_INPROJ)
    os_, ls_ = [], []
    for g, qkv in enumerate((qkv0, qkv1, qkv2)):
        o, lse = _attn_prompt(qkv, _prompt_table(biases[g]), TQ_ATTN, f"attn_prompt_g{g}")
        os_.append(o)
        ls_.append(lse)
    y_prompt = _outproj_prompt(x_prompt, os_, ls_, sga, mb, weights, TM_OUTPROJ)
    kv_prompt = [_cache_rows(kvt) for kvt in (kvt0, kvt1, kvt2)]
    conv_prompt = utail[:, 6:8][None]

    n = bd * t
    st = state_conv[0]
    p0 = jnp.repeat(st[:, 0], t, axis=0)
    p1 = jnp.repeat(st[:, 1], t, axis=0)
    qkv_s, sga_s, mb_s, u_s = _inproj_sample(
        x_sample.reshape(n, D_MODEL), wp, wco, nmix, cw, cb, p0, p1, min(TM_INPROJ, n))
    tabs = [_sample_tables(biases[g], *GROUPS[g]) for g in range(N_GROUPS)]
    caches_t = [_cache_transposed(c[0]) for c in (cache_kv_g0, cache_kv_g1, cache_kv_g2)]
    os_s, ls_s = _attn_sample(qkv_s, caches_t, [tb[0] for tb in tabs],
                              jnp.stack([tb[1] for tb in tabs]))
    y_sample = _outproj_sample(x_sample.reshape(n, D_MODEL), os_s, ls_s, sga_s, mb_s, weights,
                               min(TM_OUTPROJ, n))
    kv_sample = [qkv_s[:, g * QKV_W + GROUP_W:(g + 1) * QKV_W].reshape(
        1, bd, t, 2, HEADS_PER_GROUP, HEAD_DIM) for g in range(N_GROUPS)]
    conv_sample = u_s.reshape(bd, t, CONV_W)[:, t - 2:][None]

    return (y_prompt, y_sample.reshape(bd, t, D_MODEL),
            kv_prompt[0], kv_prompt[1], kv_prompt[2], conv_prompt,
            kv_sample[0], kv_sample[1], kv_sample[2], conv_sample)
```

```python
import functools

import jax
import jax.numpy as jnp
import numpy as np
from jax import lax
from jax.experimental import pallas as pl
from jax.experimental.pallas import tpu as pltpu

D_MODEL = 1024
HEAD_DIM = 64
HEADS_PER_GROUP = 4
GROUPS = ((128, 1), (512, 4), (2048, 16))
N_GROUPS = 3
GROUP_W = HEADS_PER_GROUP * HEAD_DIM
QKV_W = 3 * GROUP_W
ATT_W = N_GROUPS * GROUP_W
CONV_W = D_MODEL // 2
D_FF = 4 * D_MODEL
N_BUCKETS = 32
MAX_DISTANCE = 2048
KEYS_PER_QUERY = 128
BLOCK = 128
LANES = 128
EPS = 1e-6
SCALE = HEAD_DIM ** -0.5
LOG2E = float(np.log2(np.e))
NEG = -1e30

OFF_CONV = 3 * ATT_W
OFF_GA = OFF_CONV + 3 * CONV_W
OFF_GB = OFF_GA + D_MODEL
PROJ_W = OFF_GB + D_MODEL

VMEM_LIMIT_V7X = 56 * 1024 * 1024
F32 = jnp.float32
BF16 = jnp.bfloat16
_NT = (((1,), (1,)), ((), ()))


def _params(n_axes):
    return pltpu.CompilerParams(
        dimension_semantics=("arbitrary",) * n_axes, vmem_limit_bytes=VMEM_LIMIT_V7X)


def _resident(shape):
    return pl.BlockSpec(shape, lambda *_: (0,) * len(shape), pipeline_mode=pl.Buffered(1))


def _rmsnorm(x, g):
    y = x * lax.rsqrt(jnp.mean(x * x, axis=-1, keepdims=True) + EPS)
    return y * g


def _head_of_lane(shape):
    return lax.broadcasted_iota(jnp.int32, shape, len(shape) - 1) // HEAD_DIM


def _per_head_rows(x):
    head = _head_of_lane(x.shape)
    return jnp.concatenate([jnp.where(head == h, x, 0.0) for h in range(HEADS_PER_GROUP)], axis=0)


def _head_diagonal(x, t):
    head = _head_of_lane((t, GROUP_W))
    out = x[0:t, :]
    for h in range(1, HEADS_PER_GROUP):
        out = jnp.where(head == h, x[h * t:(h + 1) * t, :], out)
    return out


MAX_SUBLANE_STRIDE = 4


def _rows_by_residue(slab_ref, stage_ref, dil):
    t = slab_ref.shape[0]
    if dil <= MAX_SUBLANE_STRIDE:
        return [(r, slab_ref[pl.ds(r, t // dil, stride=dil), :]) for r in range(dil)]
    s1, s2 = MAX_SUBLANE_STRIDE, dil // MAX_SUBLANE_STRIDE
    assert s1 * s2 == dil and s2 <= MAX_SUBLANE_STRIDE
    for r1 in range(s1):
        stage_ref[pl.ds(r1 * (t // s1), t // s1), :] = slab_ref[pl.ds(r1, t // s1, stride=s1), :]
    out = {}
    for r1 in range(s1):
        for k in range(s2):
            out[s1 * k + r1] = stage_ref[pl.ds(r1 * (t // s1) + k, t // dil, stride=s2), :]
    return sorted(out.items())


def _inproj_body(*refs, sample):
    if sample:
        (x_ref, win_ref, wco_ref, nm_ref, cw_ref, cb_ref, p0_ref, p1_ref,
         qkv_ref, sga_ref, mb_ref, u_ref) = refs
    else:
        (x_ref, win_ref, wco_ref, nm_ref, cw_ref, cb_ref,
         sqkv_ref, c0_ref, c1_ref, c2_ref, tc0_ref, tc1_ref, tc2_ref, tn_ref,
         qkv0_ref, qkv1_ref, qkv2_ref, kvt0_ref, kvt1_ref, kvt2_ref, sga_ref, mb_ref, u_ref,
         so0_ref, so1_ref, so2_ref, sl0_ref, sl1_ref, sl2_ref,
         carry_ref, perm_ref, stage_ref) = refs
        qkv_refs = (qkv0_ref, qkv1_ref, qkv2_ref)
        kvt_refs = (kvt0_ref, kvt1_ref, kvt2_ref)
        cache_refs = (c0_ref, c1_ref, c2_ref)
    x = x_ref[...]
    tm = x.shape[0]
    xn = _rmsnorm(x, nm_ref[...]).astype(BF16)

    def proj(lo, hi):
        return jnp.dot(xn, win_ref[:, lo:hi], preferred_element_type=F32)

    pc = proj(OFF_CONV, OFF_GA)
    ga = proj(OFF_GA, OFF_GB)
    sga_ref[...] = jax.nn.sigmoid(ga).astype(sga_ref.dtype)
    sgb = jax.nn.sigmoid(proj(OFF_GB, PROJ_W))
    if not sample:
        sample_scores = _sample_scores(sqkv_ref, cache_refs, (tc0_ref, tc1_ref, tc2_ref), tn_ref)
    dilated = {which: proj(which * ATT_W + GROUP_W, (which + 1) * ATT_W) for which in (0, 1)}
    cb, cc, ch = pc[:, :CONV_W], pc[:, CONV_W:2 * CONV_W], pc[:, 2 * CONV_W:]
    u = cc * ch
    row = lax.broadcasted_iota(jnp.int32, (tm, 1), 0)
    if sample:
        p0, p1 = p0_ref[...], p1_ref[...]
        rs = jnp.bitwise_and(row, 7)
        u_ref[...] = u
    else:
        seq_start = pl.program_id(1) == 0
        p0 = jnp.where(seq_start, 0.0, carry_ref[6:7, :])
        p1 = jnp.where(seq_start, 0.0, carry_ref[7:8, :])
        rs = row
        u_ref[...] = u[tm - 8:, :]
    um1 = jnp.where(rs == 0, p1, pltpu.roll(u, 1, 0))
    um2 = jnp.where(rs == 0, p0, jnp.where(rs == 1, p1, pltpu.roll(u, 2, 0)))
    if not sample:
        carry_ref[...] = u[tm - 8:, :]
    z = cb_ref[...] + cw_ref[0:1, :] * um2 + cw_ref[1:2, :] * um1 + cw_ref[2:3, :] * u
    branch_b = jnp.dot((cb * z).astype(BF16), wco_ref[...], preferred_element_type=F32)
    mb_ref[...] = (sgb * branch_b).astype(mb_ref.dtype)
    dilated[2] = proj(2 * ATT_W + GROUP_W, 3 * ATT_W)
    if not sample:
        _sample_outputs(sample_scores, sqkv_ref, cache_refs, (so0_ref, so1_ref, so2_ref),
                        (sl0_ref, sl1_ref, sl2_ref))

    slab = 0
    for g, (_, dil) in reversed(list(enumerate(GROUPS))):
        for which in range(3):
            if dil == 1:
                p = proj(which * ATT_W, which * ATT_W + GROUP_W)
            else:
                p = dilated[which][:, (g - 1) * GROUP_W:g * GROUP_W]
            if which == 0:
                p = p * (SCALE * LOG2E)
            elif not sample:
                keep = kvt_refs[g].shape[-1]
                kvt_refs[g][which - 1] = p[tm - keep:, :].T
            dst = which * GROUP_W
            if sample:
                qkv_ref[:, g * QKV_W + dst:g * QKV_W + dst + GROUP_W] = p
            elif dil == 1:
                qkv_refs[g][0, :, dst:dst + GROUP_W] = p.astype(BF16)
            else:
                for half in range(GROUP_W // LANES):
                    perm_ref[slab] = p[:, half * LANES:(half + 1) * LANES]
                    lo = dst + half * LANES
                    for r, rows in _rows_by_residue(perm_ref.at[slab], stage_ref.at[slab], dil):
                        qkv_refs[g][r, :, lo:lo + LANES] = rows.astype(BF16)
                    slab += 1


def _inproj_prompt(x, w_in, w_conv_out, norm_mix, conv_w, conv_b, sample_qkv, caches_t, tcs, tn, tm):
    b, s, _ = x.shape
    nt = s // tm
    n_sample = sample_qkv.shape[0]
    seqs = n_sample // DEC_T // (b * nt)
    assert seqs >= 1 and seqs * b * nt * DEC_T == n_sample
    seq_block = lambda bi, i: bi * nt + i
    sample_in = (
        [pl.BlockSpec((seqs * DEC_T, N_GROUPS * QKV_W), lambda bi, i: (seq_block(bi, i), 0))]
        + [pl.BlockSpec((seqs,) + c.shape[1:], lambda bi, i: (seq_block(bi, i), 0, 0, 0))
           for c in caches_t]
        + [_resident(tc.shape) for tc in tcs] + [_resident(tn.shape)])
    sample_out_spec = pl.BlockSpec((seqs * DEC_T, GROUP_W), lambda bi, i: (seq_block(bi, i), 0))
    sample_out = jax.ShapeDtypeStruct((n_sample, GROUP_W), F32)
    tok = lambda w: pl.BlockSpec((None, tm, w), lambda bi, i: (bi, i, 0))
    qkv_specs = [pl.BlockSpec((None, dil, tm // dil, QKV_W), lambda bi, i: (bi, 0, i, 0))
                 for _, dil in GROUPS]
    n_slabs = sum(3 * GROUP_W // LANES for _, dil in GROUPS if dil > 1)

    def kvt_spec(win):
        keep = min(win, tm)
        first = (s - win) // tm
        return pl.BlockSpec((None, 2, GROUP_W, keep),
                            lambda bi, i: (bi, 0, 0, jnp.maximum(i - first, 0)))

    out_shape = (
        [jax.ShapeDtypeStruct((b, dil, s // dil, QKV_W), BF16) for _, dil in GROUPS]
        + [jax.ShapeDtypeStruct((b, 2, GROUP_W, win), F32) for win, _ in GROUPS]
        + [jax.ShapeDtypeStruct((b, s, D_MODEL), BF16),
           jax.ShapeDtypeStruct((b, s, D_MODEL), BF16),
           jax.ShapeDtypeStruct((b, 8, CONV_W), F32)]
        + [sample_out] * (2 * N_GROUPS))
    return pl.pallas_call(
        functools.partial(_inproj_body, sample=False),
        grid=(b, nt),
        in_specs=[tok(D_MODEL), _resident(w_in.shape), _resident(w_conv_out.shape),
                  _resident(norm_mix.shape), _resident(conv_w.shape), _resident(conv_b.shape)]
        + sample_in,
        out_specs=qkv_specs + [kvt_spec(win) for win, _ in GROUPS]
        + [tok(D_MODEL), tok(D_MODEL), pl.BlockSpec((None, 8, CONV_W), lambda bi, i: (bi, 0, 0))]
        + [sample_out_spec] * (2 * N_GROUPS),
        out_shape=out_shape,
        scratch_shapes=[pltpu.VMEM((8, CONV_W), F32), pltpu.VMEM((n_slabs, tm, LANES), F32),
                        pltpu.VMEM((n_slabs, tm, LANES), F32)],
        compiler_params=_params(2),
        name="inproj_prompt",
    )(x, w_in, w_conv_out, norm_mix, conv_w, conv_b, sample_qkv, *caches_t, *tcs, tn)


def _inproj_sample(x, w_in, w_conv_out, norm_mix, conv_w, conv_b, p0, p1, tm):
    n = x.shape[0]
    tok = lambda w: pl.BlockSpec((tm, w), lambda i: (i, 0))
    out_shape = (
        jax.ShapeDtypeStruct((n, N_GROUPS * QKV_W), F32),
        jax.ShapeDtypeStruct((n, D_MODEL), BF16),
        jax.ShapeDtypeStruct((n, D_MODEL), BF16),
        jax.ShapeDtypeStruct((n, CONV_W), F32),
    )
    return pl.pallas_call(
        functools.partial(_inproj_body, sample=True),
        grid=(n // tm,),
        in_specs=[tok(D_MODEL), _resident(w_in.shape), _resident(w_conv_out.shape),
                  _resident(norm_mix.shape), _resident(conv_w.shape), _resident(conv_b.shape),
                  tok(CONV_W), tok(CONV_W)],
        out_specs=[tok(N_GROUPS * QKV_W), tok(D_MODEL), tok(D_MODEL), tok(CONV_W)],
        out_shape=out_shape,
        compiler_params=_params(1),
        name="inproj_sample",
    )(x, w_in, w_conv_out, norm_mix, conv_w, conv_b, p0, p1)


def _attn_prompt_body(q_ref, kc_ref, kp_ref, vc_ref, vp_ref, tab_ref, o_ref, l_ref):
    i = pl.program_id(2)
    first_tile = jnp.where(i == 0, 0, 1)
    head = _head_of_lane((BLOCK, GROUP_W))
    low_head = lax.broadcasted_iota(jnp.int32, (BLOCK, LANES), 1) < HEAD_DIM
    ones = jnp.ones((2 * BLOCK, LANES), BF16)
    n_cls, tq, _ = q_ref.shape
    for c, j in [(c, j) for c in range(n_cls) for j in range(tq // BLOCK)]:
        q = q_ref[c, j * BLOCK:(j + 1) * BLOCK, :]
        qm = jnp.concatenate(
            [jnp.where(head == h, q, jnp.zeros_like(q)) for h in range(HEADS_PER_GROUP)], axis=0)
        if j == 0:
            k = jnp.concatenate([kp_ref[c], kc_ref[c, 0:BLOCK, :]], axis=0)
            v = jnp.concatenate([vp_ref[c], vc_ref[c, 0:BLOCK, :]], axis=0)
            tab = tab_ref[first_tile]
        else:
            k = kc_ref[c, (j - 1) * BLOCK:(j + 1) * BLOCK, :]
            v = vc_ref[c, (j - 1) * BLOCK:(j + 1) * BLOCK, :]
            tab = tab_ref[1]
        s = lax.dot_general(qm, k, _NT, preferred_element_type=F32) + tab
        m = jnp.max(s, axis=-1, keepdims=True)
        p = jnp.exp2(s - m).astype(BF16)
        halves_o, halves_l = [], []
        for pair in range(HEADS_PER_GROUP // 2):
            w = jnp.concatenate([v[:, pair * LANES:(pair + 1) * LANES], ones], axis=1)
            r = jnp.dot(p[2 * pair * BLOCK:2 * (pair + 1) * BLOCK, :], w,
                        preferred_element_type=F32)
            m0 = jnp.broadcast_to(m[2 * pair * BLOCK:(2 * pair + 1) * BLOCK], (BLOCK, LANES))
            m1 = jnp.broadcast_to(m[(2 * pair + 1) * BLOCK:2 * (pair + 1) * BLOCK], (BLOCK, LANES))
            acc = jnp.where(low_head, r[:BLOCK, :LANES], r[BLOCK:, :LANES])
            l = jnp.where(low_head, r[:BLOCK, LANES:], r[BLOCK:, LANES:])
            halves_o.append(acc / l)
            halves_l.append(jnp.where(low_head, m0, m1) + jnp.log2(l))
        o_ref[c, j * BLOCK:(j + 1) * BLOCK, :] = (
            jnp.concatenate(halves_o, axis=1).astype(o_ref.dtype))
        l_ref[c, j * BLOCK:(j + 1) * BLOCK, :] = jnp.concatenate(halves_l, axis=1)


def _attn_prompt(qkv, tab, rows_per_step, name):
    b, dil, L, _ = qkv.shape
    tq = min(rows_per_step, L)
    n_cls = min(rows_per_step // tq, dil)
    cur_spec = lambda which: pl.BlockSpec(
        (None, n_cls, tq, GROUP_W), lambda bi, r, i: (bi, r, i, which))
    prev_spec = lambda which: pl.BlockSpec(
        (None, n_cls, BLOCK, GROUP_W),
        lambda bi, r, i: (bi, r, jnp.maximum(i * (tq // BLOCK) - 1, 0), which))
    out_spec = pl.BlockSpec((None, n_cls, tq, GROUP_W), lambda bi, r, i: (bi, r, i, 0))
    return pl.pallas_call(
        _attn_prompt_body,
        grid=(b, dil // n_cls, L // tq),
        in_specs=[cur_spec(0), cur_spec(1), prev_spec(1), cur_spec(2), prev_spec(2),
                  _resident(tab.shape)],
        out_specs=[out_spec, out_spec],
        out_shape=(jax.ShapeDtypeStruct((b, dil, L, GROUP_W), BF16),
                   jax.ShapeDtypeStruct((b, dil, L, GROUP_W), F32)),
        compiler_params=_params(3),
        name=name,
    )(qkv, qkv, qkv, qkv, qkv, tab)


DEC_T = 8
NEW_PAD = 16


def _sample_scores(qkv_ref, cache_refs, tc_refs, tn_ref):
    pad = jnp.zeros((NEW_PAD - DEC_T, GROUP_W), F32)
    scores = []
    for i in range(cache_refs[0].shape[0]):
        rows = slice(i * DEC_T, (i + 1) * DEC_T)
        for g in range(N_GROUPS):
            base = g * QKV_W
            q = qkv_ref[rows, base:base + GROUP_W]
            kn = jnp.concatenate([qkv_ref[rows, base + GROUP_W:base + 2 * GROUP_W], pad], axis=0)
            qm = _per_head_rows(q).astype(BF16)
            s_n = (lax.dot_general(qm, kn.astype(BF16), _NT, preferred_element_type=F32)
                   + tn_ref[g])
            s_c = (jnp.dot(qm, cache_refs[g][i, 0].astype(BF16), preferred_element_type=F32)
                   + tc_refs[g][...])
            scores.append((s_c, s_n))
    return scores


def _sample_outputs(scores, qkv_ref, cache_refs, o_refs, l_refs):
    pad = jnp.zeros((NEW_PAD - DEC_T, GROUP_W), F32)
    scores = iter(scores)
    for i in range(cache_refs[0].shape[0]):
        rows = slice(i * DEC_T, (i + 1) * DEC_T)
        for g in range(N_GROUPS):
            s_c, s_n = next(scores)
            base = g * QKV_W
            vn = jnp.concatenate([qkv_ref[rows, base + 2 * GROUP_W:base + QKV_W], pad], axis=0)
            m = jnp.maximum(jnp.max(s_c, axis=-1, keepdims=True),
                            jnp.max(s_n, axis=-1, keepdims=True))
            p_c = jnp.exp2(s_c - m)
            p_n = jnp.exp2(s_n - m)
            l = jnp.sum(p_c, axis=-1, keepdims=True) + jnp.sum(p_n, axis=-1, keepdims=True)
            acc = jnp.dot(p_n.astype(BF16), vn.astype(BF16), preferred_element_type=F32)
            acc = acc + lax.dot_general(p_c.astype(BF16), cache_refs[g][i, 1].astype(BF16), _NT,
                                        preferred_element_type=F32)
            res = acc / l
            lse = jnp.broadcast_to(m + jnp.log2(l), res.shape)
            o_refs[g][rows, :] = _head_diagonal(res, DEC_T)
            l_refs[g][rows, :] = _head_diagonal(lse, DEC_T)


FF_CHUNK = 1024


def _natural_rows(ref, scratch_ref, slab):
    dil, rows, _ = ref.shape
    if dil == 1:
        return ref[0].astype(F32)
    halves = []
    for half in range(GROUP_W // LANES):
        for r in range(dil):
            scratch_ref[slab + half, pl.ds(r, rows, stride=dil), :] = (
                ref[r, :, half * LANES:(half + 1) * LANES].astype(F32))
        halves.append(scratch_ref[slab + half])
    return jnp.concatenate(halves, axis=1)


def _outproj_body(x_ref, o0_ref, o1_ref, o2_ref, l0_ref, l1_ref, l2_ref, sga_ref, mb_ref,
                  wao_ref, wo_ref, w1_ref, w2_ref, nmlp_ref, nfin_ref, y_ref, *scratch):
    if scratch:
        slabs = GROUP_W // LANES
        os_ = [_natural_rows(ref, scratch[0], slabs * k) for k, ref in
               enumerate((o0_ref, o1_ref, o2_ref))]
        ls_ = [_natural_rows(ref, scratch[0], slabs * (3 + k)) for k, ref in
               enumerate((l0_ref, l1_ref, l2_ref))]
    else:
        os_ = [ref[...] for ref in (o0_ref, o1_ref, o2_ref)]
        ls_ = [ref[...] for ref in (l0_ref, l1_ref, l2_ref)]
    (o0, o1, o2), (l0, l1, l2) = os_, ls_
    m = jnp.maximum(l0, jnp.maximum(l1, l2))
    e0, e1, e2 = jnp.exp2(l0 - m), jnp.exp2(l1 - m), jnp.exp2(l2 - m)
    att = (e0 * o0 + e1 * o1 + e2 * o2) / (e0 + e1 + e2)
    branch_a = jnp.dot(att.astype(BF16), wao_ref[...], preferred_element_type=F32)
    mix = sga_ref[...].astype(F32) * branch_a + mb_ref[...].astype(F32)
    x1 = x_ref[...] + jnp.dot(mix.astype(BF16), wo_ref[...], preferred_element_type=F32)
    h = _rmsnorm(x1, nmlp_ref[...]).astype(BF16)
    acc = x1
    for c in range(D_FF // FF_CHUNK):
        a = jnp.dot(h, w1_ref[:, c * FF_CHUNK:(c + 1) * FF_CHUNK], preferred_element_type=F32)
        a = jnp.square(jnp.maximum(a, 0.0)).astype(BF16)
        acc = acc + jnp.dot(a, w2_ref[c * FF_CHUNK:(c + 1) * FF_CHUNK, :],
                            preferred_element_type=F32)
    y_ref[...] = _rmsnorm(acc, nfin_ref[...])


def _outproj_prompt(x, os_, ls_, sga, mb, weights, tm):
    b, s, _ = x.shape
    tok = pl.BlockSpec((None, tm, D_MODEL), lambda bi, i: (bi, i, 0))
    grouped = [pl.BlockSpec((None, dil, tm // dil, GROUP_W), lambda bi, i: (bi, 0, i, 0))
               for _, dil in GROUPS]
    return pl.pallas_call(
        _outproj_body,
        grid=(b, s // tm),
        in_specs=[tok] + grouped + grouped + [tok, tok] + [_resident(w.shape) for w in weights],
        out_specs=tok,
        out_shape=jax.ShapeDtypeStruct((b, s, D_MODEL), F32),
        scratch_shapes=[pltpu.VMEM((2 * N_GROUPS * GROUP_W // LANES, tm, LANES), F32)],
        compiler_params=_params(2),
        name="outproj_ffn_prompt",
    )(x, *os_, *ls_, sga, mb, *weights)


def _outproj_sample(x, os_, ls_, sga, mb, weights, tm):
    n = x.shape[0]
    tok = lambda w: pl.BlockSpec((tm, w), lambda i: (i, 0))
    return pl.pallas_call(
        _outproj_body,
        grid=(n // tm,),
        in_specs=[tok(D_MODEL)] + [tok(GROUP_W)] * 6 + [tok(D_MODEL), tok(D_MODEL)]
        + [_resident(w.shape) for w in weights],
        out_specs=tok(D_MODEL),
        out_shape=jax.ShapeDtypeStruct((n, D_MODEL), F32),
        compiler_params=_params(1),
        name="outproj_ffn_sample",
    )(x, *os_, *ls_, sga, mb, *weights)


def _t5_bucket(dist):
    n = np.asarray(dist)
    max_exact = N_BUCKETS // 2
    large = max_exact + (np.log(np.maximum(n, 1) / max_exact) / np.log(MAX_DISTANCE / max_exact)
                         * (N_BUCKETS - max_exact)).astype(np.int32)
    large = np.minimum(large, N_BUCKETS - 1)
    return np.where(n < max_exact, n, large).astype(np.int32)


def _group_bias(rel_bias, g):
    dil = GROUPS[g][1]
    buckets = _t5_bucket(np.arange(KEYS_PER_QUERY + 1) * dil)
    bias = rel_bias[buckets][:, g * HEADS_PER_GROUP:(g + 1) * HEADS_PER_GROUP].astype(F32)
    return bias * LOG2E


def _toeplitz(vec, n_rows, n_cols, offset):
    h, n = vec.shape
    start = n - 1 - offset
    assert start - (n_rows - 1) >= 0 and start + n_cols <= n
    w = max(n, start + n_cols + 1)
    rev = jnp.pad(vec[:, ::-1], ((0, 0), (0, w - n)))
    skew = jnp.tile(rev, (1, n_rows))[:, :n_rows * (w - 1)].reshape(h, n_rows, w - 1)
    return skew[:, :, start:start + n_cols].reshape(h * n_rows, n_cols)


def _pad_neg(vec, before, after):
    h = vec.shape[0]
    return jnp.concatenate([jnp.full((h, before), NEG, F32), vec, jnp.full((h, after), NEG, F32)],
                           axis=1)


def _prompt_table(bias):
    by_stride = _pad_neg(bias.T, BLOCK - 1, BLOCK - 1)
    tab = _toeplitz(by_stride, BLOCK, 2 * BLOCK, 2 * BLOCK - 1)
    cur = (np.arange(2 * BLOCK) >= BLOCK)[None, :]
    return jnp.stack([jnp.where(cur, tab, NEG), tab])


def _sample_tables(bias, win, dil):
    h = bias.shape[1]
    spread = jnp.concatenate([bias.T[:, :, None], jnp.full((h, KEYS_PER_QUERY + 1, dil - 1), NEG)],
                             axis=2).reshape(h, -1)[:, :KEYS_PER_QUERY * dil + 1]
    tc = _toeplitz(_pad_neg(spread, 0, DEC_T - 1), DEC_T, win, win)
    tn = _toeplitz(_pad_neg(spread, NEW_PAD - 1, 0), DEC_T, NEW_PAD, NEW_PAD - 1)
    return tc, tn


def _cache_transposed(cache):
    bd, wb = cache.shape[:2]
    return jnp.transpose(cache, (0, 2, 3, 4, 1)).reshape(bd, 2, GROUP_W, wb)


def _cache_rows(kvt):
    b, _, _, win = kvt.shape
    kvt = kvt.reshape(b, 2, HEADS_PER_GROUP, HEAD_DIM, win)
    return jnp.transpose(kvt, (0, 4, 1, 2, 3))[None]


TM_INPROJ = 512
TM_OUTPROJ = 512
TQ_ATTN = 4096


def kernel(x_prompt, x_sample, cache_kv_g0, cache_kv_g1, cache_kv_g2, state_conv, w_in, w_att_out,
           w_conv_out, w_o, conv_w, conv_b, rel_bias, norm_mix, norm_mlp, w_ff1, w_ff2, norm_final):
    assert w_in.shape[0] == 1, "one layer"
    b, s, _ = x_prompt.shape
    bd, t, _ = x_sample.shape
    assert t == DEC_T
    wp = w_in[0].astype(BF16)
    wco = w_conv_out[0].astype(BF16)
    wao, wo = w_att_out[0].astype(BF16), w_o[0].astype(BF16)
    w1, w2 = w_ff1[0].astype(BF16), w_ff2[0].astype(BF16)
    nmix, nmlp, nfin = norm_mix[0][None], norm_mlp[0][None], norm_final[None]
    cw, cb = conv_w[0], conv_b[0][None]
    biases = [_group_bias(rel_bias, g) for g in range(N_GROUPS)]

    weights = (wao, wo, w1, w2, nmlp, nfin)

    n = bd * t
    st = state_conv[0]
    p0 = jnp.repeat(st[:, 0], t, axis=0)
    p1 = jnp.repeat(st[:, 1], t, axis=0)
    qkv_s, sga_s, mb_s, u_s = _inproj_sample(
        x_sample.reshape(n, D_MODEL), wp, wco, nmix, cw, cb, p0, p1, min(TM_INPROJ, n))
    tabs = [_sample_tables(biases[g], *GROUPS[g]) for g in range(N_GROUPS)]
    caches_t = [_cache_transposed(c[0]) for c in (cache_kv_g0, cache_kv_g1, cache_kv_g2)]

    (qkv0, qkv1, qkv2, kvt0, kvt1, kvt2, sga, mb, utail, *sample_att) = _inproj_prompt(
        x_prompt, wp, wco, nmix, cw, cb, qkv_s, caches_t, [tb[0] for tb in tabs],
        jnp.stack([tb[1] for tb in tabs]), TM_INPROJ)
    os_s, ls_s = sample_att[:N_GROUPS], sample_att[N_GROUPS:]
    os_, ls_ = [], []
    for g, qkv in enumerate((qkv0, qkv1, qkv2)):
        o, lse = _attn_prompt(qkv, _prompt_table(biases[g]), TQ_ATTN, f"attn_prompt_g{g}")
        os_.append(o)
        ls_.append(lse)
    y_prompt = _outproj_prompt(x_prompt, os_, ls_, sga, mb, weights, TM_OUTPROJ)
    kv_prompt = [_cache_rows(kvt) for kvt in (kvt0, kvt1, kvt2)]
    conv_prompt = utail[:, 6:8][None]

    y_sample = _outproj_sample(x_sample.reshape(n, D_MODEL), os_s, ls_s, sga_s, mb_s, weights,
                               min(TM_OUTPROJ, n))
    kv_sample = [qkv_s[:, g * QKV_W + GROUP_W:(g + 1) * QKV_W].reshape(
        1, bd, t, 2, HEADS_PER_GROUP, HEAD_DIM) for g in range(N_GROUPS)]
    conv_sample = u_s.reshape(bd, t, CONV_W)[:, t - 2:][None]

    return (y_prompt, y_sample.reshape(bd, t, D_MODEL),
            kv_prompt[0], kv_prompt[1], kv_prompt[2], conv_prompt,
            kv_sample[0], kv_sample[1], kv_sample[2], conv_sample)
```

```python
import functools

import jax
import jax.numpy as jnp
import numpy as np
from jax import lax
from jax.experimental import pallas as pl
from jax.experimental.pallas import tpu as pltpu

D_MODEL = 1024
HEAD_DIM = 64
HEADS_PER_GROUP = 4
GROUPS = ((128, 1), (512, 4), (2048, 16))
N_GROUPS = 3
GROUP_W = HEADS_PER_GROUP * HEAD_DIM
QKV_W = 3 * GROUP_W
ATT_W = N_GROUPS * GROUP_W
CONV_W = D_MODEL // 2
D_FF = 4 * D_MODEL
N_BUCKETS = 32
MAX_DISTANCE = 2048
KEYS_PER_QUERY = 128
BLOCK = 128
LANES = 128
SUBLANES = 8
EPS = 1e-6
SCALE = HEAD_DIM ** -0.5
LOG2E = float(np.log2(np.e))
NEG = -1e30

OFF_CONV = 3 * ATT_W
OFF_GA = OFF_CONV + 3 * CONV_W
OFF_GB = OFF_GA + D_MODEL
PROJ_W = OFF_GB + D_MODEL

VMEM_LIMIT_V7X = 56 * 1024 * 1024
F32 = jnp.float32
BF16 = jnp.bfloat16
_NT = (((1,), (1,)), ((), ()))


def _params(n_axes):
    return pltpu.CompilerParams(
        dimension_semantics=("arbitrary",) * n_axes, vmem_limit_bytes=VMEM_LIMIT_V7X)


def _resident(shape):
    return pl.BlockSpec(shape, lambda *_: (0,) * len(shape), pipeline_mode=pl.Buffered(1))


def _rmsnorm(x, g):
    y = x * lax.rsqrt(jnp.mean(x * x, axis=-1, keepdims=True) + EPS)
    return y * g


def _head_of_lane(shape):
    return lax.broadcasted_iota(jnp.int32, shape, len(shape) - 1) // HEAD_DIM


def _per_head_rows(x):
    head = _head_of_lane(x.shape)
    return jnp.concatenate([jnp.where(head == h, x, 0.0) for h in range(HEADS_PER_GROUP)], axis=0)


def _head_diagonal(x, t):
    head = _head_of_lane((t, GROUP_W))
    out = x[0:t, :]
    for h in range(1, HEADS_PER_GROUP):
        out = jnp.where(head == h, x[h * t:(h + 1) * t, :], out)
    return out


MAX_SUBLANE_STRIDE = 4


def _rows_by_residue(slab_ref, stage_ref, dil):
    t = slab_ref.shape[0]
    if dil <= MAX_SUBLANE_STRIDE:
        return [(r, slab_ref[pl.ds(r, t // dil, stride=dil), :]) for r in range(dil)]
    s1, s2 = MAX_SUBLANE_STRIDE, dil // MAX_SUBLANE_STRIDE
    assert s1 * s2 == dil and s2 <= MAX_SUBLANE_STRIDE
    for r1 in range(s1):
        stage_ref[pl.ds(r1 * (t // s1), t // s1), :] = slab_ref[pl.ds(r1, t // s1, stride=s1), :]
    out = {}
    for r1 in range(s1):
        for k in range(s2):
            out[s1 * k + r1] = stage_ref[pl.ds(r1 * (t // s1) + k, t // dil, stride=s2), :]
    return sorted(out.items())


def _inproj_body(*refs, sample):
    if sample:
        (x_ref, win_ref, wco_ref, nm_ref, cw_ref, cb_ref, p0_ref, p1_ref,
         qkv_ref, sga_ref, mb_ref, u_ref) = refs
    else:
        (x_ref, win_ref, wco_ref, nm_ref, cw_ref, cb_ref,
         sqkv_ref, c0_ref, c1_ref, c2_ref, tc0_ref, tc1_ref, tc2_ref, tn_ref,
         qkv0_ref, qkv1_ref, qkv2_ref, kvt0_ref, kvt1_ref, kvt2_ref, sga_ref, mb_ref, u_ref,
         so0_ref, so1_ref, so2_ref, sl0_ref, sl1_ref, sl2_ref,
         carry_ref, perm_ref, stage_ref) = refs
        qkv_refs = (qkv0_ref, qkv1_ref, qkv2_ref)
        kvt_refs = (kvt0_ref, kvt1_ref, kvt2_ref)
        cache_refs = (c0_ref, c1_ref, c2_ref)
    x = x_ref[...]
    tm = x.shape[0]
    xn = _rmsnorm(x, nm_ref[...]).astype(BF16)

    def proj(lo, hi):
        return jnp.dot(xn, win_ref[:, lo:hi], preferred_element_type=F32)

    pc = proj(OFF_CONV, OFF_GA)
    ga = proj(OFF_GA, OFF_GB)
    sga_ref[...] = jax.nn.sigmoid(ga).astype(sga_ref.dtype)
    sgb = jax.nn.sigmoid(proj(OFF_GB, PROJ_W))
    if not sample:
        sample_scores = _sample_scores(sqkv_ref, cache_refs, (tc0_ref, tc1_ref, tc2_ref), tn_ref)
    dilated = {which: proj(which * ATT_W + GROUP_W, (which + 1) * ATT_W) for which in (0, 1)}
    cb, cc, ch = pc[:, :CONV_W], pc[:, CONV_W:2 * CONV_W], pc[:, 2 * CONV_W:]
    u = cc * ch
    row = lax.broadcasted_iota(jnp.int32, (tm, 1), 0)
    if sample:
        p0, p1 = p0_ref[...], p1_ref[...]
        rs = jnp.bitwise_and(row, DEC_T - 1)
        u_ref[...] = u
    else:
        seq_start = pl.program_id(1) == 0
        p0 = jnp.where(seq_start, 0.0, carry_ref[SUBLANES - 2:SUBLANES - 1, :])
        p1 = jnp.where(seq_start, 0.0, carry_ref[SUBLANES - 1:SUBLANES, :])
        rs = row
        u_ref[...] = u[tm - SUBLANES:, :]
    um1 = jnp.where(rs == 0, p1, pltpu.roll(u, 1, 0))
    um2 = jnp.where(rs == 0, p0, jnp.where(rs == 1, p1, pltpu.roll(u, 2, 0)))
    if not sample:
        carry_ref[...] = u[tm - SUBLANES:, :]
    z = cb_ref[...] + cw_ref[0:1, :] * um2 + cw_ref[1:2, :] * um1 + cw_ref[2:3, :] * u
    branch_b = jnp.dot((cb * z).astype(BF16), wco_ref[...], preferred_element_type=F32)
    mb_ref[...] = (sgb * branch_b).astype(mb_ref.dtype)
    dilated[2] = proj(2 * ATT_W + GROUP_W, 3 * ATT_W)
    if not sample:
        _sample_outputs(sample_scores, sqkv_ref, cache_refs, (so0_ref, so1_ref, so2_ref),
                        (sl0_ref, sl1_ref, sl2_ref))

    slab = 0
    for g, (_, dil) in reversed(list(enumerate(GROUPS))):
        for which in (1, 2, 0):
            if dil == 1:
                p = proj(which * ATT_W, which * ATT_W + GROUP_W)
            else:
                p = dilated[which][:, (g - 1) * GROUP_W:g * GROUP_W]
            if which == 0:
                p = p * (SCALE * LOG2E)
            elif not sample:
                keep = kvt_refs[g].shape[-1]
                kvt_refs[g][which - 1] = p[tm - keep:, :].T
            dst = which * GROUP_W
            if sample:
                qkv_ref[:, g * QKV_W + dst:g * QKV_W + dst + GROUP_W] = p
            elif dil == 1:
                qkv_refs[g][0, :, dst:dst + GROUP_W] = p.astype(BF16)
            else:
                for half in range(GROUP_W // LANES):
                    perm_ref[slab] = p[:, half * LANES:(half + 1) * LANES]
                    lo = dst + half * LANES
                    for r, rows in _rows_by_residue(perm_ref.at[slab], stage_ref.at[slab], dil):
                        qkv_refs[g][r, :, lo:lo + LANES] = rows.astype(BF16)
                    slab += 1


def _inproj_prompt(x, w_in, w_conv_out, norm_mix, conv_w, conv_b, sample_qkv, caches_t, tcs, tn, tm):
    b, s, _ = x.shape
    nt = s // tm
    n_sample = sample_qkv.shape[0]
    seqs = n_sample // DEC_T // (b * nt)
    assert seqs >= 1 and seqs * b * nt * DEC_T == n_sample
    seq_block = lambda bi, i: bi * nt + i
    sample_in = (
        [pl.BlockSpec((seqs * DEC_T, N_GROUPS * QKV_W), lambda bi, i: (seq_block(bi, i), 0))]
        + [pl.BlockSpec((seqs,) + c.shape[1:], lambda bi, i: (seq_block(bi, i), 0, 0, 0))
           for c in caches_t]
        + [_resident(tc.shape) for tc in tcs] + [_resident(tn.shape)])
    sample_out_spec = pl.BlockSpec((seqs * DEC_T, GROUP_W), lambda bi, i: (seq_block(bi, i), 0))
    sample_out = jax.ShapeDtypeStruct((n_sample, GROUP_W), F32)
    tok = lambda w: pl.BlockSpec((None, tm, w), lambda bi, i: (bi, i, 0))
    qkv_specs = [pl.BlockSpec((None, dil, tm // dil, QKV_W), lambda bi, i: (bi, 0, i, 0))
                 for _, dil in GROUPS]
    n_slabs = sum(3 * GROUP_W // LANES for _, dil in GROUPS if dil > 1)

    def kvt_spec(win):
        keep = min(win, tm)
        first = (s - win) // tm
        return pl.BlockSpec((None, 2, GROUP_W, keep),
                            lambda bi, i: (bi, 0, 0, jnp.maximum(i - first, 0)))

    out_shape = (
        [jax.ShapeDtypeStruct((b, dil, s // dil, QKV_W), BF16) for _, dil in GROUPS]
        + [jax.ShapeDtypeStruct((b, 2, GROUP_W, win), F32) for win, _ in GROUPS]
        + [jax.ShapeDtypeStruct((b, s, D_MODEL), BF16),
           jax.ShapeDtypeStruct((b, s, D_MODEL), BF16),
           jax.ShapeDtypeStruct((b, SUBLANES, CONV_W), F32)]
        + [sample_out] * (2 * N_GROUPS))
    return pl.pallas_call(
        functools.partial(_inproj_body, sample=False),
        grid=(b, nt),
        in_specs=[tok(D_MODEL), _resident(w_in.shape), _resident(w_conv_out.shape),
                  _resident(norm_mix.shape), _resident(conv_w.shape), _resident(conv_b.shape)]
        + sample_in,
        out_specs=qkv_specs + [kvt_spec(win) for win, _ in GROUPS]
        + [tok(D_MODEL), tok(D_MODEL),
           pl.BlockSpec((None, SUBLANES, CONV_W), lambda bi, i: (bi, 0, 0))]
        + [sample_out_spec] * (2 * N_GROUPS),
        out_shape=out_shape,
        scratch_shapes=[pltpu.VMEM((SUBLANES, CONV_W), F32), pltpu.VMEM((n_slabs, tm, LANES), F32),
                        pltpu.VMEM((n_slabs, tm, LANES), F32)],
        compiler_params=_params(2),
        name="inproj_prompt",
    )(x, w_in, w_conv_out, norm_mix, conv_w, conv_b, sample_qkv, *caches_t, *tcs, tn)


def _inproj_sample(x, w_in, w_conv_out, norm_mix, conv_w, conv_b, p0, p1, tm):
    n = x.shape[0]
    tok = lambda w: pl.BlockSpec((tm, w), lambda i: (i, 0))
    out_shape = (
        jax.ShapeDtypeStruct((n, N_GROUPS * QKV_W), F32),
        jax.ShapeDtypeStruct((n, D_MODEL), BF16),
        jax.ShapeDtypeStruct((n, D_MODEL), BF16),
        jax.ShapeDtypeStruct((n, CONV_W), F32),
    )
    return pl.pallas_call(
        functools.partial(_inproj_body, sample=True),
        grid=(n // tm,),
        in_specs=[tok(D_MODEL), _resident(w_in.shape), _resident(w_conv_out.shape),
                  _resident(norm_mix.shape), _resident(conv_w.shape), _resident(conv_b.shape),
                  tok(CONV_W), tok(CONV_W)],
        out_specs=[tok(N_GROUPS * QKV_W), tok(D_MODEL), tok(D_MODEL), tok(CONV_W)],
        out_shape=out_shape,
        compiler_params=_params(1),
        name="inproj_sample",
    )(x, w_in, w_conv_out, norm_mix, conv_w, conv_b, p0, p1)


def _attn_prompt_body(q_ref, kc_ref, kp_ref, vc_ref, vp_ref, tab_ref, o_ref, l_ref):
    i = pl.program_id(2)
    first_tile = jnp.where(i == 0, 0, 1)
    head = _head_of_lane((BLOCK, GROUP_W))
    low_head = lax.broadcasted_iota(jnp.int32, (BLOCK, LANES), 1) < HEAD_DIM
    ones = jnp.ones((2 * BLOCK, LANES), BF16)
    n_cls, tq, _ = q_ref.shape
    for c, j in [(c, j) for c in range(n_cls) for j in range(tq // BLOCK)]:
        q = q_ref[c, j * BLOCK:(j + 1) * BLOCK, :]
        qm = jnp.concatenate(
            [jnp.where(head == h, q, jnp.zeros_like(q)) for h in range(HEADS_PER_GROUP)], axis=0)
        if j == 0:
            k = jnp.concatenate([kp_ref[c], kc_ref[c, 0:BLOCK, :]], axis=0)
            v = jnp.concatenate([vp_ref[c], vc_ref[c, 0:BLOCK, :]], axis=0)
            tab = tab_ref[first_tile]
        else:
            k = kc_ref[c, (j - 1) * BLOCK:(j + 1) * BLOCK, :]
            v = vc_ref[c, (j - 1) * BLOCK:(j + 1) * BLOCK, :]
            tab = tab_ref[1]
        s = lax.dot_general(qm, k, _NT, preferred_element_type=F32) + tab
        m = jnp.max(s, axis=-1, keepdims=True)
        p = jnp.exp2(s - m).astype(BF16)
        halves_o, halves_l = [], []
        for pair in range(HEADS_PER_GROUP // 2):
            w = jnp.concatenate([v[:, pair * LANES:(pair + 1) * LANES], ones], axis=1)
            r = jnp.dot(p[2 * pair * BLOCK:2 * (pair + 1) * BLOCK, :], w,
                        preferred_element_type=F32)
            m0 = jnp.broadcast_to(m[2 * pair * BLOCK:(2 * pair + 1) * BLOCK], (BLOCK, LANES))
            m1 = jnp.broadcast_to(m[(2 * pair + 1) * BLOCK:2 * (pair + 1) * BLOCK], (BLOCK, LANES))
            acc = jnp.where(low_head, r[:BLOCK, :LANES], r[BLOCK:, :LANES])
            l = jnp.where(low_head, r[:BLOCK, LANES:], r[BLOCK:, LANES:])
            halves_o.append(acc / l)
            halves_l.append(jnp.where(low_head, m0, m1) + jnp.log2(l))
        o_ref[c, j * BLOCK:(j + 1) * BLOCK, :] = (
            jnp.concatenate(halves_o, axis=1).astype(o_ref.dtype))
        l_ref[c, j * BLOCK:(j + 1) * BLOCK, :] = jnp.concatenate(halves_l, axis=1)


def _attn_prompt(qkv, tab, rows_per_step, name):
    b, dil, L, _ = qkv.shape
    tq = min(rows_per_step, L)
    n_cls = min(rows_per_step // tq, dil)
    cur_spec = lambda which: pl.BlockSpec(
        (None, n_cls, tq, GROUP_W), lambda bi, r, i: (bi, r, i, which))
    prev_spec = lambda which: pl.BlockSpec(
        (None, n_cls, BLOCK, GROUP_W),
        lambda bi, r, i: (bi, r, jnp.maximum(i * (tq // BLOCK) - 1, 0), which))
    out_spec = pl.BlockSpec((None, n_cls, tq, GROUP_W), lambda bi, r, i: (bi, r, i, 0))
    return pl.pallas_call(
        _attn_prompt_body,
        grid=(b, dil // n_cls, L // tq),
        in_specs=[cur_spec(0), cur_spec(1), prev_spec(1), cur_spec(2), prev_spec(2),
                  _resident(tab.shape)],
        out_specs=[out_spec, out_spec],
        out_shape=(jax.ShapeDtypeStruct((b, dil, L, GROUP_W), BF16),
                   jax.ShapeDtypeStruct((b, dil, L, GROUP_W), F32)),
        compiler_params=_params(3),
        name=name,
    )(qkv, qkv, qkv, qkv, qkv, tab)


DEC_T = 8
NEW_PAD = 16


def _sample_scores(qkv_ref, cache_refs, tc_refs, tn_ref):
    pad = jnp.zeros((NEW_PAD - DEC_T, GROUP_W), F32)
    scores = []
    for i in range(cache_refs[0].shape[0]):
        rows = slice(i * DEC_T, (i + 1) * DEC_T)
        for g in range(N_GROUPS):
            base = g * QKV_W
            q = qkv_ref[rows, base:base + GROUP_W]
            kn = jnp.concatenate([qkv_ref[rows, base + GROUP_W:base + 2 * GROUP_W], pad], axis=0)
            qm = _per_head_rows(q).astype(BF16)
            s_n = (lax.dot_general(qm, kn.astype(BF16), _NT, preferred_element_type=F32)
                   + tn_ref[g])
            s_c = (jnp.dot(qm, cache_refs[g][i, 0].astype(BF16), preferred_element_type=F32)
                   + tc_refs[g][...])
            scores.append((s_c, s_n))
    return scores


def _sample_outputs(scores, qkv_ref, cache_refs, o_refs, l_refs):
    pad = jnp.zeros((NEW_PAD - DEC_T, GROUP_W), F32)
    scores = iter(scores)
    for i in range(cache_refs[0].shape[0]):
        rows = slice(i * DEC_T, (i + 1) * DEC_T)
        for g in range(N_GROUPS):
            s_c, s_n = next(scores)
            base = g * QKV_W
            vn = jnp.concatenate([qkv_ref[rows, base + 2 * GROUP_W:base + QKV_W], pad], axis=0)
            m = jnp.maximum(jnp.max(s_c, axis=-1, keepdims=True),
                            jnp.max(s_n, axis=-1, keepdims=True))
            p_c = jnp.exp2(s_c - m)
            p_n = jnp.exp2(s_n - m)
            l = jnp.sum(p_c, axis=-1, keepdims=True) + jnp.sum(p_n, axis=-1, keepdims=True)
            acc = jnp.dot(p_n.astype(BF16), vn.astype(BF16), preferred_element_type=F32)
            acc = acc + lax.dot_general(p_c.astype(BF16), cache_refs[g][i, 1].astype(BF16), _NT,
                                        preferred_element_type=F32)
            res = acc / l
            lse = jnp.broadcast_to(m + jnp.log2(l), res.shape)
            o_refs[g][rows, :] = _head_diagonal(res, DEC_T)
            l_refs[g][rows, :] = _head_diagonal(lse, DEC_T)


FF_CHUNK = 1024


def _natural_rows(ref, natural_ref, stage_ref):
    dil, rows, _ = ref.shape
    if dil == 1:
        return ref[0].astype(F32)
    t = dil * rows
    halves = []
    for half in range(GROUP_W // LANES):
        piece = lambda r: ref[r, :, half * LANES:(half + 1) * LANES].astype(F32)
        if dil <= MAX_SUBLANE_STRIDE:
            for r in range(dil):
                natural_ref[half, pl.ds(r, rows, stride=dil), :] = piece(r)
        else:
            s1, s2 = MAX_SUBLANE_STRIDE, dil // MAX_SUBLANE_STRIDE
            assert s1 * s2 == dil and s2 <= MAX_SUBLANE_STRIDE
            for r1 in range(s1):
                for k in range(s2):
                    stage_ref[half, pl.ds(r1 * (t // s1) + k, rows, stride=s2), :] = (
                        piece(s1 * k + r1))
            for r1 in range(s1):
                natural_ref[half, pl.ds(r1, t // s1, stride=s1), :] = (
                    stage_ref[half, pl.ds(r1 * (t // s1), t // s1), :])
        halves.append(natural_ref[half])
    return jnp.concatenate(halves, axis=1)


def _merge_groups(os_, ls_):
    (o0, o1, o2), (l0, l1, l2) = os_, ls_
    m = jnp.maximum(l0, jnp.maximum(l1, l2))
    e0, e1, e2 = jnp.exp2(l0 - m), jnp.exp2(l1 - m), jnp.exp2(l2 - m)
    return ((e0 * o0 + e1 * o1 + e2 * o2) / (e0 + e1 + e2)).astype(BF16)


def _outproj_body(x_ref, o0_ref, o1_ref, o2_ref, l0_ref, l1_ref, l2_ref, sga_ref, mb_ref,
                  wao_ref, wo_ref, w1_ref, w2_ref, nmlp_ref, nfin_ref, y_ref, *scratch):
    if scratch:
        natural_ref, stage_ref = scratch
        os_ = [_natural_rows(ref, natural_ref.at[k], stage_ref.at[k]) for k, ref in
               enumerate((o0_ref, o1_ref, o2_ref))]
        ls_ = [_natural_rows(ref, natural_ref.at[N_GROUPS + k], stage_ref.at[N_GROUPS + k])
               for k, ref in enumerate((l0_ref, l1_ref, l2_ref))]
    else:
        os_ = [ref[...] for ref in (o0_ref, o1_ref, o2_ref)]
        ls_ = [ref[...] for ref in (l0_ref, l1_ref, l2_ref)]
    branch_a = jnp.dot(_merge_groups(os_, ls_), wao_ref[...], preferred_element_type=F32)
    mix = sga_ref[...].astype(F32) * branch_a + mb_ref[...].astype(F32)
    x1 = x_ref[...] + jnp.dot(mix.astype(BF16), wo_ref[...], preferred_element_type=F32)
    h = _rmsnorm(x1, nmlp_ref[...]).astype(BF16)
    acc = x1
    for c in range(D_FF // FF_CHUNK):
        a = jnp.dot(h, w1_ref[:, c * FF_CHUNK:(c + 1) * FF_CHUNK], preferred_element_type=F32)
        a = jnp.square(jnp.maximum(a, 0.0)).astype(BF16)
        acc = acc + jnp.dot(a, w2_ref[c * FF_CHUNK:(c + 1) * FF_CHUNK, :],
                            preferred_element_type=F32)
    y_ref[...] = _rmsnorm(acc, nfin_ref[...])


def _outproj_prompt(x, os_, ls_, sga, mb, weights, tm):
    b, s, _ = x.shape
    tok = pl.BlockSpec((None, tm, D_MODEL), lambda bi, i: (bi, i, 0))
    grouped = [pl.BlockSpec((None, dil, tm // dil, GROUP_W), lambda bi, i: (bi, 0, i, 0))
               for _, dil in GROUPS]
    return pl.pallas_call(
        _outproj_body,
        grid=(b, s // tm),
        in_specs=[tok] + grouped + grouped + [tok, tok] + [_resident(w.shape) for w in weights],
        out_specs=tok,
        out_shape=jax.ShapeDtypeStruct((b, s, D_MODEL), F32),
        scratch_shapes=[pltpu.VMEM((2 * N_GROUPS, GROUP_W // LANES, tm, LANES), F32)] * 2,
        compiler_params=_params(2),
        name="outproj_ffn_prompt",
    )(x, *os_, *ls_, sga, mb, *weights)


def _outproj_sample(x, os_, ls_, sga, mb, weights, tm):
    n = x.shape[0]
    tok = lambda w: pl.BlockSpec((tm, w), lambda i: (i, 0))
    return pl.pallas_call(
        _outproj_body,
        grid=(n // tm,),
        in_specs=[tok(D_MODEL)] + [tok(GROUP_W)] * 6 + [tok(D_MODEL), tok(D_MODEL)]
        + [_resident(w.shape) for w in weights],
        out_specs=tok(D_MODEL),
        out_shape=jax.ShapeDtypeStruct((n, D_MODEL), F32),
        compiler_params=_params(1),
        name="outproj_ffn_sample",
    )(x, *os_, *ls_, sga, mb, *weights)


def _t5_bucket(dist):
    n = np.asarray(dist)
    max_exact = N_BUCKETS // 2
    large = max_exact + (np.log(np.maximum(n, 1) / max_exact) / np.log(MAX_DISTANCE / max_exact)
                         * (N_BUCKETS - max_exact)).astype(np.int32)
    large = np.minimum(large, N_BUCKETS - 1)
    return np.where(n < max_exact, n, large).astype(np.int32)


def _group_bias(rel_bias, g):
    dil = GROUPS[g][1]
    buckets = _t5_bucket(np.arange(KEYS_PER_QUERY + 1) * dil)
    bias = rel_bias[buckets][:, g * HEADS_PER_GROUP:(g + 1) * HEADS_PER_GROUP].astype(F32)
    return bias * LOG2E


def _toeplitz(vec, n_rows, n_cols, offset):
    h, n = vec.shape
    start = n - 1 - offset
    assert start - (n_rows - 1) >= 0 and start + n_cols <= n
    w = max(n, start + n_cols + 1)
    rev = jnp.pad(vec[:, ::-1], ((0, 0), (0, w - n)))
    skew = jnp.tile(rev, (1, n_rows))[:, :n_rows * (w - 1)].reshape(h, n_rows, w - 1)
    return skew[:, :, start:start + n_cols].reshape(h * n_rows, n_cols)


def _pad_neg(vec, before, after):
    h = vec.shape[0]
    return jnp.concatenate([jnp.full((h, before), NEG, F32), vec, jnp.full((h, after), NEG, F32)],
                           axis=1)


def _prompt_table(bias):
    by_stride = _pad_neg(bias.T, BLOCK - 1, BLOCK - 1)
    tab = _toeplitz(by_stride, BLOCK, 2 * BLOCK, 2 * BLOCK - 1)
    cur = (np.arange(2 * BLOCK) >= BLOCK)[None, :]
    return jnp.stack([jnp.where(cur, tab, NEG), tab])


def _sample_tables(bias, win, dil):
    h = bias.shape[1]
    spread = jnp.concatenate([bias.T[:, :, None], jnp.full((h, KEYS_PER_QUERY + 1, dil - 1), NEG)],
                             axis=2).reshape(h, -1)[:, :KEYS_PER_QUERY * dil + 1]
    tc = _toeplitz(_pad_neg(spread, 0, DEC_T - 1), DEC_T, win, win)
    tn = _toeplitz(_pad_neg(spread, NEW_PAD - 1, 0), DEC_T, NEW_PAD, NEW_PAD - 1)
    return tc, tn


def _cache_transposed(cache):
    bd, wb = cache.shape[:2]
    return jnp.transpose(cache, (0, 2, 3, 4, 1)).reshape(bd, 2, GROUP_W, wb)


def _cache_rows(kvt):
    b, _, _, win = kvt.shape
    kvt = kvt.reshape(b, 2, HEADS_PER_GROUP, HEAD_DIM, win)
    return jnp.transpose(kvt, (0, 4, 1, 2, 3))[None]


TM_INPROJ = 512
TM_OUTPROJ = 512
TQ_ATTN = 4096


def kernel(x_prompt, x_sample, cache_kv_g0, cache_kv_g1, cache_kv_g2, state_conv, w_in, w_att_out,
           w_conv_out, w_o, conv_w, conv_b, rel_bias, norm_mix, norm_mlp, w_ff1, w_ff2, norm_final):
    assert w_in.shape[0] == 1, "one layer"
    b, s, _ = x_prompt.shape
    bd, t, _ = x_sample.shape
    assert t == DEC_T
    wp = w_in[0].astype(BF16)
    wco = w_conv_out[0].astype(BF16)
    wao, wo = w_att_out[0].astype(BF16), w_o[0].astype(BF16)
    w1, w2 = w_ff1[0].astype(BF16), w_ff2[0].astype(BF16)
    nmix, nmlp, nfin = norm_mix[0][None], norm_mlp[0][None], norm_final[None]
    cw, cb = conv_w[0], conv_b[0][None]
    biases = [_group_bias(rel_bias, g) for g in range(N_GROUPS)]

    weights = (wao, wo, w1, w2, nmlp, nfin)

    n = bd * t
    st = state_conv[0]
    p0 = jnp.repeat(st[:, 0], t, axis=0)
    p1 = jnp.repeat(st[:, 1], t, axis=0)
    qkv_s, sga_s, mb_s, u_s = _inproj_sample(
        x_sample.reshape(n, D_MODEL), wp, wco, nmix, cw, cb, p0, p1, min(TM_INPROJ, n))
    tabs = [_sample_tables(biases[g], *GROUPS[g]) for g in range(N_GROUPS)]
    caches_t = [_cache_transposed(c[0]) for c in (cache_kv_g0, cache_kv_g1, cache_kv_g2)]

    (qkv0, qkv1, qkv2, kvt0, kvt1, kvt2, sga, mb, utail, *sample_att) = _inproj_prompt(
        x_prompt, wp, wco, nmix, cw, cb, qkv_s, caches_t, [tb[0] for tb in tabs],
        jnp.stack([tb[1] for tb in tabs]), TM_INPROJ)
    os_s, ls_s = sample_att[:N_GROUPS], sample_att[N_GROUPS:]
    os_, ls_ = [], []
    for g, qkv in enumerate((qkv0, qkv1, qkv2)):
        o, lse = _attn_prompt(qkv, _prompt_table(biases[g]), TQ_ATTN, f"attn_prompt_g{g}")
        os_.append(o)
        ls_.append(lse)
    y_prompt = _outproj_prompt(x_prompt, os_, ls_, sga, mb, weights, TM_OUTPROJ)
    kv_prompt = [_cache_rows(kvt) for kvt in (kvt0, kvt1, kvt2)]
    conv_prompt = utail[:, SUBLANES - 2:][None]

    y_sample = _outproj_sample(x_sample.reshape(n, D_MODEL), os_s, ls_s, sga_s, mb_s, weights,
                               min(TM_OUTPROJ, n))
    kv_sample = [qkv_s[:, g * QKV_W + GROUP_W:(g + 1) * QKV_W].reshape(
        1, bd, t, 2, HEADS_PER_GROUP, HEAD_DIM) for g in range(N_GROUPS)]
    conv_sample = u_s.reshape(bd, t, CONV_W)[:, t - 2:][None]

    return (y_prompt, y_sample.reshape(bd, t, D_MODEL),
            kv_prompt[0], kv_prompt[1], kv_prompt[2], conv_prompt,
            kv_sample[0], kv_sample[1], kv_sample[2], conv_sample)
```

```python
import functools

import jax
import jax.numpy as jnp
import numpy as np
from jax import lax
from jax.experimental import pallas as pl
from jax.experimental.pallas import tpu as pltpu

D_MODEL = 1024
HEAD_DIM = 64
HEADS_PER_GROUP = 4
GROUPS = ((128, 1), (512, 4), (2048, 16))
N_GROUPS = 3
GROUP_W = HEADS_PER_GROUP * HEAD_DIM
QKV_W = 3 * GROUP_W
ATT_W = N_GROUPS * GROUP_W
CONV_W = D_MODEL // 2
D_FF = 4 * D_MODEL
N_BUCKETS = 32
MAX_DISTANCE = 2048
KEYS_PER_QUERY = 128
BLOCK = 128
LANES = 128
SUBLANES = 8
EPS = 1e-6
SCALE = HEAD_DIM ** -0.5
LOG2E = float(np.log2(np.e))
NEG = -1e30

OFF_CONV = 3 * ATT_W
OFF_GA = OFF_CONV + 3 * CONV_W
OFF_GB = OFF_GA + D_MODEL
PROJ_W = OFF_GB + D_MODEL

VMEM_LIMIT_V7X = 56 * 1024 * 1024
F32 = jnp.float32
BF16 = jnp.bfloat16
_NT = (((1,), (1,)), ((), ()))


def _params(n_axes):
    return pltpu.CompilerParams(
        dimension_semantics=("arbitrary",) * n_axes, vmem_limit_bytes=VMEM_LIMIT_V7X)


def _resident(shape):
    return pl.BlockSpec(shape, lambda *_: (0,) * len(shape), pipeline_mode=pl.Buffered(1))


def _rmsnorm(x, g):
    y = x * lax.rsqrt(jnp.mean(x * x, axis=-1, keepdims=True) + EPS)
    return y * g


def _head_of_lane(shape):
    return lax.broadcasted_iota(jnp.int32, shape, len(shape) - 1) // HEAD_DIM


def _per_head_rows(x):
    head = _head_of_lane(x.shape)
    return jnp.concatenate([jnp.where(head == h, x, 0.0) for h in range(HEADS_PER_GROUP)], axis=0)


def _head_diagonal(x, t):
    head = _head_of_lane((t, GROUP_W))
    out = x[0:t, :]
    for h in range(1, HEADS_PER_GROUP):
        out = jnp.where(head == h, x[h * t:(h + 1) * t, :], out)
    return out


MAX_SUBLANE_STRIDE = 4


def _rows_by_residue(slab_ref, stage_ref, dil):
    t = slab_ref.shape[0]
    if dil <= MAX_SUBLANE_STRIDE:
        return [(r, slab_ref[pl.ds(r, t // dil, stride=dil), :]) for r in range(dil)]
    s1, s2 = MAX_SUBLANE_STRIDE, dil // MAX_SUBLANE_STRIDE
    assert s1 * s2 == dil and s2 <= MAX_SUBLANE_STRIDE
    for r1 in range(s1):
        stage_ref[pl.ds(r1 * (t // s1), t // s1), :] = slab_ref[pl.ds(r1, t // s1, stride=s1), :]
    out = {}
    for r1 in range(s1):
        for k in range(s2):
            out[s1 * k + r1] = stage_ref[pl.ds(r1 * (t // s1) + k, t // dil, stride=s2), :]
    return sorted(out.items())


def _inproj_body(*refs, sample):
    if sample:
        (x_ref, win_ref, wco_ref, nm_ref, cw_ref, cb_ref, p0_ref, p1_ref,
         qkv_ref, sga_ref, mb_ref, u_ref) = refs
    else:
        (x_ref, win_ref, wco_ref, nm_ref, cw_ref, cb_ref,
         sqkv_ref, c0_ref, c1_ref, c2_ref, tc0_ref, tc1_ref, tc2_ref, tn_ref,
         qkv0_ref, qkv1_ref, qkv2_ref, kvt0_ref, kvt1_ref, kvt2_ref, sga_ref, mb_ref, u_ref,
         so0_ref, so1_ref, so2_ref, sl0_ref, sl1_ref, sl2_ref,
         carry_ref, perm_ref, stage_ref) = refs
        qkv_refs = (qkv0_ref, qkv1_ref, qkv2_ref)
        kvt_refs = (kvt0_ref, kvt1_ref, kvt2_ref)
        cache_refs = (c0_ref, c1_ref, c2_ref)
    x = x_ref[...]
    tm = x.shape[0]
    xn = _rmsnorm(x, nm_ref[...]).astype(BF16)

    def proj(lo, hi):
        return jnp.dot(xn, win_ref[:, lo:hi], preferred_element_type=F32)

    pc = proj(OFF_CONV, OFF_GA)
    ga = proj(OFF_GA, OFF_GB)
    sga_ref[...] = jax.nn.sigmoid(ga).astype(sga_ref.dtype)
    sgb = jax.nn.sigmoid(proj(OFF_GB, PROJ_W))
    if not sample:
        sample_scores = _sample_scores(sqkv_ref, cache_refs, (tc0_ref, tc1_ref, tc2_ref), tn_ref)
    dilated = {which: proj(which * ATT_W + GROUP_W, (which + 1) * ATT_W) for which in (0, 1)}
    cb, cc, ch = pc[:, :CONV_W], pc[:, CONV_W:2 * CONV_W], pc[:, 2 * CONV_W:]
    u = cc * ch
    row = lax.broadcasted_iota(jnp.int32, (tm, 1), 0)
    if sample:
        p0, p1 = p0_ref[...], p1_ref[...]
        rs = jnp.bitwise_and(row, DEC_T - 1)
        u_ref[...] = u
    else:
        seq_start = pl.program_id(1) == 0
        p0 = jnp.where(seq_start, 0.0, carry_ref[SUBLANES - 2:SUBLANES - 1, :])
        p1 = jnp.where(seq_start, 0.0, carry_ref[SUBLANES - 1:SUBLANES, :])
        rs = row
        u_ref[...] = u[tm - SUBLANES:, :]
    um1 = jnp.where(rs == 0, p1, pltpu.roll(u, 1, 0))
    um2 = jnp.where(rs == 0, p0, jnp.where(rs == 1, p1, pltpu.roll(u, 2, 0)))
    if not sample:
        carry_ref[...] = u[tm - SUBLANES:, :]
    z = cb_ref[...] + cw_ref[0:1, :] * um2 + cw_ref[1:2, :] * um1 + cw_ref[2:3, :] * u
    branch_b = jnp.dot((cb * z).astype(BF16), wco_ref[...], preferred_element_type=F32)
    mb_ref[...] = (sgb * branch_b).astype(mb_ref.dtype)
    dilated[2] = proj(2 * ATT_W + GROUP_W, 3 * ATT_W)
    if not sample:
        _sample_outputs(sample_scores, sqkv_ref, cache_refs, (so0_ref, so1_ref, so2_ref),
                        (sl0_ref, sl1_ref, sl2_ref))

    slab = 0
    for g, (_, dil) in reversed(list(enumerate(GROUPS))):
        for which in (1, 2, 0):
            if dil == 1:
                p = proj(which * ATT_W, which * ATT_W + GROUP_W)
            else:
                p = dilated[which][:, (g - 1) * GROUP_W:g * GROUP_W]
            if which == 0:
                p = p * (SCALE * LOG2E)
            elif not sample:
                keep = kvt_refs[g].shape[-1]
                kvt_refs[g][which - 1] = p[tm - keep:, :].T
            dst = which * GROUP_W
            if sample:
                qkv_ref[:, g * QKV_W + dst:g * QKV_W + dst + GROUP_W] = p
            elif dil == 1:
                qkv_refs[g][0, :, dst:dst + GROUP_W] = p.astype(BF16)
            else:
                words = pltpu.bitcast(p.astype(BF16), jnp.uint32)
                for half in range(GROUP_W // LANES):
                    perm_ref[slab] = words[:, half * LANES:(half + 1) * LANES]
                    lo = dst + half * LANES
                    for j, w in _rows_by_residue(perm_ref.at[slab], stage_ref.at[slab], dil // 2):
                        even = pltpu.bitcast(w << 16, F32)
                        odd = pltpu.bitcast(w & jnp.uint32(0xFFFF0000), F32)
                        qkv_refs[g][2 * j, :, lo:lo + LANES] = even.astype(BF16)
                        qkv_refs[g][2 * j + 1, :, lo:lo + LANES] = odd.astype(BF16)
                    slab += 1


def _inproj_prompt(x, w_in, w_conv_out, norm_mix, conv_w, conv_b, sample_qkv, caches_t, tcs, tn, tm):
    b, s, _ = x.shape
    nt = s // tm
    n_sample = sample_qkv.shape[0]
    seqs = n_sample // DEC_T // (b * nt)
    assert seqs >= 1 and seqs * b * nt * DEC_T == n_sample
    seq_block = lambda bi, i: bi * nt + i
    sample_in = (
        [pl.BlockSpec((seqs * DEC_T, N_GROUPS * QKV_W), lambda bi, i: (seq_block(bi, i), 0))]
        + [pl.BlockSpec((seqs,) + c.shape[1:], lambda bi, i: (seq_block(bi, i), 0, 0, 0))
           for c in caches_t]
        + [_resident(tc.shape) for tc in tcs] + [_resident(tn.shape)])
    sample_out_spec = pl.BlockSpec((seqs * DEC_T, GROUP_W), lambda bi, i: (seq_block(bi, i), 0))
    sample_out = jax.ShapeDtypeStruct((n_sample, GROUP_W), F32)
    tok = lambda w: pl.BlockSpec((None, tm, w), lambda bi, i: (bi, i, 0))
    qkv_specs = [pl.BlockSpec((None, dil, tm // dil, QKV_W), lambda bi, i: (bi, 0, i, 0))
                 for _, dil in GROUPS]
    n_slabs = sum(3 * GROUP_W // LANES for _, dil in GROUPS if dil > 1)

    def kvt_spec(win):
        keep = min(win, tm)
        first = (s - win) // tm
        return pl.BlockSpec((None, 2, GROUP_W, keep),
                            lambda bi, i: (bi, 0, 0, jnp.maximum(i - first, 0)))

    out_shape = (
        [jax.ShapeDtypeStruct((b, dil, s // dil, QKV_W), BF16) for _, dil in GROUPS]
        + [jax.ShapeDtypeStruct((b, 2, GROUP_W, win), F32) for win, _ in GROUPS]
        + [jax.ShapeDtypeStruct((b, s, D_MODEL), BF16),
           jax.ShapeDtypeStruct((b, s, D_MODEL), BF16),
           jax.ShapeDtypeStruct((b, SUBLANES, CONV_W), F32)]
        + [sample_out] * (2 * N_GROUPS))
    return pl.pallas_call(
        functools.partial(_inproj_body, sample=False),
        grid=(b, nt),
        in_specs=[tok(D_MODEL), _resident(w_in.shape), _resident(w_conv_out.shape),
                  _resident(norm_mix.shape), _resident(conv_w.shape), _resident(conv_b.shape)]
        + sample_in,
        out_specs=qkv_specs + [kvt_spec(win) for win, _ in GROUPS]
        + [tok(D_MODEL), tok(D_MODEL),
           pl.BlockSpec((None, SUBLANES, CONV_W), lambda bi, i: (bi, 0, 0))]
        + [sample_out_spec] * (2 * N_GROUPS),
        out_shape=out_shape,
        scratch_shapes=[pltpu.VMEM((SUBLANES, CONV_W), F32),
                        pltpu.VMEM((n_slabs, tm // 2, LANES), jnp.uint32),
                        pltpu.VMEM((n_slabs, tm // 2, LANES), jnp.uint32)],
        compiler_params=_params(2),
        name="inproj_prompt",
    )(x, w_in, w_conv_out, norm_mix, conv_w, conv_b, sample_qkv, *caches_t, *tcs, tn)


def _inproj_sample(x, w_in, w_conv_out, norm_mix, conv_w, conv_b, p0, p1, tm):
    n = x.shape[0]
    tok = lambda w: pl.BlockSpec((tm, w), lambda i: (i, 0))
    out_shape = (
        jax.ShapeDtypeStruct((n, N_GROUPS * QKV_W), F32),
        jax.ShapeDtypeStruct((n, D_MODEL), BF16),
        jax.ShapeDtypeStruct((n, D_MODEL), BF16),
        jax.ShapeDtypeStruct((n, CONV_W), F32),
    )
    return pl.pallas_call(
        functools.partial(_inproj_body, sample=True),
        grid=(n // tm,),
        in_specs=[tok(D_MODEL), _resident(w_in.shape), _resident(w_conv_out.shape),
                  _resident(norm_mix.shape), _resident(conv_w.shape), _resident(conv_b.shape),
                  tok(CONV_W), tok(CONV_W)],
        out_specs=[tok(N_GROUPS * QKV_W), tok(D_MODEL), tok(D_MODEL), tok(CONV_W)],
        out_shape=out_shape,
        compiler_params=_params(1),
        name="inproj_sample",
    )(x, w_in, w_conv_out, norm_mix, conv_w, conv_b, p0, p1)


def _attn_prompt_body(q_ref, kc_ref, kp_ref, vc_ref, vp_ref, tab_ref, o_ref, l_ref):
    i = pl.program_id(2)
    first_tile = jnp.where(i == 0, 0, 1)
    head = _head_of_lane((BLOCK, GROUP_W))
    low_head = lax.broadcasted_iota(jnp.int32, (BLOCK, LANES), 1) < HEAD_DIM
    ones = jnp.ones((2 * BLOCK, LANES), BF16)
    n_cls, tq, _ = q_ref.shape
    for c, j in [(c, j) for c in range(n_cls) for j in range(tq // BLOCK)]:
        q = q_ref[c, j * BLOCK:(j + 1) * BLOCK, :]
        qm = jnp.concatenate(
            [jnp.where(head == h, q, jnp.zeros_like(q)) for h in range(HEADS_PER_GROUP)], axis=0)
        if j == 0:
            k = jnp.concatenate([kp_ref[c], kc_ref[c, 0:BLOCK, :]], axis=0)
            v = jnp.concatenate([vp_ref[c], vc_ref[c, 0:BLOCK, :]], axis=0)
            tab = tab_ref[first_tile]
        else:
            k = kc_ref[c, (j - 1) * BLOCK:(j + 1) * BLOCK, :]
            v = vc_ref[c, (j - 1) * BLOCK:(j + 1) * BLOCK, :]
            tab = tab_ref[1]
        s = lax.dot_general(qm, k, _NT, preferred_element_type=F32) + tab
        m = jnp.max(s, axis=-1, keepdims=True)
        p = jnp.exp2(s - m).astype(BF16)
        halves_o, halves_l = [], []
        for pair in range(HEADS_PER_GROUP // 2):
            w = jnp.concatenate([v[:, pair * LANES:(pair + 1) * LANES], ones], axis=1)
            r = jnp.dot(p[2 * pair * BLOCK:2 * (pair + 1) * BLOCK, :], w,
                        preferred_element_type=F32)
            m0 = jnp.broadcast_to(m[2 * pair * BLOCK:(2 * pair + 1) * BLOCK], (BLOCK, LANES))
            m1 = jnp.broadcast_to(m[(2 * pair + 1) * BLOCK:2 * (pair + 1) * BLOCK], (BLOCK, LANES))
            acc = jnp.where(low_head, r[:BLOCK, :LANES], r[BLOCK:, :LANES])
            l = jnp.where(low_head, r[:BLOCK, LANES:], r[BLOCK:, LANES:])
            halves_o.append(acc / l)
            halves_l.append(jnp.where(low_head, m0, m1) + jnp.log2(l))
        o_ref[c, j * BLOCK:(j + 1) * BLOCK, :] = (
            jnp.concatenate(halves_o, axis=1).astype(o_ref.dtype))
        l_ref[c, j * BLOCK:(j + 1) * BLOCK, :] = jnp.concatenate(halves_l, axis=1)


def _attn_prompt(qkv, tab, rows_per_step, name):
    b, dil, L, _ = qkv.shape
    tq = min(rows_per_step, L)
    n_cls = min(rows_per_step // tq, dil)
    cur_spec = lambda which: pl.BlockSpec(
        (None, n_cls, tq, GROUP_W), lambda bi, r, i: (bi, r, i, which))
    prev_spec = lambda which: pl.BlockSpec(
        (None, n_cls, BLOCK, GROUP_W),
        lambda bi, r, i: (bi, r, jnp.maximum(i * (tq // BLOCK) - 1, 0), which))
    out_spec = pl.BlockSpec((None, n_cls, tq, GROUP_W), lambda bi, r, i: (bi, r, i, 0))
    return pl.pallas_call(
        _attn_prompt_body,
        grid=(b, dil // n_cls, L // tq),
        in_specs=[cur_spec(0), cur_spec(1), prev_spec(1), cur_spec(2), prev_spec(2),
                  _resident(tab.shape)],
        out_specs=[out_spec, out_spec],
        out_shape=(jax.ShapeDtypeStruct((b, dil, L, GROUP_W), BF16),
                   jax.ShapeDtypeStruct((b, dil, L, GROUP_W), F32)),
        compiler_params=_params(3),
        name=name,
    )(qkv, qkv, qkv, qkv, qkv, tab)


DEC_T = 8
NEW_PAD = 16


def _sample_scores(qkv_ref, cache_refs, tc_refs, tn_ref):
    pad = jnp.zeros((NEW_PAD - DEC_T, GROUP_W), F32)
    scores = []
    for i in range(cache_refs[0].shape[0]):
        rows = slice(i * DEC_T, (i + 1) * DEC_T)
        for g in range(N_GROUPS):
            base = g * QKV_W
            q = qkv_ref[rows, base:base + GROUP_W]
            kn = jnp.concatenate([qkv_ref[rows, base + GROUP_W:base + 2 * GROUP_W], pad], axis=0)
            qm = _per_head_rows(q).astype(BF16)
            s_n = (lax.dot_general(qm, kn.astype(BF16), _NT, preferred_element_type=F32)
                   + tn_ref[g])
            s_c = (jnp.dot(qm, cache_refs[g][i, 0].astype(BF16), preferred_element_type=F32)
                   + tc_refs[g][...])
            scores.append((s_c, s_n))
    return scores


def _sample_outputs(scores, qkv_ref, cache_refs, o_refs, l_refs):
    pad = jnp.zeros((NEW_PAD - DEC_T, GROUP_W), F32)
    scores = iter(scores)
    for i in range(cache_refs[0].shape[0]):
        rows = slice(i * DEC_T, (i + 1) * DEC_T)
        for g in range(N_GROUPS):
            s_c, s_n = next(scores)
            base = g * QKV_W
            vn = jnp.concatenate([qkv_ref[rows, base + 2 * GROUP_W:base + QKV_W], pad], axis=0)
            m = jnp.maximum(jnp.max(s_c, axis=-1, keepdims=True),
                            jnp.max(s_n, axis=-1, keepdims=True))
            p_c = jnp.exp2(s_c - m)
            p_n = jnp.exp2(s_n - m)
            l = jnp.sum(p_c, axis=-1, keepdims=True) + jnp.sum(p_n, axis=-1, keepdims=True)
            acc = jnp.dot(p_n.astype(BF16), vn.astype(BF16), preferred_element_type=F32)
            acc = acc + lax.dot_general(p_c.astype(BF16), cache_refs[g][i, 1].astype(BF16), _NT,
                                        preferred_element_type=F32)
            res = acc / l
            lse = jnp.broadcast_to(m + jnp.log2(l), res.shape)
            o_refs[g][rows, :] = _head_diagonal(res, DEC_T)
            l_refs[g][rows, :] = _head_diagonal(lse, DEC_T)


FF_CHUNK = 1024


def _natural_rows(ref, natural_ref, stage_ref):
    dil, rows, _ = ref.shape
    if dil == 1:
        return ref[0].astype(F32)
    t = dil * rows
    halves = []
    for half in range(GROUP_W // LANES):
        piece = lambda r: ref[r, :, half * LANES:(half + 1) * LANES].astype(F32)
        if dil <= MAX_SUBLANE_STRIDE:
            for r in range(dil):
                natural_ref[half, pl.ds(r, rows, stride=dil), :] = piece(r)
        else:
            s1, s2 = MAX_SUBLANE_STRIDE, dil // MAX_SUBLANE_STRIDE
            assert s1 * s2 == dil and s2 <= MAX_SUBLANE_STRIDE
            for r1 in range(s1):
                for k in range(s2):
                    stage_ref[half, pl.ds(r1 * (t // s1) + k, rows, stride=s2), :] = (
                        piece(s1 * k + r1))
            for r1 in range(s1):
                natural_ref[half, pl.ds(r1, t // s1, stride=s1), :] = (
                    stage_ref[half, pl.ds(r1 * (t // s1), t // s1), :])
        halves.append(natural_ref[half])
    return jnp.concatenate(halves, axis=1)


def _merge_groups(os_, ls_):
    (o0, o1, o2), (l0, l1, l2) = os_, ls_
    m = jnp.maximum(l0, jnp.maximum(l1, l2))
    e0, e1, e2 = jnp.exp2(l0 - m), jnp.exp2(l1 - m), jnp.exp2(l2 - m)
    return ((e0 * o0 + e1 * o1 + e2 * o2) / (e0 + e1 + e2)).astype(BF16)


def _outproj_body(x_ref, o0_ref, o1_ref, o2_ref, l0_ref, l1_ref, l2_ref, sga_ref, mb_ref,
                  wao_ref, wo_ref, w1_ref, w2_ref, nmlp_ref, nfin_ref, y_ref, *scratch):
    if scratch:
        natural_ref, stage_ref = scratch
        os_ = [_natural_rows(ref, natural_ref.at[k], stage_ref.at[k]) for k, ref in
               enumerate((o0_ref, o1_ref, o2_ref))]
        ls_ = [_natural_rows(ref, natural_ref.at[N_GROUPS + k], stage_ref.at[N_GROUPS + k])
               for k, ref in enumerate((l0_ref, l1_ref, l2_ref))]
    else:
        os_ = [ref[...] for ref in (o0_ref, o1_ref, o2_ref)]
        ls_ = [ref[...] for ref in (l0_ref, l1_ref, l2_ref)]
    branch_a = jnp.dot(_merge_groups(os_, ls_), wao_ref[...], preferred_element_type=F32)
    mix = sga_ref[...].astype(F32) * branch_a + mb_ref[...].astype(F32)
    x1 = x_ref[...] + jnp.dot(mix.astype(BF16), wo_ref[...], preferred_element_type=F32)
    h = _rmsnorm(x1, nmlp_ref[...]).astype(BF16)
    acc = x1
    for c in range(D_FF // FF_CHUNK):
        a = jnp.dot(h, w1_ref[:, c * FF_CHUNK:(c + 1) * FF_CHUNK], preferred_element_type=F32)
        a = jnp.square(jnp.maximum(a, 0.0)).astype(BF16)
        acc = acc + jnp.dot(a, w2_ref[c * FF_CHUNK:(c + 1) * FF_CHUNK, :],
                            preferred_element_type=F32)
    y_ref[...] = _rmsnorm(acc, nfin_ref[...])


def _outproj_prompt(x, os_, ls_, sga, mb, weights, tm):
    b, s, _ = x.shape
    tok = pl.BlockSpec((None, tm, D_MODEL), lambda bi, i: (bi, i, 0))
    grouped = [pl.BlockSpec((None, dil, tm // dil, GROUP_W), lambda bi, i: (bi, 0, i, 0))
               for _, dil in GROUPS]
    return pl.pallas_call(
        _outproj_body,
        grid=(b, s // tm),
        in_specs=[tok] + grouped + grouped + [tok, tok] + [_resident(w.shape) for w in weights],
        out_specs=tok,
        out_shape=jax.ShapeDtypeStruct((b, s, D_MODEL), F32),
        scratch_shapes=[pltpu.VMEM((2 * N_GROUPS, GROUP_W // LANES, tm, LANES), F32)] * 2,
        compiler_params=_params(2),
        name="outproj_ffn_prompt",
    )(x, *os_, *ls_, sga, mb, *weights)


def _outproj_sample(x, os_, ls_, sga, mb, weights, tm):
    n = x.shape[0]
    tok = lambda w: pl.BlockSpec((tm, w), lambda i: (i, 0))
    return pl.pallas_call(
        _outproj_body,
        grid=(n // tm,),
        in_specs=[tok(D_MODEL)] + [tok(GROUP_W)] * 6 + [tok(D_MODEL), tok(D_MODEL)]
        + [_resident(w.shape) for w in weights],
        out_specs=tok(D_MODEL),
        out_shape=jax.ShapeDtypeStruct((n, D_MODEL), F32),
        compiler_params=_params(1),
        name="outproj_ffn_sample",
    )(x, *os_, *ls_, sga, mb, *weights)


def _t5_bucket(dist):
    n = np.asarray(dist)
    max_exact = N_BUCKETS // 2
    large = max_exact + (np.log(np.maximum(n, 1) / max_exact) / np.log(MAX_DISTANCE / max_exact)
                         * (N_BUCKETS - max_exact)).astype(np.int32)
    large = np.minimum(large, N_BUCKETS - 1)
    return np.where(n < max_exact, n, large).astype(np.int32)


def _group_bias(rel_bias, g):
    dil = GROUPS[g][1]
    buckets = _t5_bucket(np.arange(KEYS_PER_QUERY + 1) * dil)
    bias = rel_bias[buckets][:, g * HEADS_PER_GROUP:(g + 1) * HEADS_PER_GROUP].astype(F32)
    return bias * LOG2E


def _toeplitz(vec, n_rows, n_cols, offset):
    h, n = vec.shape
    start = n - 1 - offset
    assert start - (n_rows - 1) >= 0 and start + n_cols <= n
    w = max(n, start + n_cols + 1)
    rev = jnp.pad(vec[:, ::-1], ((0, 0), (0, w - n)))
    skew = jnp.tile(rev, (1, n_rows))[:, :n_rows * (w - 1)].reshape(h, n_rows, w - 1)
    return skew[:, :, start:start + n_cols].reshape(h * n_rows, n_cols)


def _pad_neg(vec, before, after):
    h = vec.shape[0]
    return jnp.concatenate([jnp.full((h, before), NEG, F32), vec, jnp.full((h, after), NEG, F32)],
                           axis=1)


def _prompt_table(bias):
    by_stride = _pad_neg(bias.T, BLOCK - 1, BLOCK - 1)
    tab = _toeplitz(by_stride, BLOCK, 2 * BLOCK, 2 * BLOCK - 1)
    cur = (np.arange(2 * BLOCK) >= BLOCK)[None, :]
    return jnp.stack([jnp.where(cur, tab, NEG), tab])


def _sample_tables(bias, win, dil):
    h = bias.shape[1]
    spread = jnp.concatenate([bias.T[:, :, None], jnp.full((h, KEYS_PER_QUERY + 1, dil - 1), NEG)],
                             axis=2).reshape(h, -1)[:, :KEYS_PER_QUERY * dil + 1]
    tc = _toeplitz(_pad_neg(spread, 0, DEC_T - 1), DEC_T, win, win)
    tn = _toeplitz(_pad_neg(spread, NEW_PAD - 1, 0), DEC_T, NEW_PAD, NEW_PAD - 1)
    return tc, tn


def _cache_transposed(cache):
    bd, wb = cache.shape[:2]
    return jnp.transpose(cache, (0, 2, 3, 4, 1)).reshape(bd, 2, GROUP_W, wb)


def _cache_rows(kvt):
    b, _, _, win = kvt.shape
    kvt = kvt.reshape(b, 2, HEADS_PER_GROUP, HEAD_DIM, win)
    return jnp.transpose(kvt, (0, 4, 1, 2, 3))[None]


TM_INPROJ = 512
TM_OUTPROJ = 512
TQ_ATTN = 4096


def kernel(x_prompt, x_sample, cache_kv_g0, cache_kv_g1, cache_kv_g2, state_conv, w_in, w_att_out,
           w_conv_out, w_o, conv_w, conv_b, rel_bias, norm_mix, norm_mlp, w_ff1, w_ff2, norm_final):
    assert w_in.shape[0] == 1, "one layer"
    b, s, _ = x_prompt.shape
    bd, t, _ = x_sample.shape
    assert t == DEC_T
    wp = w_in[0].astype(BF16)
    wco = w_conv_out[0].astype(BF16)
    wao, wo = w_att_out[0].astype(BF16), w_o[0].astype(BF16)
    w1, w2 = w_ff1[0].astype(BF16), w_ff2[0].astype(BF16)
    nmix, nmlp, nfin = norm_mix[0][None], norm_mlp[0][None], norm_final[None]
    cw, cb = conv_w[0], conv_b[0][None]
    biases = [_group_bias(rel_bias, g) for g in range(N_GROUPS)]

    weights = (wao, wo, w1, w2, nmlp, nfin)

    n = bd * t
    st = state_conv[0]
    p0 = jnp.repeat(st[:, 0], t, axis=0)
    p1 = jnp.repeat(st[:, 1], t, axis=0)
    qkv_s, sga_s, mb_s, u_s = _inproj_sample(
        x_sample.reshape(n, D_MODEL), wp, wco, nmix, cw, cb, p0, p1, min(TM_INPROJ, n))
    tabs = [_sample_tables(biases[g], *GROUPS[g]) for g in range(N_GROUPS)]
    caches_t = [_cache_transposed(c[0]) for c in (cache_kv_g0, cache_kv_g1, cache_kv_g2)]

    (qkv0, qkv1, qkv2, kvt0, kvt1, kvt2, sga, mb, utail, *sample_att) = _inproj_prompt(
        x_prompt, wp, wco, nmix, cw, cb, qkv_s, caches_t, [tb[0] for tb in tabs],
        jnp.stack([tb[1] for tb in tabs]), TM_INPROJ)
    os_s, ls_s = sample_att[:N_GROUPS], sample_att[N_GROUPS:]
    os_, ls_ = [], []
    for g, qkv in enumerate((qkv0, qkv1, qkv2)):
        o, lse = _attn_prompt(qkv, _prompt_table(biases[g]), TQ_ATTN, f"attn_prompt_g{g}")
        os_.append(o)
        ls_.append(lse)
    y_prompt = _outproj_prompt(x_prompt, os_, ls_, sga, mb, weights, TM_OUTPROJ)
    kv_prompt = [_cache_rows(kvt) for kvt in (kvt0, kvt1, kvt2)]
    conv_prompt = utail[:, SUBLANES - 2:][None]

    y_sample = _outproj_sample(x_sample.reshape(n, D_MODEL), os_s, ls_s, sga_s, mb_s, weights,
                               min(TM_OUTPROJ, n))
    kv_sample = [qkv_s[:, g * QKV_W + GROUP_W:(g + 1) * QKV_W].reshape(
        1, bd, t, 2, HEADS_PER_GROUP, HEAD_DIM) for g in range(N_GROUPS)]
    conv_sample = u_s.reshape(bd, t, CONV_W)[:, t - 2:][None]

    return (y_prompt, y_sample.reshape(bd, t, D_MODEL),
            kv_prompt[0], kv_prompt[1], kv_prompt[2], conv_prompt,
            kv_sample[0], kv_sample[1], kv_sample[2], conv_sample)
```

```python
import functools

import jax
import jax.numpy as jnp
import numpy as np
from jax import lax
from jax.experimental import pallas as pl
from jax.experimental.pallas import tpu as pltpu

D_MODEL = 1024
HEAD_DIM = 64
HEADS_PER_GROUP = 4
GROUPS = ((128, 1), (512, 4), (2048, 16))
N_GROUPS = 3
GROUP_W = HEADS_PER_GROUP * HEAD_DIM
QKV_W = 3 * GROUP_W
ATT_W = N_GROUPS * GROUP_W
CONV_W = D_MODEL // 2
D_FF = 4 * D_MODEL
N_BUCKETS = 32
MAX_DISTANCE = 2048
KEYS_PER_QUERY = 128
BLOCK = 128
LANES = 128
SUBLANES = 8
EPS = 1e-6
SCALE = HEAD_DIM ** -0.5
LOG2E = float(np.log2(np.e))
NEG = -1e30

OFF_CONV = 3 * ATT_W
OFF_GA = OFF_CONV + 3 * CONV_W
OFF_GB = OFF_GA + D_MODEL
PROJ_W = OFF_GB + D_MODEL

VMEM_LIMIT_V7X = 56 * 1024 * 1024
F32 = jnp.float32
BF16 = jnp.bfloat16
_NT = (((1,), (1,)), ((), ()))


def _params(n_axes):
    return pltpu.CompilerParams(
        dimension_semantics=("arbitrary",) * n_axes, vmem_limit_bytes=VMEM_LIMIT_V7X)


def _resident(shape):
    return pl.BlockSpec(shape, lambda *_: (0,) * len(shape), pipeline_mode=pl.Buffered(1))


def _rmsnorm(x, g):
    y = x * lax.rsqrt(jnp.mean(x * x, axis=-1, keepdims=True) + EPS)
    return y * g


def _head_of_lane(shape):
    return lax.broadcasted_iota(jnp.int32, shape, len(shape) - 1) // HEAD_DIM


def _per_head_rows(x):
    head = _head_of_lane(x.shape)
    return jnp.concatenate([jnp.where(head == h, x, 0.0) for h in range(HEADS_PER_GROUP)], axis=0)


def _head_diagonal(x, t):
    head = _head_of_lane((t, GROUP_W))
    out = x[0:t, :]
    for h in range(1, HEADS_PER_GROUP):
        out = jnp.where(head == h, x[h * t:(h + 1) * t, :], out)
    return out


MAX_SUBLANE_STRIDE = 4


def _rows_by_residue(slab_ref, stage_ref, dil):
    t = slab_ref.shape[0]
    if dil <= MAX_SUBLANE_STRIDE:
        return [(r, slab_ref[pl.ds(r, t // dil, stride=dil), :]) for r in range(dil)]
    s1, s2 = MAX_SUBLANE_STRIDE, dil // MAX_SUBLANE_STRIDE
    assert s1 * s2 == dil and s2 <= MAX_SUBLANE_STRIDE
    for r1 in range(s1):
        stage_ref[pl.ds(r1 * (t // s1), t // s1), :] = slab_ref[pl.ds(r1, t // s1, stride=s1), :]
    out = {}
    for r1 in range(s1):
        for k in range(s2):
            out[s1 * k + r1] = stage_ref[pl.ds(r1 * (t // s1) + k, t // dil, stride=s2), :]
    return sorted(out.items())


def _inproj_body(*refs, sample):
    if sample:
        (x_ref, win_ref, wco_ref, nm_ref, cw_ref, cb_ref, p0_ref, p1_ref,
         qkv_ref, sga_ref, mb_ref, u_ref) = refs
    else:
        (x_ref, win_ref, wco_ref, nm_ref, cw_ref, cb_ref,
         sqkv_ref, c0_ref, c1_ref, c2_ref, tc0_ref, tc1_ref, tc2_ref, tn_ref,
         qkv0_ref, qkv1_ref, qkv2_ref, kvt0_ref, kvt1_ref, kvt2_ref, sga_ref, mb_ref, u_ref,
         so0_ref, so1_ref, so2_ref, sl0_ref, sl1_ref, sl2_ref,
         carry_ref, perm_ref, stage_ref) = refs
        qkv_refs = (qkv0_ref, qkv1_ref, qkv2_ref)
        kvt_refs = (kvt0_ref, kvt1_ref, kvt2_ref)
        cache_refs = (c0_ref, c1_ref, c2_ref)
    x = x_ref[...]
    tm = x.shape[0]
    xn = _rmsnorm(x, nm_ref[...]).astype(BF16)

    def proj(lo, hi):
        return jnp.dot(xn, win_ref[:, lo:hi], preferred_element_type=F32)

    pc = proj(OFF_CONV, OFF_GA)
    ga = proj(OFF_GA, OFF_GB)
    sga_ref[...] = jax.nn.sigmoid(ga).astype(sga_ref.dtype)
    sgb = jax.nn.sigmoid(proj(OFF_GB, PROJ_W))
    if not sample:
        sample_scores = _sample_scores(sqkv_ref, cache_refs, (tc0_ref, tc1_ref, tc2_ref), tn_ref)
    dilated = {which: proj(which * ATT_W + GROUP_W, (which + 1) * ATT_W) for which in (0, 1)}
    cb, cc, ch = pc[:, :CONV_W], pc[:, CONV_W:2 * CONV_W], pc[:, 2 * CONV_W:]
    u = cc * ch
    row = lax.broadcasted_iota(jnp.int32, (tm, 1), 0)
    if sample:
        p0, p1 = p0_ref[...], p1_ref[...]
        rs = jnp.bitwise_and(row, DEC_T - 1)
        u_ref[...] = u
    else:
        seq_start = pl.program_id(1) == 0
        p0 = jnp.where(seq_start, 0.0, carry_ref[SUBLANES - 2:SUBLANES - 1, :])
        p1 = jnp.where(seq_start, 0.0, carry_ref[SUBLANES - 1:SUBLANES, :])
        rs = row
        u_ref[...] = u[tm - SUBLANES:, :]
    um1 = jnp.where(rs == 0, p1, pltpu.roll(u, 1, 0))
    um2 = jnp.where(rs == 0, p0, jnp.where(rs == 1, p1, pltpu.roll(u, 2, 0)))
    if not sample:
        carry_ref[...] = u[tm - SUBLANES:, :]
    z = cb_ref[...] + cw_ref[0:1, :] * um2 + cw_ref[1:2, :] * um1 + cw_ref[2:3, :] * u
    branch_b = jnp.dot((cb * z).astype(BF16), wco_ref[...], preferred_element_type=F32)
    mb_ref[...] = (sgb * branch_b).astype(mb_ref.dtype)
    dilated[2] = proj(2 * ATT_W + GROUP_W, 3 * ATT_W)
    if not sample:
        _sample_outputs(sample_scores, sqkv_ref, cache_refs, (so0_ref, so1_ref, so2_ref),
                        (sl0_ref, sl1_ref, sl2_ref))

    slab = 0
    for g, (_, dil) in reversed(list(enumerate(GROUPS))):
        for which in (1, 2, 0):
            if dil == 1:
                p = proj(which * ATT_W, which * ATT_W + GROUP_W)
            else:
                p = dilated[which][:, (g - 1) * GROUP_W:g * GROUP_W]
            if which == 0:
                p = p * (SCALE * LOG2E)
            elif not sample:
                keep = kvt_refs[g].shape[-1]
                kvt_refs[g][which - 1] = p[tm - keep:, :].T
            dst = which * GROUP_W
            if sample:
                qkv_ref[:, g * QKV_W + dst:g * QKV_W + dst + GROUP_W] = p
            elif dil == 1:
                qkv_refs[g][0, :, dst:dst + GROUP_W] = p.astype(BF16)
            else:
                words = pltpu.bitcast(p.astype(BF16), jnp.uint32)
                for half in range(GROUP_W // LANES):
                    perm_ref[slab] = words[:, half * LANES:(half + 1) * LANES]
                    lo = dst + half * LANES
                    for j, w in _rows_by_residue(perm_ref.at[slab], stage_ref.at[slab], dil // 2):
                        even = pltpu.bitcast(w << 16, F32)
                        odd = pltpu.bitcast(w & jnp.uint32(0xFFFF0000), F32)
                        qkv_refs[g][2 * j, :, lo:lo + LANES] = even.astype(BF16)
                        qkv_refs[g][2 * j + 1, :, lo:lo + LANES] = odd.astype(BF16)
                    slab += 1


def _inproj_prompt(x, w_in, w_conv_out, norm_mix, conv_w, conv_b, sample_qkv, caches_t, tcs, tn, tm):
    b, s, _ = x.shape
    nt = s // tm
    n_sample = sample_qkv.shape[0]
    seqs = n_sample // DEC_T // (b * nt)
    assert seqs >= 1 and seqs * b * nt * DEC_T == n_sample
    seq_block = lambda bi, i: bi * nt + i
    sample_in = (
        [pl.BlockSpec((seqs * DEC_T, N_GROUPS * QKV_W), lambda bi, i: (seq_block(bi, i), 0))]
        + [pl.BlockSpec((seqs,) + c.shape[1:], lambda bi, i: (seq_block(bi, i), 0, 0, 0))
           for c in caches_t]
        + [_resident(tc.shape) for tc in tcs] + [_resident(tn.shape)])
    sample_out_spec = pl.BlockSpec((seqs * DEC_T, GROUP_W), lambda bi, i: (seq_block(bi, i), 0))
    sample_out = jax.ShapeDtypeStruct((n_sample, GROUP_W), F32)
    tok = lambda w: pl.BlockSpec((None, tm, w), lambda bi, i: (bi, i, 0))
    qkv_specs = [pl.BlockSpec((None, dil, tm // dil, QKV_W), lambda bi, i: (bi, 0, i, 0))
                 for _, dil in GROUPS]
    n_slabs = sum(3 * GROUP_W // LANES for _, dil in GROUPS if dil > 1)

    def kvt_spec(win):
        keep = min(win, tm)
        first = (s - win) // tm
        return pl.BlockSpec((None, 2, GROUP_W, keep),
                            lambda bi, i: (bi, 0, 0, jnp.maximum(i - first, 0)))

    out_shape = (
        [jax.ShapeDtypeStruct((b, dil, s // dil, QKV_W), BF16) for _, dil in GROUPS]
        + [jax.ShapeDtypeStruct((b, 2, GROUP_W, win), F32) for win, _ in GROUPS]
        + [jax.ShapeDtypeStruct((b, s, D_MODEL), BF16),
           jax.ShapeDtypeStruct((b, s, D_MODEL), BF16),
           jax.ShapeDtypeStruct((b, SUBLANES, CONV_W), F32)]
        + [sample_out] * (2 * N_GROUPS))
    return pl.pallas_call(
        functools.partial(_inproj_body, sample=False),
        grid=(b, nt),
        in_specs=[tok(D_MODEL), _resident(w_in.shape), _resident(w_conv_out.shape),
                  _resident(norm_mix.shape), _resident(conv_w.shape), _resident(conv_b.shape)]
        + sample_in,
        out_specs=qkv_specs + [kvt_spec(win) for win, _ in GROUPS]
        + [tok(D_MODEL), tok(D_MODEL),
           pl.BlockSpec((None, SUBLANES, CONV_W), lambda bi, i: (bi, 0, 0))]
        + [sample_out_spec] * (2 * N_GROUPS),
        out_shape=out_shape,
        scratch_shapes=[pltpu.VMEM((SUBLANES, CONV_W), F32),
                        pltpu.VMEM((n_slabs, tm // 2, LANES), jnp.uint32),
                        pltpu.VMEM((n_slabs, tm // 2, LANES), jnp.uint32)],
        compiler_params=_params(2),
        name="inproj_prompt",
    )(x, w_in, w_conv_out, norm_mix, conv_w, conv_b, sample_qkv, *caches_t, *tcs, tn)


def _inproj_sample(x, w_in, w_conv_out, norm_mix, conv_w, conv_b, p0, p1, tm):
    n = x.shape[0]
    tok = lambda w: pl.BlockSpec((tm, w), lambda i: (i, 0))
    out_shape = (
        jax.ShapeDtypeStruct((n, N_GROUPS * QKV_W), F32),
        jax.ShapeDtypeStruct((n, D_MODEL), BF16),
        jax.ShapeDtypeStruct((n, D_MODEL), BF16),
        jax.ShapeDtypeStruct((n, CONV_W), F32),
    )
    return pl.pallas_call(
        functools.partial(_inproj_body, sample=True),
        grid=(n // tm,),
        in_specs=[tok(D_MODEL), _resident(w_in.shape), _resident(w_conv_out.shape),
                  _resident(norm_mix.shape), _resident(conv_w.shape), _resident(conv_b.shape),
                  tok(CONV_W), tok(CONV_W)],
        out_specs=[tok(N_GROUPS * QKV_W), tok(D_MODEL), tok(D_MODEL), tok(CONV_W)],
        out_shape=out_shape,
        compiler_params=_params(1),
        name="inproj_sample",
    )(x, w_in, w_conv_out, norm_mix, conv_w, conv_b, p0, p1)


def _attn_prompt_body(q_ref, kc_ref, kp_ref, vc_ref, vp_ref, tab_ref, o_ref, l_ref):
    i = pl.program_id(2)
    first_tile = jnp.where(i == 0, 0, 1)
    head = _head_of_lane((BLOCK, GROUP_W))
    low_head = lax.broadcasted_iota(jnp.int32, (BLOCK, LANES), 1) < HEAD_DIM
    ones = jnp.ones((2 * BLOCK, LANES), BF16)
    n_cls, tq, _ = q_ref.shape
    for c, j in [(c, j) for c in range(n_cls) for j in range(tq // BLOCK)]:
        q = q_ref[c, j * BLOCK:(j + 1) * BLOCK, :]
        qm = jnp.concatenate(
            [jnp.where(head == h, q, jnp.zeros_like(q)) for h in range(HEADS_PER_GROUP)], axis=0)
        if j == 0:
            k = jnp.concatenate([kp_ref[c], kc_ref[c, 0:BLOCK, :]], axis=0)
            v = jnp.concatenate([vp_ref[c], vc_ref[c, 0:BLOCK, :]], axis=0)
            tab = tab_ref[first_tile]
        else:
            k = kc_ref[c, (j - 1) * BLOCK:(j + 1) * BLOCK, :]
            v = vc_ref[c, (j - 1) * BLOCK:(j + 1) * BLOCK, :]
            tab = tab_ref[1]
        s = lax.dot_general(qm, k, _NT, preferred_element_type=F32) + tab
        m = jnp.max(s, axis=-1, keepdims=True)
        p = jnp.exp2(s - m).astype(BF16)
        halves_o, halves_l = [], []
        for pair in range(HEADS_PER_GROUP // 2):
            w = jnp.concatenate([v[:, pair * LANES:(pair + 1) * LANES], ones], axis=1)
            r = jnp.dot(p[2 * pair * BLOCK:2 * (pair + 1) * BLOCK, :], w,
                        preferred_element_type=F32)
            m0 = jnp.broadcast_to(m[2 * pair * BLOCK:(2 * pair + 1) * BLOCK], (BLOCK, LANES))
            m1 = jnp.broadcast_to(m[(2 * pair + 1) * BLOCK:2 * (pair + 1) * BLOCK], (BLOCK, LANES))
            acc = jnp.where(low_head, r[:BLOCK, :LANES], r[BLOCK:, :LANES])
            l = jnp.where(low_head, r[:BLOCK, LANES:], r[BLOCK:, LANES:])
            halves_o.append(acc / l)
            halves_l.append(jnp.where(low_head, m0, m1) + jnp.log2(l))
        o_ref[c, j * BLOCK:(j + 1) * BLOCK, :] = (
            jnp.concatenate(halves_o, axis=1).astype(o_ref.dtype))
        l_ref[c, j * BLOCK:(j + 1) * BLOCK, :] = jnp.concatenate(halves_l, axis=1)


def _attn_prompt(qkv, tab, rows_per_step, name):
    b, dil, L, _ = qkv.shape
    tq = min(rows_per_step, L)
    n_cls = min(rows_per_step // tq, dil)
    cur_spec = lambda which: pl.BlockSpec(
        (None, n_cls, tq, GROUP_W), lambda bi, r, i: (bi, r, i, which))
    prev_spec = lambda which: pl.BlockSpec(
        (None, n_cls, BLOCK, GROUP_W),
        lambda bi, r, i: (bi, r, jnp.maximum(i * (tq // BLOCK) - 1, 0), which))
    out_spec = pl.BlockSpec((None, n_cls, tq, GROUP_W), lambda bi, r, i: (bi, r, i, 0))
    return pl.pallas_call(
        _attn_prompt_body,
        grid=(b, dil // n_cls, L // tq),
        in_specs=[cur_spec(0), cur_spec(1), prev_spec(1), cur_spec(2), prev_spec(2),
                  _resident(tab.shape)],
        out_specs=[out_spec, out_spec],
        out_shape=(jax.ShapeDtypeStruct((b, dil, L, GROUP_W), BF16),
                   jax.ShapeDtypeStruct((b, dil, L, GROUP_W), F32)),
        compiler_params=_params(3),
        name=name,
    )(qkv, qkv, qkv, qkv, qkv, tab)


DEC_T = 8
NEW_PAD = 16


def _sample_scores(qkv_ref, cache_refs, tc_refs, tn_ref):
    pad = jnp.zeros((NEW_PAD - DEC_T, GROUP_W), F32)
    scores = []
    for i in range(cache_refs[0].shape[0]):
        rows = slice(i * DEC_T, (i + 1) * DEC_T)
        for g in range(N_GROUPS):
            base = g * QKV_W
            q = qkv_ref[rows, base:base + GROUP_W]
            kn = jnp.concatenate([qkv_ref[rows, base + GROUP_W:base + 2 * GROUP_W], pad], axis=0)
            qm = _per_head_rows(q).astype(BF16)
            s_n = (lax.dot_general(qm, kn.astype(BF16), _NT, preferred_element_type=F32)
                   + tn_ref[g])
            s_c = (jnp.dot(qm, cache_refs[g][i, 0].astype(BF16), preferred_element_type=F32)
                   + tc_refs[g][...])
            scores.append((s_c, s_n))
    return scores


def _sample_outputs(scores, qkv_ref, cache_refs, o_refs, l_refs):
    pad = jnp.zeros((NEW_PAD - DEC_T, GROUP_W), F32)
    scores = iter(scores)
    for i in range(cache_refs[0].shape[0]):
        rows = slice(i * DEC_T, (i + 1) * DEC_T)
        for g in range(N_GROUPS):
            s_c, s_n = next(scores)
            base = g * QKV_W
            vn = jnp.concatenate([qkv_ref[rows, base + 2 * GROUP_W:base + QKV_W], pad], axis=0)
            m = jnp.maximum(jnp.max(s_c, axis=-1, keepdims=True),
                            jnp.max(s_n, axis=-1, keepdims=True))
            p_c = jnp.exp2(s_c - m)
            p_n = jnp.exp2(s_n - m)
            l = jnp.sum(p_c, axis=-1, keepdims=True) + jnp.sum(p_n, axis=-1, keepdims=True)
            acc = jnp.dot(p_n.astype(BF16), vn.astype(BF16), preferred_element_type=F32)
            acc = acc + lax.dot_general(p_c.astype(BF16), cache_refs[g][i, 1].astype(BF16), _NT,
                                        preferred_element_type=F32)
            res = acc / l
            lse = jnp.broadcast_to(m + jnp.log2(l), res.shape)
            o_refs[g][rows, :] = _head_diagonal(res, DEC_T)
            l_refs[g][rows, :] = _head_diagonal(lse, DEC_T)


FF_CHUNK = 1024
DOWN_ROW_GROUPS = 2


def _natural_rows(ref, natural_ref, stage_ref):
    dil, rows, _ = ref.shape
    if dil == 1:
        return ref[0].astype(F32)
    t = dil * rows
    halves = []
    for half in range(GROUP_W // LANES):
        piece = lambda r: ref[r, :, half * LANES:(half + 1) * LANES].astype(F32)
        if dil <= MAX_SUBLANE_STRIDE:
            for r in range(dil):
                natural_ref[half, pl.ds(r, rows, stride=dil), :] = piece(r)
        else:
            s1, s2 = MAX_SUBLANE_STRIDE, dil // MAX_SUBLANE_STRIDE
            assert s1 * s2 == dil and s2 <= MAX_SUBLANE_STRIDE
            for r1 in range(s1):
                for k in range(s2):
                    stage_ref[half, pl.ds(r1 * (t // s1) + k, rows, stride=s2), :] = (
                        piece(s1 * k + r1))
            for r1 in range(s1):
                natural_ref[half, pl.ds(r1, t // s1, stride=s1), :] = (
                    stage_ref[half, pl.ds(r1 * (t // s1), t // s1), :])
        halves.append(natural_ref[half])
    return jnp.concatenate(halves, axis=1)


def _merge_groups(os_, ls_):
    (o0, o1, o2), (l0, l1, l2) = os_, ls_
    m = jnp.maximum(l0, jnp.maximum(l1, l2))
    e0, e1, e2 = jnp.exp2(l0 - m), jnp.exp2(l1 - m), jnp.exp2(l2 - m)
    return ((e0 * o0 + e1 * o1 + e2 * o2) / (e0 + e1 + e2)).astype(BF16)


def _outproj_body(x_ref, o0_ref, o1_ref, o2_ref, l0_ref, l1_ref, l2_ref, sga_ref, mb_ref,
                  wao_ref, wo_ref, w1_ref, w2_ref, nmlp_ref, nfin_ref, y_ref, *scratch):
    if scratch:
        natural_ref, stage_ref = scratch
        os_ = [_natural_rows(ref, natural_ref.at[k], stage_ref.at[k]) for k, ref in
               enumerate((o0_ref, o1_ref, o2_ref))]
        ls_ = [_natural_rows(ref, natural_ref.at[N_GROUPS + k], stage_ref.at[N_GROUPS + k])
               for k, ref in enumerate((l0_ref, l1_ref, l2_ref))]
    else:
        os_ = [ref[...] for ref in (o0_ref, o1_ref, o2_ref)]
        ls_ = [ref[...] for ref in (l0_ref, l1_ref, l2_ref)]
    branch_a = jnp.dot(_merge_groups(os_, ls_), wao_ref[...], preferred_element_type=F32)
    mix = sga_ref[...].astype(F32) * branch_a + mb_ref[...].astype(F32)
    x1 = x_ref[...] + jnp.dot(mix.astype(BF16), wo_ref[...], preferred_element_type=F32)
    h = _rmsnorm(x1, nmlp_ref[...]).astype(BF16)
    acts = []
    for c in range(D_FF // FF_CHUNK):
        a = jnp.dot(h, w1_ref[:, c * FF_CHUNK:(c + 1) * FF_CHUNK], preferred_element_type=F32)
        acts.append(jnp.square(jnp.maximum(a, 0.0)).astype(BF16))
    rows = x1.shape[0] // DOWN_ROW_GROUPS
    for k in range(DOWN_ROW_GROUPS):
        r = slice(k * rows, (k + 1) * rows)
        acc = x1[r]
        for c, a in enumerate(acts):
            acc = acc + jnp.dot(a[r], w2_ref[c * FF_CHUNK:(c + 1) * FF_CHUNK, :],
                                preferred_element_type=F32)
        y_ref[r, :] = _rmsnorm(acc, nfin_ref[...])


def _outproj_prompt(x, os_, ls_, sga, mb, weights, tm):
    b, s, _ = x.shape
    tok = pl.BlockSpec((None, tm, D_MODEL), lambda bi, i: (bi, i, 0))
    grouped = [pl.BlockSpec((None, dil, tm // dil, GROUP_W), lambda bi, i: (bi, 0, i, 0))
               for _, dil in GROUPS]
    return pl.pallas_call(
        _outproj_body,
        grid=(b, s // tm),
        in_specs=[tok] + grouped + grouped + [tok, tok] + [_resident(w.shape) for w in weights],
        out_specs=tok,
        out_shape=jax.ShapeDtypeStruct((b, s, D_MODEL), F32),
        scratch_shapes=[pltpu.VMEM((2 * N_GROUPS, GROUP_W // LANES, tm, LANES), F32)] * 2,
        compiler_params=_params(2),
        name="outproj_ffn_prompt",
    )(x, *os_, *ls_, sga, mb, *weights)


def _outproj_sample(x, os_, ls_, sga, mb, weights, tm):
    n = x.shape[0]
    tok = lambda w: pl.BlockSpec((tm, w), lambda i: (i, 0))
    return pl.pallas_call(
        _outproj_body,
        grid=(n // tm,),
        in_specs=[tok(D_MODEL)] + [tok(GROUP_W)] * 6 + [tok(D_MODEL), tok(D_MODEL)]
        + [_resident(w.shape) for w in weights],
        out_specs=tok(D_MODEL),
        out_shape=jax.ShapeDtypeStruct((n, D_MODEL), F32),
        compiler_params=_params(1),
        name="outproj_ffn_sample",
    )(x, *os_, *ls_, sga, mb, *weights)


def _t5_bucket(dist):
    n = np.asarray(dist)
    max_exact = N_BUCKETS // 2
    large = max_exact + (np.log(np.maximum(n, 1) / max_exact) / np.log(MAX_DISTANCE / max_exact)
                         * (N_BUCKETS - max_exact)).astype(np.int32)
    large = np.minimum(large, N_BUCKETS - 1)
    return np.where(n < max_exact, n, large).astype(np.int32)


def _group_bias(rel_bias, g):
    dil = GROUPS[g][1]
    buckets = _t5_bucket(np.arange(KEYS_PER_QUERY + 1) * dil)
    bias = rel_bias[buckets][:, g * HEADS_PER_GROUP:(g + 1) * HEADS_PER_GROUP].astype(F32)
    return bias * LOG2E


def _toeplitz(vec, n_rows, n_cols, offset):
    h, n = vec.shape
    start = n - 1 - offset
    assert start - (n_rows - 1) >= 0 and start + n_cols <= n
    w = max(n, start + n_cols + 1)
    rev = jnp.pad(vec[:, ::-1], ((0, 0), (0, w - n)))
    skew = jnp.tile(rev, (1, n_rows))[:, :n_rows * (w - 1)].reshape(h, n_rows, w - 1)
    return skew[:, :, start:start + n_cols].reshape(h * n_rows, n_cols)


def _pad_neg(vec, before, after):
    h = vec.shape[0]
    return jnp.concatenate([jnp.full((h, before), NEG, F32), vec, jnp.full((h, after), NEG, F32)],
                           axis=1)


def _prompt_table(bias):
    by_stride = _pad_neg(bias.T, BLOCK - 1, BLOCK - 1)
    tab = _toeplitz(by_stride, BLOCK, 2 * BLOCK, 2 * BLOCK - 1)
    cur = (np.arange(2 * BLOCK) >= BLOCK)[None, :]
    return jnp.stack([jnp.where(cur, tab, NEG), tab])


def _sample_tables(bias, win, dil):
    h = bias.shape[1]
    spread = jnp.concatenate([bias.T[:, :, None], jnp.full((h, KEYS_PER_QUERY + 1, dil - 1), NEG)],
                             axis=2).reshape(h, -1)[:, :KEYS_PER_QUERY * dil + 1]
    tc = _toeplitz(_pad_neg(spread, 0, DEC_T - 1), DEC_T, win, win)
    tn = _toeplitz(_pad_neg(spread, NEW_PAD - 1, 0), DEC_T, NEW_PAD, NEW_PAD - 1)
    return tc, tn


def _cache_transposed(cache):
    bd, wb = cache.shape[:2]
    return jnp.transpose(cache, (0, 2, 3, 4, 1)).reshape(bd, 2, GROUP_W, wb)


def _cache_rows(kvt):
    b, _, _, win = kvt.shape
    kvt = kvt.reshape(b, 2, HEADS_PER_GROUP, HEAD_DIM, win)
    return jnp.transpose(kvt, (0, 4, 1, 2, 3))[None]


TM_INPROJ = 512
TM_OUTPROJ = 512
TQ_ATTN = 4096


def kernel(x_prompt, x_sample, cache_kv_g0, cache_kv_g1, cache_kv_g2, state_conv, w_in, w_att_out,
           w_conv_out, w_o, conv_w, conv_b, rel_bias, norm_mix, norm_mlp, w_ff1, w_ff2, norm_final):
    assert w_in.shape[0] == 1, "one layer"
    b, s, _ = x_prompt.shape
    bd, t, _ = x_sample.shape
    assert t == DEC_T
    wp = w_in[0].astype(BF16)
    wco = w_conv_out[0].astype(BF16)
    wao, wo = w_att_out[0].astype(BF16), w_o[0].astype(BF16)
    w1, w2 = w_ff1[0].astype(BF16), w_ff2[0].astype(BF16)
    nmix, nmlp, nfin = norm_mix[0][None], norm_mlp[0][None], norm_final[None]
    cw, cb = conv_w[0], conv_b[0][None]
    biases = [_group_bias(rel_bias, g) for g in range(N_GROUPS)]

    weights = (wao, wo, w1, w2, nmlp, nfin)

    n = bd * t
    st = state_conv[0]
    p0 = jnp.repeat(st[:, 0], t, axis=0)
    p1 = jnp.repeat(st[:, 1], t, axis=0)
    qkv_s, sga_s, mb_s, u_s = _inproj_sample(
        x_sample.reshape(n, D_MODEL), wp, wco, nmix, cw, cb, p0, p1, min(TM_INPROJ, n))
    tabs = [_sample_tables(biases[g], *GROUPS[g]) for g in range(N_GROUPS)]
    caches_t = [_cache_transposed(c[0]) for c in (cache_kv_g0, cache_kv_g1, cache_kv_g2)]

    (qkv0, qkv1, qkv2, kvt0, kvt1, kvt2, sga, mb, utail, *sample_att) = _inproj_prompt(
        x_prompt, wp, wco, nmix, cw, cb, qkv_s, caches_t, [tb[0] for tb in tabs],
        jnp.stack([tb[1] for tb in tabs]), TM_INPROJ)
    os_s, ls_s = sample_att[:N_GROUPS], sample_att[N_GROUPS:]
    os_, ls_ = [], []
    for g, qkv in enumerate((qkv0, qkv1, qkv2)):
        o, lse = _attn_prompt(qkv, _prompt_table(biases[g]), TQ_ATTN, f"attn_prompt_g{g}")
        os_.append(o)
        ls_.append(lse)
    y_prompt = _outproj_prompt(x_prompt, os_, ls_, sga, mb, weights, TM_OUTPROJ)
    kv_prompt = [_cache_rows(kvt) for kvt in (kvt0, kvt1, kvt2)]
    conv_prompt = utail[:, SUBLANES - 2:][None]

    y_sample = _outproj_sample(x_sample.reshape(n, D_MODEL), os_s, ls_s, sga_s, mb_s, weights,
                               min(TM_OUTPROJ, n))
    kv_sample = [qkv_s[:, g * QKV_W + GROUP_W:(g + 1) * QKV_W].reshape(
        1, bd, t, 2, HEADS_PER_GROUP, HEAD_DIM) for g in range(N_GROUPS)]
    conv_sample = u_s.reshape(bd, t, CONV_W)[:, t - 2:][None]

    return (y_prompt, y_sample.reshape(bd, t, D_MODEL),
            kv_prompt[0], kv_prompt[1], kv_prompt[2], conv_prompt,
            kv_sample[0], kv_sample[1], kv_sample[2], conv_sample)
```

```python
import functools

import jax
import jax.numpy as jnp
import numpy as np
from jax import lax
from jax.experimental import pallas as pl
from jax.experimental.pallas import tpu as pltpu

D_MODEL = 1024
HEAD_DIM = 64
HEADS_PER_GROUP = 4
GROUPS = ((128, 1), (512, 4), (2048, 16))
N_GROUPS = 3
GROUP_W = HEADS_PER_GROUP * HEAD_DIM
QKV_W = 3 * GROUP_W
ATT_W = N_GROUPS * GROUP_W
CONV_W = D_MODEL // 2
D_FF = 4 * D_MODEL
N_BUCKETS = 32
MAX_DISTANCE = 2048
KEYS_PER_QUERY = 128
BLOCK = 128
LANES = 128
SUBLANES = 8
EPS = 1e-6
SCALE = HEAD_DIM ** -0.5
LOG2E = float(np.log2(np.e))
NEG = -1e30

OFF_CONV = 3 * ATT_W
OFF_GA = OFF_CONV + 3 * CONV_W
OFF_GB = OFF_GA + D_MODEL
PROJ_W = OFF_GB + D_MODEL

VMEM_LIMIT_V7X = 56 * 1024 * 1024
F32 = jnp.float32
BF16 = jnp.bfloat16
_NT = (((1,), (1,)), ((), ()))


def _params(n_axes):
    return pltpu.CompilerParams(
        dimension_semantics=("arbitrary",) * n_axes, vmem_limit_bytes=VMEM_LIMIT_V7X)


def _resident(shape):
    return pl.BlockSpec(shape, lambda *_: (0,) * len(shape), pipeline_mode=pl.Buffered(1))


def _rmsnorm(x, g):
    y = x * lax.rsqrt(jnp.mean(x * x, axis=-1, keepdims=True) + EPS)
    return y * g


def _head_of_lane(shape):
    return lax.broadcasted_iota(jnp.int32, shape, len(shape) - 1) // HEAD_DIM


def _per_head_rows(x):
    head = _head_of_lane(x.shape)
    return jnp.concatenate([jnp.where(head == h, x, 0.0) for h in range(HEADS_PER_GROUP)], axis=0)


def _head_diagonal(x, t):
    head = _head_of_lane((t, GROUP_W))
    out = x[0:t, :]
    for h in range(1, HEADS_PER_GROUP):
        out = jnp.where(head == h, x[h * t:(h + 1) * t, :], out)
    return out


MAX_SUBLANE_STRIDE = 4


def _rows_by_residue(slab_ref, stage_ref, dil):
    t = slab_ref.shape[0]
    if dil <= MAX_SUBLANE_STRIDE:
        return [(r, slab_ref[pl.ds(r, t // dil, stride=dil), :]) for r in range(dil)]
    s1, s2 = MAX_SUBLANE_STRIDE, dil // MAX_SUBLANE_STRIDE
    assert s1 * s2 == dil and s2 <= MAX_SUBLANE_STRIDE
    for r1 in range(s1):
        stage_ref[pl.ds(r1 * (t // s1), t // s1), :] = slab_ref[pl.ds(r1, t // s1, stride=s1), :]
    out = {}
    for r1 in range(s1):
        for k in range(s2):
            out[s1 * k + r1] = stage_ref[pl.ds(r1 * (t // s1) + k, t // dil, stride=s2), :]
    return sorted(out.items())


def _inproj_body(*refs, sample):
    if sample:
        (x_ref, win_ref, wco_ref, nm_ref, cw_ref, cb_ref, p0_ref, p1_ref,
         qkv_ref, sga_ref, mb_ref, u_ref) = refs
    else:
        (x_ref, win_ref, wco_ref, nm_ref, cw_ref, cb_ref,
         sqkv_ref, c0_ref, c1_ref, c2_ref, tc0_ref, tc1_ref, tc2_ref, tn_ref,
         qkv0_ref, qkv1_ref, qkv2_ref, kvt0_ref, kvt1_ref, kvt2_ref, sga_ref, mb_ref, u_ref,
         so0_ref, so1_ref, so2_ref, sl0_ref, sl1_ref, sl2_ref,
         carry_ref, perm_ref, stage_ref) = refs
        qkv_refs = (qkv0_ref, qkv1_ref, qkv2_ref)
        kvt_refs = (kvt0_ref, kvt1_ref, kvt2_ref)
        cache_refs = (c0_ref, c1_ref, c2_ref)
    x = x_ref[...]
    tm = x.shape[0]
    xn = _rmsnorm(x, nm_ref[...]).astype(BF16)

    def proj(lo, hi):
        return jnp.dot(xn, win_ref[:, lo:hi], preferred_element_type=F32)

    pc = proj(OFF_CONV, OFF_GA)
    ga = proj(OFF_GA, OFF_GB)
    sga_ref[...] = jax.nn.sigmoid(ga).astype(sga_ref.dtype)
    sgb = jax.nn.sigmoid(proj(OFF_GB, PROJ_W))
    if not sample:
        sample_scores = _sample_scores(sqkv_ref, cache_refs, (tc0_ref, tc1_ref, tc2_ref), tn_ref)
    dilated = {which: proj(which * ATT_W + GROUP_W, (which + 1) * ATT_W) for which in (0, 1)}
    cb, cc, ch = pc[:, :CONV_W], pc[:, CONV_W:2 * CONV_W], pc[:, 2 * CONV_W:]
    u = cc * ch
    row = lax.broadcasted_iota(jnp.int32, (tm, 1), 0)
    if sample:
        p0, p1 = p0_ref[...], p1_ref[...]
        rs = jnp.bitwise_and(row, DEC_T - 1)
        u_ref[...] = u
    else:
        seq_start = pl.program_id(1) == 0
        p0 = jnp.where(seq_start, 0.0, carry_ref[SUBLANES - 2:SUBLANES - 1, :])
        p1 = jnp.where(seq_start, 0.0, carry_ref[SUBLANES - 1:SUBLANES, :])
        rs = row
        u_ref[...] = u[tm - SUBLANES:, :]
    um1 = jnp.where(rs == 0, p1, pltpu.roll(u, 1, 0))
    um2 = jnp.where(rs == 0, p0, jnp.where(rs == 1, p1, pltpu.roll(u, 2, 0)))
    if not sample:
        carry_ref[...] = u[tm - SUBLANES:, :]
    z = cb_ref[...] + cw_ref[0:1, :] * um2 + cw_ref[1:2, :] * um1 + cw_ref[2:3, :] * u
    branch_b = jnp.dot((cb * z).astype(BF16), wco_ref[...], preferred_element_type=F32)
    mb_ref[...] = (sgb * branch_b).astype(mb_ref.dtype)
    dilated[2] = proj(2 * ATT_W + GROUP_W, 3 * ATT_W)
    if not sample:
        _sample_outputs(sample_scores, sqkv_ref, cache_refs, (so0_ref, so1_ref, so2_ref),
                        (sl0_ref, sl1_ref, sl2_ref))

    slab = 0
    for g, (_, dil) in reversed(list(enumerate(GROUPS))):
        for which in (1, 2, 0):
            if dil == 1:
                p = proj(which * ATT_W, which * ATT_W + GROUP_W)
            else:
                p = dilated[which][:, (g - 1) * GROUP_W:g * GROUP_W]
            if which == 0:
                p = p * (SCALE * LOG2E)
            elif not sample:
                keep = kvt_refs[g].shape[-1]
                kvt_refs[g][which - 1] = p[tm - keep:, :].T
            dst = which * GROUP_W
            if sample:
                qkv_ref[:, g * QKV_W + dst:g * QKV_W + dst + GROUP_W] = p
            elif dil == 1:
                qkv_refs[g][0, :, dst:dst + GROUP_W] = p.astype(BF16)
            else:
                words = pltpu.bitcast(p.astype(BF16), jnp.uint32)
                for half in range(GROUP_W // LANES):
                    perm_ref[slab] = words[:, half * LANES:(half + 1) * LANES]
                    lo = dst + half * LANES
                    for j, w in _rows_by_residue(perm_ref.at[slab], stage_ref.at[slab], dil // 2):
                        even = pltpu.bitcast(w << 16, F32)
                        odd = pltpu.bitcast(w & jnp.uint32(0xFFFF0000), F32)
                        qkv_refs[g][2 * j, :, lo:lo + LANES] = even.astype(BF16)
                        qkv_refs[g][2 * j + 1, :, lo:lo + LANES] = odd.astype(BF16)
                    slab += 1


def _inproj_prompt(x, w_in, w_conv_out, norm_mix, conv_w, conv_b, sample_qkv, caches_t, tcs, tn, tm):
    b, s, _ = x.shape
    nt = s // tm
    n_sample = sample_qkv.shape[0]
    seqs = n_sample // DEC_T // (b * nt)
    assert seqs >= 1 and seqs * b * nt * DEC_T == n_sample
    seq_block = lambda bi, i: bi * nt + i
    sample_in = (
        [pl.BlockSpec((seqs * DEC_T, N_GROUPS * QKV_W), lambda bi, i: (seq_block(bi, i), 0))]
        + [pl.BlockSpec((seqs,) + c.shape[1:], lambda bi, i: (seq_block(bi, i), 0, 0, 0))
           for c in caches_t]
        + [_resident(tc.shape) for tc in tcs] + [_resident(tn.shape)])
    sample_out_spec = pl.BlockSpec((seqs * DEC_T, GROUP_W), lambda bi, i: (seq_block(bi, i), 0))
    sample_out = jax.ShapeDtypeStruct((n_sample, GROUP_W), F32)
    tok = lambda w: pl.BlockSpec((None, tm, w), lambda bi, i: (bi, i, 0))
    qkv_specs = [pl.BlockSpec((None, dil, tm // dil, QKV_W), lambda bi, i: (bi, 0, i, 0))
                 for _, dil in GROUPS]
    n_slabs = sum(3 * GROUP_W // LANES for _, dil in GROUPS if dil > 1)

    def kvt_spec(win):
        keep = min(win, tm)
        first = (s - win) // tm
        return pl.BlockSpec((None, 2, GROUP_W, keep),
                            lambda bi, i: (bi, 0, 0, jnp.maximum(i - first, 0)))

    out_shape = (
        [jax.ShapeDtypeStruct((b, dil, s // dil, QKV_W), BF16) for _, dil in GROUPS]
        + [jax.ShapeDtypeStruct((b, 2, GROUP_W, win), F32) for win, _ in GROUPS]
        + [jax.ShapeDtypeStruct((b, s, D_MODEL), BF16),
           jax.ShapeDtypeStruct((b, s, D_MODEL), BF16),
           jax.ShapeDtypeStruct((b, SUBLANES, CONV_W), F32)]
        + [sample_out] * (2 * N_GROUPS))
    return pl.pallas_call(
        functools.partial(_inproj_body, sample=False),
        grid=(b, nt),
        in_specs=[tok(D_MODEL), _resident(w_in.shape), _resident(w_conv_out.shape),
                  _resident(norm_mix.shape), _resident(conv_w.shape), _resident(conv_b.shape)]
        + sample_in,
        out_specs=qkv_specs + [kvt_spec(win) for win, _ in GROUPS]
        + [tok(D_MODEL), tok(D_MODEL),
           pl.BlockSpec((None, SUBLANES, CONV_W), lambda bi, i: (bi, 0, 0))]
        + [sample_out_spec] * (2 * N_GROUPS),
        out_shape=out_shape,
        scratch_shapes=[pltpu.VMEM((SUBLANES, CONV_W), F32),
                        pltpu.VMEM((n_slabs, tm // 2, LANES), jnp.uint32),
                        pltpu.VMEM((n_slabs, tm // 2, LANES), jnp.uint32)],
        compiler_params=_params(2),
        name="inproj_prompt",
    )(x, w_in, w_conv_out, norm_mix, conv_w, conv_b, sample_qkv, *caches_t, *tcs, tn)


def _inproj_sample(x, w_in, w_conv_out, norm_mix, conv_w, conv_b, p0, p1, tm):
    n = x.shape[0]
    tok = lambda w: pl.BlockSpec((tm, w), lambda i: (i, 0))
    out_shape = (
        jax.ShapeDtypeStruct((n, N_GROUPS * QKV_W), F32),
        jax.ShapeDtypeStruct((n, D_MODEL), BF16),
        jax.ShapeDtypeStruct((n, D_MODEL), BF16),
        jax.ShapeDtypeStruct((n, CONV_W), F32),
    )
    return pl.pallas_call(
        functools.partial(_inproj_body, sample=True),
        grid=(n // tm,),
        in_specs=[tok(D_MODEL), _resident(w_in.shape), _resident(w_conv_out.shape),
                  _resident(norm_mix.shape), _resident(conv_w.shape), _resident(conv_b.shape),
                  tok(CONV_W), tok(CONV_W)],
        out_specs=[tok(N_GROUPS * QKV_W), tok(D_MODEL), tok(D_MODEL), tok(CONV_W)],
        out_shape=out_shape,
        compiler_params=_params(1),
        name="inproj_sample",
    )(x, w_in, w_conv_out, norm_mix, conv_w, conv_b, p0, p1)


def _attn_prompt_body(q_ref, kc_ref, kp_ref, vc_ref, vp_ref, tab_ref, o_ref, l_ref):
    i = pl.program_id(2)
    first_tile = jnp.where(i == 0, 0, 1)
    head = _head_of_lane((BLOCK, GROUP_W))
    low_head = lax.broadcasted_iota(jnp.int32, (BLOCK, LANES), 1) < HEAD_DIM
    ones = jnp.ones((2 * BLOCK, LANES), BF16)
    n_cls, tq, _ = q_ref.shape
    for c, j in [(c, j) for c in range(n_cls) for j in range(tq // BLOCK)]:
        q = q_ref[c, j * BLOCK:(j + 1) * BLOCK, :]
        qm = jnp.concatenate(
            [jnp.where(head == h, q, jnp.zeros_like(q)) for h in range(HEADS_PER_GROUP)], axis=0)
        if j == 0:
            k = jnp.concatenate([kp_ref[c], kc_ref[c, 0:BLOCK, :]], axis=0)
            v = jnp.concatenate([vp_ref[c], vc_ref[c, 0:BLOCK, :]], axis=0)
            tab = tab_ref[first_tile]
        else:
            k = kc_ref[c, (j - 1) * BLOCK:(j + 1) * BLOCK, :]
            v = vc_ref[c, (j - 1) * BLOCK:(j + 1) * BLOCK, :]
            tab = tab_ref[1]
        s = lax.dot_general(qm, k, _NT, preferred_element_type=F32) + tab
        m = jnp.max(s, axis=-1, keepdims=True)
        p = jnp.exp2(s - m).astype(BF16)
        halves_o, halves_l = [], []
        for pair in range(HEADS_PER_GROUP // 2):
            w = jnp.concatenate([v[:, pair * LANES:(pair + 1) * LANES], ones], axis=1)
            r = jnp.dot(p[2 * pair * BLOCK:2 * (pair + 1) * BLOCK, :], w,
                        preferred_element_type=F32)
            m0 = jnp.broadcast_to(m[2 * pair * BLOCK:(2 * pair + 1) * BLOCK], (BLOCK, LANES))
            m1 = jnp.broadcast_to(m[(2 * pair + 1) * BLOCK:2 * (pair + 1) * BLOCK], (BLOCK, LANES))
            acc = jnp.where(low_head, r[:BLOCK, :LANES], r[BLOCK:, :LANES])
            l = jnp.where(low_head, r[:BLOCK, LANES:], r[BLOCK:, LANES:])
            halves_o.append(acc / l)
            halves_l.append(jnp.where(low_head, m0, m1) + jnp.log2(l))
        o_ref[c, j * BLOCK:(j + 1) * BLOCK, :] = (
            jnp.concatenate(halves_o, axis=1).astype(o_ref.dtype))
        l_ref[c, j * BLOCK:(j + 1) * BLOCK, :] = jnp.concatenate(halves_l, axis=1)


def _attn_prompt(qkv, tab, rows_per_step, name):
    b, dil, L, _ = qkv.shape
    tq = min(rows_per_step, L)
    n_cls = min(rows_per_step // tq, dil)
    cur_spec = lambda which: pl.BlockSpec(
        (None, n_cls, tq, GROUP_W), lambda bi, r, i: (bi, r, i, which))
    prev_spec = lambda which: pl.BlockSpec(
        (None, n_cls, BLOCK, GROUP_W),
        lambda bi, r, i: (bi, r, jnp.maximum(i * (tq // BLOCK) - 1, 0), which))
    out_spec = pl.BlockSpec((None, n_cls, tq, GROUP_W), lambda bi, r, i: (bi, r, i, 0))
    return pl.pallas_call(
        _attn_prompt_body,
        grid=(b, dil // n_cls, L // tq),
        in_specs=[cur_spec(0), cur_spec(1), prev_spec(1), cur_spec(2), prev_spec(2),
                  _resident(tab.shape)],
        out_specs=[out_spec, out_spec],
        out_shape=(jax.ShapeDtypeStruct((b, dil, L, GROUP_W), BF16),
                   jax.ShapeDtypeStruct((b, dil, L, GROUP_W), F32)),
        compiler_params=_params(3),
        name=name,
    )(qkv, qkv, qkv, qkv, qkv, tab)


DEC_T = 8
NEW_PAD = 16


def _sample_scores(qkv_ref, cache_refs, tc_refs, tn_ref):
    pad = jnp.zeros((NEW_PAD - DEC_T, GROUP_W), F32)
    scores = []
    for i in range(cache_refs[0].shape[0]):
        rows = slice(i * DEC_T, (i + 1) * DEC_T)
        for g in range(N_GROUPS):
            base = g * QKV_W
            q = qkv_ref[rows, base:base + GROUP_W]
            kn = jnp.concatenate([qkv_ref[rows, base + GROUP_W:base + 2 * GROUP_W], pad], axis=0)
            qm = _per_head_rows(q).astype(BF16)
            s_n = (lax.dot_general(qm, kn.astype(BF16), _NT, preferred_element_type=F32)
                   + tn_ref[g])
            s_c = (jnp.dot(qm, cache_refs[g][i, 0].astype(BF16), preferred_element_type=F32)
                   + tc_refs[g][...])
            scores.append((s_c, s_n))
    return scores


def _sample_outputs(scores, qkv_ref, cache_refs, o_refs, l_refs):
    pad = jnp.zeros((NEW_PAD - DEC_T, GROUP_W), F32)
    scores = iter(scores)
    for i in range(cache_refs[0].shape[0]):
        rows = slice(i * DEC_T, (i + 1) * DEC_T)
        for g in range(N_GROUPS):
            s_c, s_n = next(scores)
            base = g * QKV_W
            vn = jnp.concatenate([qkv_ref[rows, base + 2 * GROUP_W:base + QKV_W], pad], axis=0)
            m = jnp.maximum(jnp.max(s_c, axis=-1, keepdims=True),
                            jnp.max(s_n, axis=-1, keepdims=True))
            p_c = jnp.exp2(s_c - m)
            p_n = jnp.exp2(s_n - m)
            l = jnp.sum(p_c, axis=-1, keepdims=True) + jnp.sum(p_n, axis=-1, keepdims=True)
            acc = jnp.dot(p_n.astype(BF16), vn.astype(BF16), preferred_element_type=F32)
            acc = acc + lax.dot_general(p_c.astype(BF16), cache_refs[g][i, 1].astype(BF16), _NT,
                                        preferred_element_type=F32)
            res = acc / l
            lse = jnp.broadcast_to(m + jnp.log2(l), res.shape)
            o_refs[g][rows, :] = _head_diagonal(res, DEC_T)
            l_refs[g][rows, :] = _head_diagonal(lse, DEC_T)


FF_CHUNK = 1024
DOWN_ROW_GROUPS = 2


def _natural_rows(ref, natural_ref, stage_ref):
    dil, rows, _ = ref.shape
    if dil == 1:
        return ref[0].astype(F32)
    t = dil * rows
    halves = []
    for half in range(GROUP_W // LANES):
        piece = lambda r: ref[r, :, half * LANES:(half + 1) * LANES].astype(F32)
        if dil <= MAX_SUBLANE_STRIDE:
            for r in range(dil):
                natural_ref[half, pl.ds(r, rows, stride=dil), :] = piece(r)
        else:
            s1, s2 = MAX_SUBLANE_STRIDE, dil // MAX_SUBLANE_STRIDE
            assert s1 * s2 == dil and s2 <= MAX_SUBLANE_STRIDE
            for r1 in range(s1):
                for k in range(s2):
                    stage_ref[half, pl.ds(r1 * (t // s1) + k, rows, stride=s2), :] = (
                        piece(s1 * k + r1))
            for r1 in range(s1):
                natural_ref[half, pl.ds(r1, t // s1, stride=s1), :] = (
                    stage_ref[half, pl.ds(r1 * (t // s1), t // s1), :])
        halves.append(natural_ref[half])
    return jnp.concatenate(halves, axis=1)


def _merge_groups(os_, ls_):
    (o0, o1, o2), (l0, l1, l2) = os_, ls_
    m = jnp.maximum(l0, jnp.maximum(l1, l2))
    e0, e1, e2 = jnp.exp2(l0 - m), jnp.exp2(l1 - m), jnp.exp2(l2 - m)
    return ((e0 * o0 + e1 * o1 + e2 * o2) / (e0 + e1 + e2)).astype(BF16)


def _outproj_body(x_ref, o0_ref, o1_ref, o2_ref, l0_ref, l1_ref, l2_ref, sga_ref, mb_ref,
                  wao_ref, wo_ref, w1_ref, w2_ref, nmlp_ref, nfin_ref, y_ref, *scratch):
    if scratch:
        natural_ref, stage_ref = scratch
        os_ = [_natural_rows(ref, natural_ref.at[k], stage_ref.at[k]) for k, ref in
               enumerate((o0_ref, o1_ref, o2_ref))]
        ls_ = [_natural_rows(ref, natural_ref.at[N_GROUPS + k], stage_ref.at[N_GROUPS + k])
               for k, ref in enumerate((l0_ref, l1_ref, l2_ref))]
    else:
        os_ = [ref[...] for ref in (o0_ref, o1_ref, o2_ref)]
        ls_ = [ref[...] for ref in (l0_ref, l1_ref, l2_ref)]
    branch_a = jnp.dot(_merge_groups(os_, ls_), wao_ref[...], preferred_element_type=F32)
    mix = sga_ref[...].astype(F32) * branch_a + mb_ref[...].astype(F32)
    x1 = x_ref[...] + jnp.dot(mix.astype(BF16), wo_ref[...], preferred_element_type=F32)
    h = _rmsnorm(x1, nmlp_ref[...]).astype(BF16)
    acts = []
    for c in range(D_FF // FF_CHUNK):
        a = jnp.dot(h, w1_ref[:, c * FF_CHUNK:(c + 1) * FF_CHUNK], preferred_element_type=F32)
        acts.append(jnp.square(jnp.maximum(a, 0.0)).astype(BF16))
    rows = x1.shape[0] // DOWN_ROW_GROUPS
    for k in range(DOWN_ROW_GROUPS):
        r = slice(k * rows, (k + 1) * rows)
        acc = x1[r]
        for c, a in enumerate(acts):
            acc = acc + jnp.dot(a[r], w2_ref[c * FF_CHUNK:(c + 1) * FF_CHUNK, :],
                                preferred_element_type=F32)
        y_ref[r, :] = _rmsnorm(acc, nfin_ref[...])


def _outproj_prompt(x, os_, ls_, sga, mb, weights, tm):
    b, s, _ = x.shape
    tok = pl.BlockSpec((None, tm, D_MODEL), lambda bi, i: (bi, i, 0))
    grouped = [pl.BlockSpec((None, dil, tm // dil, GROUP_W), lambda bi, i: (bi, 0, i, 0))
               for _, dil in GROUPS]
    return pl.pallas_call(
        _outproj_body,
        grid=(b, s // tm),
        in_specs=[tok] + grouped + grouped + [tok, tok] + [_resident(w.shape) for w in weights],
        out_specs=tok,
        out_shape=jax.ShapeDtypeStruct((b, s, D_MODEL), F32),
        scratch_shapes=[pltpu.VMEM((2 * N_GROUPS, GROUP_W // LANES, tm, LANES), F32)] * 2,
        compiler_params=_params(2),
        name="outproj_ffn_prompt",
    )(x, *os_, *ls_, sga, mb, *weights)


def _outproj_sample(x, os_, ls_, sga, mb, weights, tm):
    n = x.shape[0]
    tok = lambda w: pl.BlockSpec((tm, w), lambda i: (i, 0))
    return pl.pallas_call(
        _outproj_body,
        grid=(n // tm,),
        in_specs=[tok(D_MODEL)] + [tok(GROUP_W)] * 6 + [tok(D_MODEL), tok(D_MODEL)]
        + [_resident(w.shape) for w in weights],
        out_specs=tok(D_MODEL),
        out_shape=jax.ShapeDtypeStruct((n, D_MODEL), F32),
        compiler_params=_params(1),
        name="outproj_ffn_sample",
    )(x, *os_, *ls_, sga, mb, *weights)


def _t5_bucket(dist):
    n = np.asarray(dist)
    max_exact = N_BUCKETS // 2
    large = max_exact + (np.log(np.maximum(n, 1) / max_exact) / np.log(MAX_DISTANCE / max_exact)
                         * (N_BUCKETS - max_exact)).astype(np.int32)
    large = np.minimum(large, N_BUCKETS - 1)
    return np.where(n < max_exact, n, large).astype(np.int32)


def _bias_table(rel_bias, g, dist):
    attended = dist >= 0
    bucket = _t5_bucket(np.maximum(dist, 0)).reshape(1, -1)
    onehot = (bucket == np.arange(N_BUCKETS)[:, None]) & attended.reshape(1, -1)
    heads = rel_bias[:, g * HEADS_PER_GROUP:(g + 1) * HEADS_PER_GROUP].astype(F32).T * LOG2E
    tab = jnp.dot(heads, onehot.astype(np.float32), precision=lax.Precision.HIGHEST)
    mask = np.tile(np.where(attended, 0.0, NEG).astype(np.float32), (HEADS_PER_GROUP, 1))
    return tab.reshape(HEADS_PER_GROUP * dist.shape[0], dist.shape[1]) + mask


def _strides_to_dist(strides, dil):
    return np.where((strides >= 0) & (strides <= KEYS_PER_QUERY), strides * dil, -1)


def _prompt_table(rel_bias, g):
    a = np.arange(BLOCK)[:, None]
    c = np.arange(2 * BLOCK)[None, :]
    tab = _bias_table(rel_bias, g, _strides_to_dist(BLOCK + a - c, GROUPS[g][1]))
    return jnp.stack([jnp.where(c >= BLOCK, tab, NEG), tab])


def _sample_tables(rel_bias, g):
    win, dil = GROUPS[g]
    t = np.arange(DEC_T)[:, None]

    def dist(back):
        return np.where(back % dil == 0, _strides_to_dist(back // dil, dil), -1)

    new = np.arange(NEW_PAD)[None, :]
    tn_dist = np.where(new < DEC_T, dist(t - new), -1)
    return (_bias_table(rel_bias, g, dist(win + t - np.arange(win)[None, :])),
            _bias_table(rel_bias, g, tn_dist))


def _cache_transposed(cache):
    bd, wb = cache.shape[:2]
    return jnp.transpose(cache, (0, 2, 3, 4, 1)).reshape(bd, 2, GROUP_W, wb)


def _new_cache_rows(kv, bd):
    t = kv.shape[0] // bd
    kvt = jnp.transpose(kv.reshape(bd, t, 2 * GROUP_W), (1, 2, 0))
    kvt = kvt.reshape(t, 2, HEADS_PER_GROUP, HEAD_DIM, bd)
    return jnp.transpose(kvt, (4, 0, 1, 2, 3))[None]


def _cache_rows(kvt):
    b, _, _, win = kvt.shape
    kvt = kvt.reshape(b, 2, HEADS_PER_GROUP, HEAD_DIM, win)
    return jnp.transpose(kvt, (0, 4, 1, 2, 3))[None]


TM_INPROJ = 512
TM_OUTPROJ = 512
TQ_ATTN = 4096


def kernel(x_prompt, x_sample, cache_kv_g0, cache_kv_g1, cache_kv_g2, state_conv, w_in, w_att_out,
           w_conv_out, w_o, conv_w, conv_b, rel_bias, norm_mix, norm_mlp, w_ff1, w_ff2, norm_final):
    assert w_in.shape[0] == 1, "one layer"
    b, s, _ = x_prompt.shape
    bd, t, _ = x_sample.shape
    assert t == DEC_T
    wp = w_in[0].astype(BF16)
    wco = w_conv_out[0].astype(BF16)
    wao, wo = w_att_out[0].astype(BF16), w_o[0].astype(BF16)
    w1, w2 = w_ff1[0].astype(BF16), w_ff2[0].astype(BF16)
    nmix, nmlp, nfin = norm_mix[0][None], norm_mlp[0][None], norm_final[None]
    cw, cb = conv_w[0], conv_b[0][None]

    weights = (wao, wo, w1, w2, nmlp, nfin)

    n = bd * t
    st = state_conv[0]
    p0 = jnp.repeat(st[:, 0], t, axis=0)
    p1 = jnp.repeat(st[:, 1], t, axis=0)
    qkv_s, sga_s, mb_s, u_s = _inproj_sample(
        x_sample.reshape(n, D_MODEL), wp, wco, nmix, cw, cb, p0, p1, min(TM_INPROJ, n))
    tabs = [_sample_tables(rel_bias, g) for g in range(N_GROUPS)]
    caches_t = [_cache_transposed(c[0]) for c in (cache_kv_g0, cache_kv_g1, cache_kv_g2)]

    (qkv0, qkv1, qkv2, kvt0, kvt1, kvt2, sga, mb, utail, *sample_att) = _inproj_prompt(
        x_prompt, wp, wco, nmix, cw, cb, qkv_s, caches_t, [tb[0] for tb in tabs],
        jnp.stack([tb[1] for tb in tabs]), TM_INPROJ)
    os_s, ls_s = sample_att[:N_GROUPS], sample_att[N_GROUPS:]
    os_, ls_ = [], []
    for g, qkv in enumerate((qkv0, qkv1, qkv2)):
        o, lse = _attn_prompt(qkv, _prompt_table(rel_bias, g), TQ_ATTN, f"attn_prompt_g{g}")
        os_.append(o)
        ls_.append(lse)
    y_prompt = _outproj_prompt(x_prompt, os_, ls_, sga, mb, weights, TM_OUTPROJ)
    kv_prompt = [_cache_rows(kvt) for kvt in (kvt0, kvt1, kvt2)]
    conv_prompt = utail[:, SUBLANES - 2:][None]

    y_sample = _outproj_sample(x_sample.reshape(n, D_MODEL), os_s, ls_s, sga_s, mb_s, weights,
                               min(TM_OUTPROJ, n))
    kv_sample = [_new_cache_rows(qkv_s[:, g * QKV_W + GROUP_W:(g + 1) * QKV_W], bd)
                 for g in range(N_GROUPS)]
    conv_sample = u_s.reshape(bd, t, CONV_W)[:, t - 2:][None]

    return (y_prompt, y_sample.reshape(bd, t, D_MODEL),
            kv_prompt[0], kv_prompt[1], kv_prompt[2], conv_prompt,
            kv_sample[0], kv_sample[1], kv_sample[2], conv_sample)
```

```python
import functools

import jax
import jax.numpy as jnp
import numpy as np
from jax import lax
from jax.experimental import pallas as pl
from jax.experimental.pallas import tpu as pltpu

D_MODEL = 1024
HEAD_DIM = 64
HEADS_PER_GROUP = 4
GROUPS = ((128, 1), (512, 4), (2048, 16))
N_GROUPS = 3
GROUP_W = HEADS_PER_GROUP * HEAD_DIM
QKV_W = 3 * GROUP_W
ATT_W = N_GROUPS * GROUP_W
CONV_W = D_MODEL // 2
D_FF = 4 * D_MODEL
N_BUCKETS = 32
MAX_DISTANCE = 2048
KEYS_PER_QUERY = 128
BLOCK = 128
LANES = 128
SUBLANES = 8
EPS = 1e-6
SCALE = HEAD_DIM ** -0.5
LOG2E = float(np.log2(np.e))
NEG = -1e30

OFF_CONV = 3 * ATT_W
OFF_GA = OFF_CONV + 3 * CONV_W
OFF_GB = OFF_GA + D_MODEL
PROJ_W = OFF_GB + D_MODEL

VMEM_LIMIT_V7X = 56 * 1024 * 1024
F32 = jnp.float32
BF16 = jnp.bfloat16
_NT = (((1,), (1,)), ((), ()))


def _params(n_axes):
    return pltpu.CompilerParams(
        dimension_semantics=("arbitrary",) * n_axes, vmem_limit_bytes=VMEM_LIMIT_V7X)


def _resident(shape):
    return pl.BlockSpec(shape, lambda *_: (0,) * len(shape), pipeline_mode=pl.Buffered(1))


def _rmsnorm(x, g):
    y = x * lax.rsqrt(jnp.mean(x * x, axis=-1, keepdims=True) + EPS)
    return y * g


def _head_of_lane(shape):
    return lax.broadcasted_iota(jnp.int32, shape, len(shape) - 1) // HEAD_DIM


def _per_head_rows(x):
    head = _head_of_lane(x.shape)
    return jnp.concatenate([jnp.where(head == h, x, 0.0) for h in range(HEADS_PER_GROUP)], axis=0)


def _head_diagonal(x, t):
    head = _head_of_lane((t, GROUP_W))
    out = x[0:t, :]
    for h in range(1, HEADS_PER_GROUP):
        out = jnp.where(head == h, x[h * t:(h + 1) * t, :], out)
    return out


MAX_SUBLANE_STRIDE = 4


def _rows_by_residue(slab_ref, stage_ref, dil):
    t = slab_ref.shape[0]
    if dil <= MAX_SUBLANE_STRIDE:
        return [(r, slab_ref[pl.ds(r, t // dil, stride=dil), :]) for r in range(dil)]
    s1, s2 = MAX_SUBLANE_STRIDE, dil // MAX_SUBLANE_STRIDE
    assert s1 * s2 == dil and s2 <= MAX_SUBLANE_STRIDE
    for r1 in range(s1):
        stage_ref[pl.ds(r1 * (t // s1), t // s1), :] = slab_ref[pl.ds(r1, t // s1, stride=s1), :]
    out = {}
    for r1 in range(s1):
        for k in range(s2):
            out[s1 * k + r1] = stage_ref[pl.ds(r1 * (t // s1) + k, t // dil, stride=s2), :]
    return sorted(out.items())


def _inproj_body(*refs, sample):
    if sample:
        (x_ref, win_ref, wco_ref, nm_ref, cw_ref, cb_ref, p0_ref, p1_ref,
         qkv_ref, sga_ref, mb_ref, u_ref) = refs
    else:
        (x_ref, win_ref, wco_ref, nm_ref, cw_ref, cb_ref,
         sqkv_ref, c0_ref, c1_ref, c2_ref, tc0_ref, tc1_ref, tc2_ref, tn_ref,
         qkv0_ref, qkv1_ref, qkv2_ref, kvt0_ref, kvt1_ref, kvt2_ref, sga_ref, mb_ref, u_ref,
         so0_ref, so1_ref, so2_ref, sl0_ref, sl1_ref, sl2_ref,
         carry_ref, perm_ref, stage_ref) = refs
        qkv_refs = (qkv0_ref, qkv1_ref, qkv2_ref)
        kvt_refs = (kvt0_ref, kvt1_ref, kvt2_ref)
        cache_refs = (c0_ref, c1_ref, c2_ref)
    x = x_ref[...]
    tm = x.shape[0]
    xn = _rmsnorm(x, nm_ref[...]).astype(BF16)

    def proj(lo, hi):
        return jnp.dot(xn, win_ref[:, lo:hi], preferred_element_type=F32)

    pc = proj(OFF_CONV, OFF_GA)
    ga = proj(OFF_GA, OFF_GB)
    sga_ref[...] = jax.nn.sigmoid(ga).astype(sga_ref.dtype)
    sgb = jax.nn.sigmoid(proj(OFF_GB, PROJ_W))
    if not sample:
        sample_scores = _sample_scores(sqkv_ref, cache_refs, (tc0_ref, tc1_ref, tc2_ref), tn_ref)
    dilated = {which: proj(which * ATT_W + GROUP_W, (which + 1) * ATT_W) for which in (0, 1)}
    cb, cc, ch = pc[:, :CONV_W], pc[:, CONV_W:2 * CONV_W], pc[:, 2 * CONV_W:]
    u = cc * ch
    row = lax.broadcasted_iota(jnp.int32, (tm, 1), 0)
    if sample:
        p0, p1 = p0_ref[...], p1_ref[...]
        rs = jnp.bitwise_and(row, DEC_T - 1)
        u_ref[...] = u
    else:
        seq_start = pl.program_id(1) == 0
        p0 = jnp.where(seq_start, 0.0, carry_ref[SUBLANES - 2:SUBLANES - 1, :])
        p1 = jnp.where(seq_start, 0.0, carry_ref[SUBLANES - 1:SUBLANES, :])
        rs = row
        u_ref[...] = u[tm - SUBLANES:, :]
    um1 = jnp.where(rs == 0, p1, pltpu.roll(u, 1, 0))
    um2 = jnp.where(rs == 0, p0, jnp.where(rs == 1, p1, pltpu.roll(u, 2, 0)))
    if not sample:
        carry_ref[...] = u[tm - SUBLANES:, :]
    z = cb_ref[...] + cw_ref[0:1, :] * um2 + cw_ref[1:2, :] * um1 + cw_ref[2:3, :] * u
    branch_b = jnp.dot((cb * z).astype(BF16), wco_ref[...], preferred_element_type=F32)
    mb_ref[...] = (sgb * branch_b).astype(mb_ref.dtype)
    dilated[2] = proj(2 * ATT_W + GROUP_W, 3 * ATT_W)
    if not sample:
        _sample_outputs(sample_scores, sqkv_ref, cache_refs, (so0_ref, so1_ref, so2_ref),
                        (sl0_ref, sl1_ref, sl2_ref))

    slab = 0
    for g, (_, dil) in reversed(list(enumerate(GROUPS))):
        for which in (1, 2, 0):
            if dil == 1:
                p = proj(which * ATT_W, which * ATT_W + GROUP_W)
            else:
                p = dilated[which][:, (g - 1) * GROUP_W:g * GROUP_W]
            if which == 0:
                p = p * (SCALE * LOG2E)
            elif not sample:
                keep = kvt_refs[g].shape[-1]
                kvt_refs[g][which - 1] = p[tm - keep:, :].T
            dst = which * GROUP_W
            if sample:
                qkv_ref[:, g * QKV_W + dst:g * QKV_W + dst + GROUP_W] = p
            elif dil == 1:
                qkv_refs[g][0, :, dst:dst + GROUP_W] = p.astype(BF16)
            else:
                words = pltpu.bitcast(p.astype(BF16), jnp.uint32)
                for half in range(GROUP_W // LANES):
                    perm_ref[slab] = words[:, half * LANES:(half + 1) * LANES]
                    lo = dst + half * LANES
                    for j, w in _rows_by_residue(perm_ref.at[slab], stage_ref.at[slab], dil // 2):
                        even = pltpu.bitcast(w << 16, F32)
                        odd = pltpu.bitcast(w & jnp.uint32(0xFFFF0000), F32)
                        qkv_refs[g][2 * j, :, lo:lo + LANES] = even.astype(BF16)
                        qkv_refs[g][2 * j + 1, :, lo:lo + LANES] = odd.astype(BF16)
                    slab += 1


def _inproj_prompt(x, w_in, w_conv_out, norm_mix, conv_w, conv_b, sample_qkv, caches_t, tcs, tn, tm):
    b, s, _ = x.shape
    nt = s // tm
    n_sample = sample_qkv.shape[0]
    seqs = n_sample // DEC_T // (b * nt)
    assert seqs >= 1 and seqs * b * nt * DEC_T == n_sample
    seq_block = lambda bi, i: bi * nt + i
    sample_in = (
        [pl.BlockSpec((seqs * DEC_T, N_GROUPS * QKV_W), lambda bi, i: (seq_block(bi, i), 0))]
        + [pl.BlockSpec((seqs,) + c.shape[1:], lambda bi, i: (seq_block(bi, i), 0, 0, 0))
           for c in caches_t]
        + [_resident(tc.shape) for tc in tcs] + [_resident(tn.shape)])
    sample_out_spec = pl.BlockSpec((seqs * DEC_T, GROUP_W), lambda bi, i: (seq_block(bi, i), 0))
    sample_out = jax.ShapeDtypeStruct((n_sample, GROUP_W), F32)
    tok = lambda w: pl.BlockSpec((None, tm, w), lambda bi, i: (bi, i, 0))
    qkv_specs = [pl.BlockSpec((None, dil, tm // dil, QKV_W), lambda bi, i: (bi, 0, i, 0))
                 for _, dil in GROUPS]
    n_slabs = sum(3 * GROUP_W // LANES for _, dil in GROUPS if dil > 1)

    def kvt_spec(win):
        keep = min(win, tm)
        first = (s - win) // tm
        return pl.BlockSpec((None, 2, GROUP_W, keep),
                            lambda bi, i: (bi, 0, 0, jnp.maximum(i - first, 0)))

    out_shape = (
        [jax.ShapeDtypeStruct((b, dil, s // dil, QKV_W), BF16) for _, dil in GROUPS]
        + [jax.ShapeDtypeStruct((b, 2, GROUP_W, win), F32) for win, _ in GROUPS]
        + [jax.ShapeDtypeStruct((b, s, D_MODEL), BF16),
           jax.ShapeDtypeStruct((b, s, D_MODEL), BF16),
           jax.ShapeDtypeStruct((b, SUBLANES, CONV_W), F32)]
        + [sample_out] * (2 * N_GROUPS))
    return pl.pallas_call(
        functools.partial(_inproj_body, sample=False),
        grid=(b, nt),
        in_specs=[tok(D_MODEL), _resident(w_in.shape), _resident(w_conv_out.shape),
                  _resident(norm_mix.shape), _resident(conv_w.shape), _resident(conv_b.shape)]
        + sample_in,
        out_specs=qkv_specs + [kvt_spec(win) for win, _ in GROUPS]
        + [tok(D_MODEL), tok(D_MODEL),
           pl.BlockSpec((None, SUBLANES, CONV_W), lambda bi, i: (bi, 0, 0))]
        + [sample_out_spec] * (2 * N_GROUPS),
        out_shape=out_shape,
        scratch_shapes=[pltpu.VMEM((SUBLANES, CONV_W), F32),
                        pltpu.VMEM((n_slabs, tm // 2, LANES), jnp.uint32),
                        pltpu.VMEM((n_slabs, tm // 2, LANES), jnp.uint32)],
        compiler_params=_params(2),
        name="inproj_prompt",
    )(x, w_in, w_conv_out, norm_mix, conv_w, conv_b, sample_qkv, *caches_t, *tcs, tn)


def _inproj_sample(x, w_in, w_conv_out, norm_mix, conv_w, conv_b, p0, p1, tm):
    n = x.shape[0]
    tok = lambda w: pl.BlockSpec((tm, w), lambda i: (i, 0))
    out_shape = (
        jax.ShapeDtypeStruct((n, N_GROUPS * QKV_W), F32),
        jax.ShapeDtypeStruct((n, D_MODEL), BF16),
        jax.ShapeDtypeStruct((n, D_MODEL), BF16),
        jax.ShapeDtypeStruct((n, CONV_W), F32),
    )
    return pl.pallas_call(
        functools.partial(_inproj_body, sample=True),
        grid=(n // tm,),
        in_specs=[tok(D_MODEL), _resident(w_in.shape), _resident(w_conv_out.shape),
                  _resident(norm_mix.shape), _resident(conv_w.shape), _resident(conv_b.shape),
                  tok(CONV_W), tok(CONV_W)],
        out_specs=[tok(N_GROUPS * QKV_W), tok(D_MODEL), tok(D_MODEL), tok(CONV_W)],
        out_shape=out_shape,
        compiler_params=_params(1),
        name="inproj_sample",
    )(x, w_in, w_conv_out, norm_mix, conv_w, conv_b, p0, p1)


def _attn_prompt_body(q_ref, kc_ref, kp_ref, vc_ref, vp_ref, tab_ref, o_ref, l_ref):
    i = pl.program_id(2)
    first_tile = jnp.where(i == 0, 0, 1)
    head = _head_of_lane((BLOCK, GROUP_W))
    low_head = lax.broadcasted_iota(jnp.int32, (BLOCK, LANES), 1) < HEAD_DIM
    ones = jnp.ones((2 * BLOCK, LANES), BF16)
    n_cls, tq, _ = q_ref.shape
    for c, j in [(c, j) for c in range(n_cls) for j in range(tq // BLOCK)]:
        q = q_ref[c, j * BLOCK:(j + 1) * BLOCK, :]
        qm = jnp.concatenate(
            [jnp.where(head == h, q, jnp.zeros_like(q)) for h in range(HEADS_PER_GROUP)], axis=0)
        if j == 0:
            k = jnp.concatenate([kp_ref[c], kc_ref[c, 0:BLOCK, :]], axis=0)
            v = jnp.concatenate([vp_ref[c], vc_ref[c, 0:BLOCK, :]], axis=0)
            tab = tab_ref[first_tile]
        else:
            k = kc_ref[c, (j - 1) * BLOCK:(j + 1) * BLOCK, :]
            v = vc_ref[c, (j - 1) * BLOCK:(j + 1) * BLOCK, :]
            tab = tab_ref[1]
        s = lax.dot_general(qm, k, _NT, preferred_element_type=F32) + tab
        m = jnp.max(s, axis=-1, keepdims=True)
        p = jnp.exp2(s - m).astype(BF16)
        halves_o, halves_l = [], []
        for pair in range(HEADS_PER_GROUP // 2):
            w = jnp.concatenate([v[:, pair * LANES:(pair + 1) * LANES], ones], axis=1)
            r = jnp.dot(p[2 * pair * BLOCK:2 * (pair + 1) * BLOCK, :], w,
                        preferred_element_type=F32)
            m0 = jnp.broadcast_to(m[2 * pair * BLOCK:(2 * pair + 1) * BLOCK], (BLOCK, LANES))
            m1 = jnp.broadcast_to(m[(2 * pair + 1) * BLOCK:2 * (pair + 1) * BLOCK], (BLOCK, LANES))
            acc = jnp.where(low_head, r[:BLOCK, :LANES], r[BLOCK:, :LANES])
            l = jnp.where(low_head, r[:BLOCK, LANES:], r[BLOCK:, LANES:])
            halves_o.append(acc / l)
            halves_l.append(jnp.where(low_head, m0, m1) + jnp.log2(l))
        o_ref[c, j * BLOCK:(j + 1) * BLOCK, :] = (
            jnp.concatenate(halves_o, axis=1).astype(o_ref.dtype))
        l_ref[c, j * BLOCK:(j + 1) * BLOCK, :] = jnp.concatenate(halves_l, axis=1)


def _attn_prompt(qkv, tab, rows_per_step, name):
    b, dil, L, _ = qkv.shape
    tq = min(rows_per_step, L)
    n_cls = min(rows_per_step // tq, dil)
    cur_spec = lambda which: pl.BlockSpec(
        (None, n_cls, tq, GROUP_W), lambda bi, r, i: (bi, r, i, which))
    prev_spec = lambda which: pl.BlockSpec(
        (None, n_cls, BLOCK, GROUP_W),
        lambda bi, r, i: (bi, r, jnp.maximum(i * (tq // BLOCK) - 1, 0), which))
    out_spec = pl.BlockSpec((None, n_cls, tq, GROUP_W), lambda bi, r, i: (bi, r, i, 0))
    return pl.pallas_call(
        _attn_prompt_body,
        grid=(b, dil // n_cls, L // tq),
        in_specs=[cur_spec(0), cur_spec(1), prev_spec(1), cur_spec(2), prev_spec(2),
                  _resident(tab.shape)],
        out_specs=[out_spec, out_spec],
        out_shape=(jax.ShapeDtypeStruct((b, dil, L, GROUP_W), BF16),
                   jax.ShapeDtypeStruct((b, dil, L, GROUP_W), F32)),
        compiler_params=_params(3),
        name=name,
    )(qkv, qkv, qkv, qkv, qkv, tab)


DEC_T = 8
NEW_PAD = 16


def _sample_scores(qkv_ref, cache_refs, tc_refs, tn_ref):
    pad = jnp.zeros((NEW_PAD - DEC_T, GROUP_W), F32)
    scores = []
    for i in range(cache_refs[0].shape[0]):
        rows = slice(i * DEC_T, (i + 1) * DEC_T)
        for g in range(N_GROUPS):
            base = g * QKV_W
            q = qkv_ref[rows, base:base + GROUP_W]
            kn = jnp.concatenate([qkv_ref[rows, base + GROUP_W:base + 2 * GROUP_W], pad], axis=0)
            qm = _per_head_rows(q).astype(BF16)
            s_n = (lax.dot_general(qm, kn.astype(BF16), _NT, preferred_element_type=F32)
                   + tn_ref[g])
            s_c = (jnp.dot(qm, cache_refs[g][i, 0].astype(BF16), preferred_element_type=F32)
                   + tc_refs[g][...])
            scores.append((s_c, s_n))
    return scores


def _sample_outputs(scores, qkv_ref, cache_refs, o_refs, l_refs):
    pad = jnp.zeros((NEW_PAD - DEC_T, GROUP_W), F32)
    scores = iter(scores)
    for i in range(cache_refs[0].shape[0]):
        rows = slice(i * DEC_T, (i + 1) * DEC_T)
        for g in range(N_GROUPS):
            s_c, s_n = next(scores)
            base = g * QKV_W
            vn = jnp.concatenate([qkv_ref[rows, base + 2 * GROUP_W:base + QKV_W], pad], axis=0)
            m = jnp.maximum(jnp.max(s_c, axis=-1, keepdims=True),
                            jnp.max(s_n, axis=-1, keepdims=True))
            p_c = jnp.exp2(s_c - m)
            p_n = jnp.exp2(s_n - m)
            l = jnp.sum(p_c, axis=-1, keepdims=True) + jnp.sum(p_n, axis=-1, keepdims=True)
            acc = jnp.dot(p_n.astype(BF16), vn.astype(BF16), preferred_element_type=F32)
            acc = acc + lax.dot_general(p_c.astype(BF16), cache_refs[g][i, 1].astype(BF16), _NT,
                                        preferred_element_type=F32)
            res = acc / l
            lse = jnp.broadcast_to(m + jnp.log2(l), res.shape)
            o_refs[g][rows, :] = _head_diagonal(res, DEC_T)
            l_refs[g][rows, :] = _head_diagonal(lse, DEC_T)


FF_CHUNK = 1024
DOWN_ROW_GROUPS = 2


def _natural_rows(ref, natural_ref, stage_ref):
    dil, rows, _ = ref.shape
    if dil == 1:
        return ref[0].astype(F32)
    t = dil * rows
    halves = []
    for half in range(GROUP_W // LANES):
        piece = lambda r: ref[r, :, half * LANES:(half + 1) * LANES].astype(F32)
        if dil <= MAX_SUBLANE_STRIDE:
            for r in range(dil):
                natural_ref[half, pl.ds(r, rows, stride=dil), :] = piece(r)
        else:
            s1, s2 = MAX_SUBLANE_STRIDE, dil // MAX_SUBLANE_STRIDE
            assert s1 * s2 == dil and s2 <= MAX_SUBLANE_STRIDE
            for r1 in range(s1):
                for k in range(s2):
                    stage_ref[half, pl.ds(r1 * (t // s1) + k, rows, stride=s2), :] = (
                        piece(s1 * k + r1))
            for r1 in range(s1):
                natural_ref[half, pl.ds(r1, t // s1, stride=s1), :] = (
                    stage_ref[half, pl.ds(r1 * (t // s1), t // s1), :])
        halves.append(natural_ref[half])
    return jnp.concatenate(halves, axis=1)


def _merge_groups(os_, ls_):
    (o0, o1, o2), (l0, l1, l2) = os_, ls_
    m = jnp.maximum(l0, jnp.maximum(l1, l2))
    e0, e1, e2 = jnp.exp2(l0 - m), jnp.exp2(l1 - m), jnp.exp2(l2 - m)
    return ((e0 * o0 + e1 * o1 + e2 * o2) / (e0 + e1 + e2)).astype(BF16)


def _outproj_body(x_ref, o0_ref, o1_ref, o2_ref, l0_ref, l1_ref, l2_ref, sga_ref, mb_ref,
                  wao_ref, wo_ref, w1_ref, w2_ref, nmlp_ref, nfin_ref, y_ref, *scratch):
    if scratch:
        natural_ref, stage_ref = scratch
        os_ = [_natural_rows(ref, natural_ref.at[k], stage_ref.at[k]) for k, ref in
               enumerate((o0_ref, o1_ref, o2_ref))]
        ls_ = [_natural_rows(ref, natural_ref.at[N_GROUPS + k], stage_ref.at[N_GROUPS + k])
               for k, ref in enumerate((l0_ref, l1_ref, l2_ref))]
    else:
        os_ = [ref[...] for ref in (o0_ref, o1_ref, o2_ref)]
        ls_ = [ref[...] for ref in (l0_ref, l1_ref, l2_ref)]
    branch_a = jnp.dot(_merge_groups(os_, ls_), wao_ref[...], preferred_element_type=F32)
    mix = sga_ref[...].astype(F32) * branch_a + mb_ref[...].astype(F32)
    x1 = x_ref[...] + jnp.dot(mix.astype(BF16), wo_ref[...], preferred_element_type=F32)
    h = _rmsnorm(x1, nmlp_ref[...]).astype(BF16)
    acts = []
    for c in range(D_FF // FF_CHUNK):
        a = jnp.dot(h, w1_ref[:, c * FF_CHUNK:(c + 1) * FF_CHUNK], preferred_element_type=F32)
        acts.append(jnp.square(jnp.maximum(a, 0.0)).astype(BF16))
    rows = x1.shape[0] // DOWN_ROW_GROUPS
    for k in range(DOWN_ROW_GROUPS):
        r = slice(k * rows, (k + 1) * rows)
        acc = x1[r]
        for c, a in enumerate(acts):
            acc = acc + jnp.dot(a[r], w2_ref[c * FF_CHUNK:(c + 1) * FF_CHUNK, :],
                                preferred_element_type=F32)
        y_ref[r, :] = _rmsnorm(acc, nfin_ref[...])


def _outproj_prompt(x, os_, ls_, sga, mb, weights, tm):
    b, s, _ = x.shape
    tok = pl.BlockSpec((None, tm, D_MODEL), lambda bi, i: (bi, i, 0))
    grouped = [pl.BlockSpec((None, dil, tm // dil, GROUP_W), lambda bi, i: (bi, 0, i, 0))
               for _, dil in GROUPS]
    return pl.pallas_call(
        _outproj_body,
        grid=(b, s // tm),
        in_specs=[tok] + grouped + grouped + [tok, tok] + [_resident(w.shape) for w in weights],
        out_specs=tok,
        out_shape=jax.ShapeDtypeStruct((b, s, D_MODEL), F32),
        scratch_shapes=[pltpu.VMEM((2 * N_GROUPS, GROUP_W // LANES, tm, LANES), F32)] * 2,
        compiler_params=_params(2),
        name="outproj_ffn_prompt",
    )(x, *os_, *ls_, sga, mb, *weights)


def _outproj_sample(x, os_, ls_, sga, mb, weights, tm):
    n = x.shape[0]
    tok = lambda w: pl.BlockSpec((tm, w), lambda i: (i, 0))
    return pl.pallas_call(
        _outproj_body,
        grid=(n // tm,),
        in_specs=[tok(D_MODEL)] + [tok(GROUP_W)] * 6 + [tok(D_MODEL), tok(D_MODEL)]
        + [_resident(w.shape) for w in weights],
        out_specs=tok(D_MODEL),
        out_shape=jax.ShapeDtypeStruct((n, D_MODEL), F32),
        compiler_params=_params(1),
        name="outproj_ffn_sample",
    )(x, *os_, *ls_, sga, mb, *weights)


def _t5_bucket(dist):
    n = np.asarray(dist)
    max_exact = N_BUCKETS // 2
    large = max_exact + (np.log(np.maximum(n, 1) / max_exact) / np.log(MAX_DISTANCE / max_exact)
                         * (N_BUCKETS - max_exact)).astype(np.int32)
    large = np.minimum(large, N_BUCKETS - 1)
    return np.where(n < max_exact, n, large).astype(np.int32)


def _bias_table(rel_bias, g, dist):
    attended = dist >= 0
    bucket = _t5_bucket(np.maximum(dist, 0)).reshape(1, -1)
    onehot = (bucket == np.arange(N_BUCKETS)[:, None]) & attended.reshape(1, -1)
    heads = rel_bias[:, g * HEADS_PER_GROUP:(g + 1) * HEADS_PER_GROUP].astype(F32).T * LOG2E
    tab = jnp.dot(heads, onehot.astype(np.float32), precision=lax.Precision.HIGHEST)
    mask = np.tile(np.where(attended, 0.0, NEG).astype(np.float32), (HEADS_PER_GROUP, 1))
    return tab.reshape(HEADS_PER_GROUP * dist.shape[0], dist.shape[1]) + mask


def _strides_to_dist(strides, dil):
    return np.where((strides >= 0) & (strides <= KEYS_PER_QUERY), strides * dil, -1)


def _prompt_table(rel_bias, g):
    a = np.arange(BLOCK)[:, None]
    c = np.arange(2 * BLOCK)[None, :]
    tab = _bias_table(rel_bias, g, _strides_to_dist(BLOCK + a - c, GROUPS[g][1]))
    return jnp.stack([jnp.where(c >= BLOCK, tab, NEG), tab])


def _sample_tables(rel_bias, g):
    win, dil = GROUPS[g]
    t = np.arange(DEC_T)[:, None]

    def dist(back):
        return np.where(back % dil == 0, _strides_to_dist(back // dil, dil), -1)

    new = np.arange(NEW_PAD)[None, :]
    tn_dist = np.where(new < DEC_T, dist(t - new), -1)
    return (_bias_table(rel_bias, g, dist(win + t - np.arange(win)[None, :])),
            _bias_table(rel_bias, g, tn_dist))


def _cache_transposed(cache):
    bd, wb = cache.shape[:2]
    return jnp.transpose(cache, (0, 2, 3, 4, 1)).reshape(bd, 2, GROUP_W, wb)


def _new_cache_rows(kv, bd):
    t = kv.shape[0] // bd
    kvt = jnp.transpose(kv.reshape(bd, t, 2 * GROUP_W), (1, 2, 0))
    kvt = kvt.reshape(t, 2, HEADS_PER_GROUP, HEAD_DIM, bd)
    return jnp.transpose(kvt, (4, 0, 1, 2, 3))[None]


def _cache_rows(kvt):
    b, _, _, win = kvt.shape
    kvt = kvt.reshape(b, 2, HEADS_PER_GROUP, HEAD_DIM, win)
    return jnp.transpose(kvt, (0, 4, 1, 2, 3))[None]


TM_INPROJ = 512
TM_OUTPROJ = 512
TQ_ATTN = 8192


def kernel(x_prompt, x_sample, cache_kv_g0, cache_kv_g1, cache_kv_g2, state_conv, w_in, w_att_out,
           w_conv_out, w_o, conv_w, conv_b, rel_bias, norm_mix, norm_mlp, w_ff1, w_ff2, norm_final):
    assert w_in.shape[0] == 1, "one layer"
    b, s, _ = x_prompt.shape
    bd, t, _ = x_sample.shape
    assert t == DEC_T
    wp = w_in[0].astype(BF16)
    wco = w_conv_out[0].astype(BF16)
    wao, wo = w_att_out[0].astype(BF16), w_o[0].astype(BF16)
    w1, w2 = w_ff1[0].astype(BF16), w_ff2[0].astype(BF16)
    nmix, nmlp, nfin = norm_mix[0][None], norm_mlp[0][None], norm_final[None]
    cw, cb = conv_w[0], conv_b[0][None]

    weights = (wao, wo, w1, w2, nmlp, nfin)

    n = bd * t
    st = state_conv[0]
    p0 = jnp.repeat(st[:, 0], t, axis=0)
    p1 = jnp.repeat(st[:, 1], t, axis=0)
    qkv_s, sga_s, mb_s, u_s = _inproj_sample(
        x_sample.reshape(n, D_MODEL), wp, wco, nmix, cw, cb, p0, p1, min(TM_INPROJ, n))
    tabs = [_sample_tables(rel_bias, g) for g in range(N_GROUPS)]
    caches_t = [_cache_transposed(c[0]) for c in (cache_kv_g0, cache_kv_g1, cache_kv_g2)]

    (qkv0, qkv1, qkv2, kvt0, kvt1, kvt2, sga, mb, utail, *sample_att) = _inproj_prompt(
        x_prompt, wp, wco, nmix, cw, cb, qkv_s, caches_t, [tb[0] for tb in tabs],
        jnp.stack([tb[1] for tb in tabs]), TM_INPROJ)
    os_s, ls_s = sample_att[:N_GROUPS], sample_att[N_GROUPS:]
    os_, ls_ = [], []
    for g, qkv in enumerate((qkv0, qkv1, qkv2)):
        o, lse = _attn_prompt(qkv, _prompt_table(rel_bias, g), TQ_ATTN, f"attn_prompt_g{g}")
        os_.append(o)
        ls_.append(lse)
    y_prompt = _outproj_prompt(x_prompt, os_, ls_, sga, mb, weights, TM_OUTPROJ)
    kv_prompt = [_cache_rows(kvt) for kvt in (kvt0, kvt1, kvt2)]
    conv_prompt = utail[:, SUBLANES - 2:][None]

    y_sample = _outproj_sample(x_sample.reshape(n, D_MODEL), os_s, ls_s, sga_s, mb_s, weights,
                               min(TM_OUTPROJ, n))
    kv_sample = [_new_cache_rows(qkv_s[:, g * QKV_W + GROUP_W:(g + 1) * QKV_W], bd)
                 for g in range(N_GROUPS)]
    conv_sample = u_s.reshape(bd, t, CONV_W)[:, t - 2:][None]

    return (y_prompt, y_sample.reshape(bd, t, D_MODEL),
            kv_prompt[0], kv_prompt[1], kv_prompt[2], conv_prompt,
            kv_sample[0], kv_sample[1], kv_sample[2], conv_sample)
```

```python
import functools

import jax
import jax.numpy as jnp
import numpy as np
from jax import lax
from jax.experimental import pallas as pl
from jax.experimental.pallas import tpu as pltpu

D_MODEL = 1024
HEAD_DIM = 64
HEADS_PER_GROUP = 4
GROUPS = ((128, 1), (512, 4), (2048, 16))
N_GROUPS = 3
GROUP_W = HEADS_PER_GROUP * HEAD_DIM
QKV_W = 3 * GROUP_W
ATT_W = N_GROUPS * GROUP_W
CONV_W = D_MODEL // 2
D_FF = 4 * D_MODEL
N_BUCKETS = 32
MAX_DISTANCE = 2048
KEYS_PER_QUERY = 128
BLOCK = 128
LANES = 128
SUBLANES = 8
EPS = 1e-6
SCALE = HEAD_DIM ** -0.5
LOG2E = float(np.log2(np.e))
NEG = -1e30

OFF_CONV = 3 * ATT_W
OFF_GA = OFF_CONV + 3 * CONV_W
OFF_GB = OFF_GA + D_MODEL
PROJ_W = OFF_GB + D_MODEL

VMEM_LIMIT_V7X = 56 * 1024 * 1024
F32 = jnp.float32
BF16 = jnp.bfloat16
_NT = (((1,), (1,)), ((), ()))


def _params(n_axes):
    return pltpu.CompilerParams(
        dimension_semantics=("arbitrary",) * n_axes, vmem_limit_bytes=VMEM_LIMIT_V7X)


def _resident(shape):
    return pl.BlockSpec(shape, lambda *_: (0,) * len(shape), pipeline_mode=pl.Buffered(1))


def _rmsnorm(x, g):
    y = x * lax.rsqrt(jnp.mean(x * x, axis=-1, keepdims=True) + EPS)
    return y * g


def _head_of_lane(shape):
    return lax.broadcasted_iota(jnp.int32, shape, len(shape) - 1) // HEAD_DIM


def _per_head_rows(x):
    head = _head_of_lane(x.shape)
    return jnp.concatenate([jnp.where(head == h, x, 0.0) for h in range(HEADS_PER_GROUP)], axis=0)


def _head_diagonal(x, t):
    head = _head_of_lane((t, GROUP_W))
    out = x[0:t, :]
    for h in range(1, HEADS_PER_GROUP):
        out = jnp.where(head == h, x[h * t:(h + 1) * t, :], out)
    return out


MAX_SUBLANE_STRIDE = 4


def _rows_by_residue(slab_ref, stage_ref, dil):
    t = slab_ref.shape[0]
    if dil <= MAX_SUBLANE_STRIDE:
        return [(r, slab_ref[pl.ds(r, t // dil, stride=dil), :]) for r in range(dil)]
    s1, s2 = MAX_SUBLANE_STRIDE, dil // MAX_SUBLANE_STRIDE
    assert s1 * s2 == dil and s2 <= MAX_SUBLANE_STRIDE
    for r1 in range(s1):
        stage_ref[pl.ds(r1 * (t // s1), t // s1), :] = slab_ref[pl.ds(r1, t // s1, stride=s1), :]
    out = {}
    for r1 in range(s1):
        for k in range(s2):
            out[s1 * k + r1] = stage_ref[pl.ds(r1 * (t // s1) + k, t // dil, stride=s2), :]
    return sorted(out.items())


def _inproj_body(*refs, sample):
    if sample:
        (x_ref, win_ref, wco_ref, nm_ref, cw_ref, cb_ref, p0_ref, p1_ref,
         qkv_ref, sga_ref, mb_ref, u_ref) = refs
    else:
        (x_ref, win_ref, wco_ref, nm_ref, cw_ref, cb_ref,
         sqkv_ref, c0_ref, c1_ref, c2_ref, tc0_ref, tc1_ref, tc2_ref, tn_ref,
         qkv0_ref, qkv1_ref, qkv2_ref, kvt0_ref, kvt1_ref, kvt2_ref, sga_ref, mb_ref, u_ref,
         so0_ref, so1_ref, so2_ref, sl0_ref, sl1_ref, sl2_ref,
         carry_ref, perm_ref, stage_ref) = refs
        qkv_refs = (qkv0_ref, qkv1_ref, qkv2_ref)
        kvt_refs = (kvt0_ref, kvt1_ref, kvt2_ref)
        cache_refs = (c0_ref, c1_ref, c2_ref)
    if not sample:
        sample_scores = _sample_scores(sqkv_ref, cache_refs, (tc0_ref, tc1_ref, tc2_ref), tn_ref)
    x = x_ref[...]
    tm = x.shape[0]
    xn = _rmsnorm(x, nm_ref[...]).astype(BF16)

    def proj(lo, hi):
        return jnp.dot(xn, win_ref[:, lo:hi], preferred_element_type=F32)

    pc = proj(OFF_CONV, OFF_GA)
    ga = proj(OFF_GA, OFF_GB)
    sga_ref[...] = jax.nn.sigmoid(ga).astype(sga_ref.dtype)
    sgb = jax.nn.sigmoid(proj(OFF_GB, PROJ_W))
    dilated = {which: proj(which * ATT_W + GROUP_W, (which + 1) * ATT_W) for which in (0, 1)}
    cb, cc, ch = pc[:, :CONV_W], pc[:, CONV_W:2 * CONV_W], pc[:, 2 * CONV_W:]
    u = cc * ch
    row = lax.broadcasted_iota(jnp.int32, (tm, 1), 0)
    if sample:
        p0, p1 = p0_ref[...], p1_ref[...]
        rs = jnp.bitwise_and(row, DEC_T - 1)
        u_ref[...] = u
    else:
        seq_start = pl.program_id(1) == 0
        p0 = jnp.where(seq_start, 0.0, carry_ref[SUBLANES - 2:SUBLANES - 1, :])
        p1 = jnp.where(seq_start, 0.0, carry_ref[SUBLANES - 1:SUBLANES, :])
        rs = row
        u_ref[...] = u[tm - SUBLANES:, :]
    um1 = jnp.where(rs == 0, p1, pltpu.roll(u, 1, 0))
    um2 = jnp.where(rs == 0, p0, jnp.where(rs == 1, p1, pltpu.roll(u, 2, 0)))
    if not sample:
        carry_ref[...] = u[tm - SUBLANES:, :]
    z = cb_ref[...] + cw_ref[0:1, :] * um2 + cw_ref[1:2, :] * um1 + cw_ref[2:3, :] * u
    branch_b = jnp.dot((cb * z).astype(BF16), wco_ref[...], preferred_element_type=F32)
    mb_ref[...] = (sgb * branch_b).astype(mb_ref.dtype)
    dilated[2] = proj(2 * ATT_W + GROUP_W, 3 * ATT_W)
    if not sample:
        _sample_outputs(sample_scores, sqkv_ref, cache_refs, (so0_ref, so1_ref, so2_ref),
                        (sl0_ref, sl1_ref, sl2_ref))

    slab = 0
    for g, (_, dil) in reversed(list(enumerate(GROUPS))):
        for which in (1, 2, 0):
            if dil == 1:
                p = proj(which * ATT_W, which * ATT_W + GROUP_W)
            else:
                p = dilated[which][:, (g - 1) * GROUP_W:g * GROUP_W]
            if which == 0:
                p = p * (SCALE * LOG2E)
            elif not sample:
                keep = kvt_refs[g].shape[-1]
                kvt_refs[g][which - 1] = p[tm - keep:, :].T
            dst = which * GROUP_W
            if sample:
                qkv_ref[:, g * QKV_W + dst:g * QKV_W + dst + GROUP_W] = p
            elif dil == 1:
                qkv_refs[g][0, :, dst:dst + GROUP_W] = p.astype(BF16)
            else:
                words = pltpu.bitcast(p.astype(BF16), jnp.uint32)
                for half in range(GROUP_W // LANES):
                    perm_ref[slab] = words[:, half * LANES:(half + 1) * LANES]
                    lo = dst + half * LANES
                    for j, w in _rows_by_residue(perm_ref.at[slab], stage_ref.at[slab], dil // 2):
                        even = pltpu.bitcast(w << 16, F32)
                        odd = pltpu.bitcast(w & jnp.uint32(0xFFFF0000), F32)
                        qkv_refs[g][2 * j, :, lo:lo + LANES] = even.astype(BF16)
                        qkv_refs[g][2 * j + 1, :, lo:lo + LANES] = odd.astype(BF16)
                    slab += 1


def _inproj_prompt(x, w_in, w_conv_out, norm_mix, conv_w, conv_b, sample_qkv, caches_t, tcs, tn, tm):
    b, s, _ = x.shape
    nt = s // tm
    n_sample = sample_qkv.shape[0]
    seqs = n_sample // DEC_T // (b * nt)
    assert seqs >= 1 and seqs * b * nt * DEC_T == n_sample
    seq_block = lambda bi, i: bi * nt + i
    sample_in = (
        [pl.BlockSpec((seqs * DEC_T, N_GROUPS * QKV_W), lambda bi, i: (seq_block(bi, i), 0))]
        + [pl.BlockSpec((seqs,) + c.shape[1:], lambda bi, i: (seq_block(bi, i), 0, 0, 0))
           for c in caches_t]
        + [_resident(tc.shape) for tc in tcs] + [_resident(tn.shape)])
    sample_out_spec = pl.BlockSpec((seqs * DEC_T, GROUP_W), lambda bi, i: (seq_block(bi, i), 0))
    sample_out = jax.ShapeDtypeStruct((n_sample, GROUP_W), F32)
    tok = lambda w: pl.BlockSpec((None, tm, w), lambda bi, i: (bi, i, 0))
    qkv_specs = [pl.BlockSpec((None, dil, tm // dil, QKV_W), lambda bi, i: (bi, 0, i, 0))
                 for _, dil in GROUPS]
    n_slabs = sum(3 * GROUP_W // LANES for _, dil in GROUPS if dil > 1)

    def kvt_spec(win):
        keep = min(win, tm)
        first = (s - win) // tm
        return pl.BlockSpec((None, 2, GROUP_W, keep),
                            lambda bi, i: (bi, 0, 0, jnp.maximum(i - first, 0)))

    out_shape = (
        [jax.ShapeDtypeStruct((b, dil, s // dil, QKV_W), BF16) for _, dil in GROUPS]
        + [jax.ShapeDtypeStruct((b, 2, GROUP_W, win), F32) for win, _ in GROUPS]
        + [jax.ShapeDtypeStruct((b, s, D_MODEL), BF16),
           jax.ShapeDtypeStruct((b, s, D_MODEL), BF16),
           jax.ShapeDtypeStruct((b, SUBLANES, CONV_W), F32)]
        + [sample_out] * (2 * N_GROUPS))
    return pl.pallas_call(
        functools.partial(_inproj_body, sample=False),
        grid=(b, nt),
        in_specs=[tok(D_MODEL), _resident(w_in.shape), _resident(w_conv_out.shape),
                  _resident(norm_mix.shape), _resident(conv_w.shape), _resident(conv_b.shape)]
        + sample_in,
        out_specs=qkv_specs + [kvt_spec(win) for win, _ in GROUPS]
        + [tok(D_MODEL), tok(D_MODEL),
           pl.BlockSpec((None, SUBLANES, CONV_W), lambda bi, i: (bi, 0, 0))]
        + [sample_out_spec] * (2 * N_GROUPS),
        out_shape=out_shape,
        scratch_shapes=[pltpu.VMEM((SUBLANES, CONV_W), F32),
                        pltpu.VMEM((n_slabs, tm // 2, LANES), jnp.uint32),
                        pltpu.VMEM((n_slabs, tm // 2, LANES), jnp.uint32)],
        compiler_params=_params(2),
        name="inproj_prompt",
    )(x, w_in, w_conv_out, norm_mix, conv_w, conv_b, sample_qkv, *caches_t, *tcs, tn)


def _inproj_sample(x, w_in, w_conv_out, norm_mix, conv_w, conv_b, p0, p1, tm):
    n = x.shape[0]
    tok = lambda w: pl.BlockSpec((tm, w), lambda i: (i, 0))
    out_shape = (
        jax.ShapeDtypeStruct((n, N_GROUPS * QKV_W), F32),
        jax.ShapeDtypeStruct((n, D_MODEL), BF16),
        jax.ShapeDtypeStruct((n, D_MODEL), BF16),
        jax.ShapeDtypeStruct((n, CONV_W), F32),
    )
    return pl.pallas_call(
        functools.partial(_inproj_body, sample=True),
        grid=(n // tm,),
        in_specs=[tok(D_MODEL), _resident(w_in.shape), _resident(w_conv_out.shape),
                  _resident(norm_mix.shape), _resident(conv_w.shape), _resident(conv_b.shape),
                  tok(CONV_W), tok(CONV_W)],
        out_specs=[tok(N_GROUPS * QKV_W), tok(D_MODEL), tok(D_MODEL), tok(CONV_W)],
        out_shape=out_shape,
        compiler_params=_params(1),
        name="inproj_sample",
    )(x, w_in, w_conv_out, norm_mix, conv_w, conv_b, p0, p1)


def _attn_prompt_body(q_ref, kc_ref, kp_ref, vc_ref, vp_ref, tab_ref, o_ref, l_ref):
    i = pl.program_id(2)
    first_tile = jnp.where(i == 0, 0, 1)
    head = _head_of_lane((BLOCK, GROUP_W))
    low_head = lax.broadcasted_iota(jnp.int32, (BLOCK, LANES), 1) < HEAD_DIM
    ones = jnp.ones((2 * BLOCK, LANES), BF16)
    n_cls, tq, _ = q_ref.shape
    for c, j in [(c, j) for c in range(n_cls) for j in range(tq // BLOCK)]:
        q = q_ref[c, j * BLOCK:(j + 1) * BLOCK, :]
        qm = jnp.concatenate(
            [jnp.where(head == h, q, jnp.zeros_like(q)) for h in range(HEADS_PER_GROUP)], axis=0)
        if j == 0:
            k = jnp.concatenate([kp_ref[c], kc_ref[c, 0:BLOCK, :]], axis=0)
            v = jnp.concatenate([vp_ref[c], vc_ref[c, 0:BLOCK, :]], axis=0)
            tab = tab_ref[first_tile]
        else:
            k = kc_ref[c, (j - 1) * BLOCK:(j + 1) * BLOCK, :]
            v = vc_ref[c, (j - 1) * BLOCK:(j + 1) * BLOCK, :]
            tab = tab_ref[1]
        s = lax.dot_general(qm, k, _NT, preferred_element_type=F32) + tab
        m = jnp.max(s, axis=-1, keepdims=True)
        p = jnp.exp2(s - m).astype(BF16)
        halves_o, halves_l = [], []
        for pair in range(HEADS_PER_GROUP // 2):
            w = jnp.concatenate([v[:, pair * LANES:(pair + 1) * LANES], ones], axis=1)
            r = jnp.dot(p[2 * pair * BLOCK:2 * (pair + 1) * BLOCK, :], w,
                        preferred_element_type=F32)
            m0 = jnp.broadcast_to(m[2 * pair * BLOCK:(2 * pair + 1) * BLOCK], (BLOCK, LANES))
            m1 = jnp.broadcast_to(m[(2 * pair + 1) * BLOCK:2 * (pair + 1) * BLOCK], (BLOCK, LANES))
            acc = jnp.where(low_head, r[:BLOCK, :LANES], r[BLOCK:, :LANES])
            l = jnp.where(low_head, r[:BLOCK, LANES:], r[BLOCK:, LANES:])
            halves_o.append(acc / l)
            halves_l.append(jnp.where(low_head, m0, m1) + jnp.log2(l))
        o_ref[c, j * BLOCK:(j + 1) * BLOCK, :] = (
            jnp.concatenate(halves_o, axis=1).astype(o_ref.dtype))
        l_ref[c, j * BLOCK:(j + 1) * BLOCK, :] = jnp.concatenate(halves_l, axis=1)


def _attn_prompt(qkv, tab, rows_per_step, name):
    b, dil, L, _ = qkv.shape
    tq = min(rows_per_step, L)
    n_cls = min(rows_per_step // tq, dil)
    cur_spec = lambda which: pl.BlockSpec(
        (None, n_cls, tq, GROUP_W), lambda bi, r, i: (bi, r, i, which))
    prev_spec = lambda which: pl.BlockSpec(
        (None, n_cls, BLOCK, GROUP_W),
        lambda bi, r, i: (bi, r, jnp.maximum(i * (tq // BLOCK) - 1, 0), which))
    out_spec = pl.BlockSpec((None, n_cls, tq, GROUP_W), lambda bi, r, i: (bi, r, i, 0))
    return pl.pallas_call(
        _attn_prompt_body,
        grid=(b, dil // n_cls, L // tq),
        in_specs=[cur_spec(0), cur_spec(1), prev_spec(1), cur_spec(2), prev_spec(2),
                  _resident(tab.shape)],
        out_specs=[out_spec, out_spec],
        out_shape=(jax.ShapeDtypeStruct((b, dil, L, GROUP_W), BF16),
                   jax.ShapeDtypeStruct((b, dil, L, GROUP_W), F32)),
        compiler_params=_params(3),
        name=name,
    )(qkv, qkv, qkv, qkv, qkv, tab)


DEC_T = 8
NEW_PAD = 16


def _sample_scores(qkv_ref, cache_refs, tc_refs, tn_ref):
    pad = jnp.zeros((NEW_PAD - DEC_T, GROUP_W), F32)
    scores = []
    for i in range(cache_refs[0].shape[0]):
        rows = slice(i * DEC_T, (i + 1) * DEC_T)
        for g in range(N_GROUPS):
            base = g * QKV_W
            q = qkv_ref[rows, base:base + GROUP_W]
            kn = jnp.concatenate([qkv_ref[rows, base + GROUP_W:base + 2 * GROUP_W], pad], axis=0)
            qm = _per_head_rows(q).astype(BF16)
            s_n = (lax.dot_general(qm, kn.astype(BF16), _NT, preferred_element_type=F32)
                   + tn_ref[g])
            s_c = (jnp.dot(qm, cache_refs[g][i, 0].astype(BF16), preferred_element_type=F32)
                   + tc_refs[g][...])
            scores.append((s_c, s_n))
    return scores


def _sample_outputs(scores, qkv_ref, cache_refs, o_refs, l_refs):
    pad = jnp.zeros((NEW_PAD - DEC_T, GROUP_W), F32)
    scores = iter(scores)
    for i in range(cache_refs[0].shape[0]):
        rows = slice(i * DEC_T, (i + 1) * DEC_T)
        for g in range(N_GROUPS):
            s_c, s_n = next(scores)
            base = g * QKV_W
            vn = jnp.concatenate([qkv_ref[rows, base + 2 * GROUP_W:base + QKV_W], pad], axis=0)
            m = jnp.maximum(jnp.max(s_c, axis=-1, keepdims=True),
                            jnp.max(s_n, axis=-1, keepdims=True))
            p_c = jnp.exp2(s_c - m)
            p_n = jnp.exp2(s_n - m)
            l = jnp.sum(p_c, axis=-1, keepdims=True) + jnp.sum(p_n, axis=-1, keepdims=True)
            acc = jnp.dot(p_n.astype(BF16), vn.astype(BF16), preferred_element_type=F32)
            acc = acc + lax.dot_general(p_c.astype(BF16), cache_refs[g][i, 1].astype(BF16), _NT,
                                        preferred_element_type=F32)
            res = acc / l
            lse = jnp.broadcast_to(m + jnp.log2(l), res.shape)
            o_refs[g][rows, :] = _head_diagonal(res, DEC_T)
            l_refs[g][rows, :] = _head_diagonal(lse, DEC_T)


FF_CHUNK = 1024
DOWN_ROW_GROUPS = 2


def _natural_rows(ref, natural_ref, stage_ref):
    dil, rows, _ = ref.shape
    if dil == 1:
        return ref[0].astype(F32)
    t = dil * rows
    halves = []
    for half in range(GROUP_W // LANES):
        piece = lambda r: ref[r, :, half * LANES:(half + 1) * LANES].astype(F32)
        if dil <= MAX_SUBLANE_STRIDE:
            for r in range(dil):
                natural_ref[half, pl.ds(r, rows, stride=dil), :] = piece(r)
        else:
            s1, s2 = MAX_SUBLANE_STRIDE, dil // MAX_SUBLANE_STRIDE
            assert s1 * s2 == dil and s2 <= MAX_SUBLANE_STRIDE
            for r1 in range(s1):
                for k in range(s2):
                    stage_ref[half, pl.ds(r1 * (t // s1) + k, rows, stride=s2), :] = (
                        piece(s1 * k + r1))
            for r1 in range(s1):
                natural_ref[half, pl.ds(r1, t // s1, stride=s1), :] = (
                    stage_ref[half, pl.ds(r1 * (t // s1), t // s1), :])
        halves.append(natural_ref[half])
    return jnp.concatenate(halves, axis=1)


def _merge_groups(os_, ls_):
    (o0, o1, o2), (l0, l1, l2) = os_, ls_
    m = jnp.maximum(l0, jnp.maximum(l1, l2))
    e0, e1, e2 = jnp.exp2(l0 - m), jnp.exp2(l1 - m), jnp.exp2(l2 - m)
    return ((e0 * o0 + e1 * o1 + e2 * o2) / (e0 + e1 + e2)).astype(BF16)


def _outproj_body(x_ref, o0_ref, o1_ref, o2_ref, l0_ref, l1_ref, l2_ref, sga_ref, mb_ref,
                  wao_ref, wo_ref, w1_ref, w2_ref, nmlp_ref, nfin_ref, y_ref, *scratch):
    if scratch:
        natural_ref, stage_ref = scratch
        os_ = [_natural_rows(ref, natural_ref.at[k], stage_ref.at[k]) for k, ref in
               enumerate((o0_ref, o1_ref, o2_ref))]
        ls_ = [_natural_rows(ref, natural_ref.at[N_GROUPS + k], stage_ref.at[N_GROUPS + k])
               for k, ref in enumerate((l0_ref, l1_ref, l2_ref))]
    else:
        os_ = [ref[...] for ref in (o0_ref, o1_ref, o2_ref)]
        ls_ = [ref[...] for ref in (l0_ref, l1_ref, l2_ref)]
    branch_a = jnp.dot(_merge_groups(os_, ls_), wao_ref[...], preferred_element_type=F32)
    mix = sga_ref[...].astype(F32) * branch_a + mb_ref[...].astype(F32)
    x1 = x_ref[...] + jnp.dot(mix.astype(BF16), wo_ref[...], preferred_element_type=F32)
    h = _rmsnorm(x1, nmlp_ref[...]).astype(BF16)
    acts = []
    for c in range(D_FF // FF_CHUNK):
        a = jnp.dot(h, w1_ref[:, c * FF_CHUNK:(c + 1) * FF_CHUNK], preferred_element_type=F32)
        acts.append(jnp.square(jnp.maximum(a, 0.0)).astype(BF16))
    rows = x1.shape[0] // DOWN_ROW_GROUPS
    for k in range(DOWN_ROW_GROUPS):
        r = slice(k * rows, (k + 1) * rows)
        acc = x1[r]
        for c, a in enumerate(acts):
            acc = acc + jnp.dot(a[r], w2_ref[c * FF_CHUNK:(c + 1) * FF_CHUNK, :],
                                preferred_element_type=F32)
        y_ref[r, :] = _rmsnorm(acc, nfin_ref[...])


def _outproj_prompt(x, os_, ls_, sga, mb, weights, tm):
    b, s, _ = x.shape
    tok = pl.BlockSpec((None, tm, D_MODEL), lambda bi, i: (bi, i, 0))
    grouped = [pl.BlockSpec((None, dil, tm // dil, GROUP_W), lambda bi, i: (bi, 0, i, 0))
               for _, dil in GROUPS]
    return pl.pallas_call(
        _outproj_body,
        grid=(b, s // tm),
        in_specs=[tok] + grouped + grouped + [tok, tok] + [_resident(w.shape) for w in weights],
        out_specs=tok,
        out_shape=jax.ShapeDtypeStruct((b, s, D_MODEL), F32),
        scratch_shapes=[pltpu.VMEM((2 * N_GROUPS, GROUP_W // LANES, tm, LANES), F32)] * 2,
        compiler_params=_params(2),
        name="outproj_ffn_prompt",
    )(x, *os_, *ls_, sga, mb, *weights)


def _outproj_sample(x, os_, ls_, sga, mb, weights, tm):
    n = x.shape[0]
    tok = lambda w: pl.BlockSpec((tm, w), lambda i: (i, 0))
    return pl.pallas_call(
        _outproj_body,
        grid=(n // tm,),
        in_specs=[tok(D_MODEL)] + [tok(GROUP_W)] * 6 + [tok(D_MODEL), tok(D_MODEL)]
        + [_resident(w.shape) for w in weights],
        out_specs=tok(D_MODEL),
        out_shape=jax.ShapeDtypeStruct((n, D_MODEL), F32),
        compiler_params=_params(1),
        name="outproj_ffn_sample",
    )(x, *os_, *ls_, sga, mb, *weights)


def _t5_bucket(dist):
    n = np.asarray(dist)
    max_exact = N_BUCKETS // 2
    large = max_exact + (np.log(np.maximum(n, 1) / max_exact) / np.log(MAX_DISTANCE / max_exact)
                         * (N_BUCKETS - max_exact)).astype(np.int32)
    large = np.minimum(large, N_BUCKETS - 1)
    return np.where(n < max_exact, n, large).astype(np.int32)


def _bias_table(rel_bias, g, dist):
    attended = dist >= 0
    bucket = _t5_bucket(np.maximum(dist, 0)).reshape(1, -1)
    onehot = (bucket == np.arange(N_BUCKETS)[:, None]) & attended.reshape(1, -1)
    heads = rel_bias[:, g * HEADS_PER_GROUP:(g + 1) * HEADS_PER_GROUP].astype(F32).T * LOG2E
    tab = jnp.dot(heads, onehot.astype(np.float32), precision=lax.Precision.HIGHEST)
    mask = np.tile(np.where(attended, 0.0, NEG).astype(np.float32), (HEADS_PER_GROUP, 1))
    return tab.reshape(HEADS_PER_GROUP * dist.shape[0], dist.shape[1]) + mask


def _strides_to_dist(strides, dil):
    return np.where((strides >= 0) & (strides <= KEYS_PER_QUERY), strides * dil, -1)


def _prompt_table(rel_bias, g):
    a = np.arange(BLOCK)[:, None]
    c = np.arange(2 * BLOCK)[None, :]
    tab = _bias_table(rel_bias, g, _strides_to_dist(BLOCK + a - c, GROUPS[g][1]))
    return jnp.stack([jnp.where(c >= BLOCK, tab, NEG), tab])


def _sample_tables(rel_bias, g):
    win, dil = GROUPS[g]
    t = np.arange(DEC_T)[:, None]

    def dist(back):
        return np.where(back % dil == 0, _strides_to_dist(back // dil, dil), -1)

    new = np.arange(NEW_PAD)[None, :]
    tn_dist = np.where(new < DEC_T, dist(t - new), -1)
    return (_bias_table(rel_bias, g, dist(win + t - np.arange(win)[None, :])),
            _bias_table(rel_bias, g, tn_dist))


def _cache_transposed(cache):
    bd, wb = cache.shape[:2]
    return jnp.transpose(cache, (0, 2, 3, 4, 1)).reshape(bd, 2, GROUP_W, wb)


def _new_cache_rows(kv, bd):
    t = kv.shape[0] // bd
    kvt = jnp.transpose(kv.reshape(bd, t, 2 * GROUP_W), (1, 2, 0))
    kvt = kvt.reshape(t, 2, HEADS_PER_GROUP, HEAD_DIM, bd)
    return jnp.transpose(kvt, (4, 0, 1, 2, 3))[None]


def _cache_rows(kvt):
    b, _, _, win = kvt.shape
    kvt = kvt.reshape(b, 2, HEADS_PER_GROUP, HEAD_DIM, win)
    return jnp.transpose(kvt, (0, 4, 1, 2, 3))[None]


TM_INPROJ = 512
TM_OUTPROJ = 512
TQ_ATTN = 4096


def kernel(x_prompt, x_sample, cache_kv_g0, cache_kv_g1, cache_kv_g2, state_conv, w_in, w_att_out,
           w_conv_out, w_o, conv_w, conv_b, rel_bias, norm_mix, norm_mlp, w_ff1, w_ff2, norm_final):
    assert w_in.shape[0] == 1, "one layer"
    b, s, _ = x_prompt.shape
    bd, t, _ = x_sample.shape
    assert t == DEC_T
    wp = w_in[0].astype(BF16)
    wco = w_conv_out[0].astype(BF16)
    wao, wo = w_att_out[0].astype(BF16), w_o[0].astype(BF16)
    w1, w2 = w_ff1[0].astype(BF16), w_ff2[0].astype(BF16)
    nmix, nmlp, nfin = norm_mix[0][None], norm_mlp[0][None], norm_final[None]
    cw, cb = conv_w[0], conv_b[0][None]

    weights = (wao, wo, w1, w2, nmlp, nfin)

    n = bd * t
    st = state_conv[0]
    p0 = jnp.repeat(st[:, 0], t, axis=0)
    p1 = jnp.repeat(st[:, 1], t, axis=0)
    qkv_s, sga_s, mb_s, u_s = _inproj_sample(
        x_sample.reshape(n, D_MODEL), wp, wco, nmix, cw, cb, p0, p1, min(TM_INPROJ, n))
    tabs = [_sample_tables(rel_bias, g) for g in range(N_GROUPS)]
    caches_t = [_cache_transposed(c[0]) for c in (cache_kv_g0, cache_kv_g1, cache_kv_g2)]

    (qkv0, qkv1, qkv2, kvt0, kvt1, kvt2, sga, mb, utail, *sample_att) = _inproj_prompt(
        x_prompt, wp, wco, nmix, cw, cb, qkv_s, caches_t, [tb[0] for tb in tabs],
        jnp.stack([tb[1] for tb in tabs]), TM_INPROJ)
    os_s, ls_s = sample_att[:N_GROUPS], sample_att[N_GROUPS:]
    os_, ls_ = [], []
    for g, qkv in enumerate((qkv0, qkv1, qkv2)):
        o, lse = _attn_prompt(qkv, _prompt_table(rel_bias, g), TQ_ATTN, f"attn_prompt_g{g}")
        os_.append(o)
        ls_.append(lse)
    y_prompt = _outproj_prompt(x_prompt, os_, ls_, sga, mb, weights, TM_OUTPROJ)
    kv_prompt = [_cache_rows(kvt) for kvt in (kvt0, kvt1, kvt2)]
    conv_prompt = utail[:, SUBLANES - 2:][None]

    y_sample = _outproj_sample(x_sample.reshape(n, D_MODEL), os_s, ls_s, sga_s, mb_s, weights,
                               min(TM_OUTPROJ, n))
    kv_sample = [_new_cache_rows(qkv_s[:, g * QKV_W + GROUP_W:(g + 1) * QKV_W], bd)
                 for g in range(N_GROUPS)]
    conv_sample = u_s.reshape(bd, t, CONV_W)[:, t - 2:][None]

    return (y_prompt, y_sample.reshape(bd, t, D_MODEL),
            kv_prompt[0], kv_prompt[1], kv_prompt[2], conv_prompt,
            kv_sample[0], kv_sample[1], kv_sample[2], conv_sample)
```

```python
import functools

import jax
import jax.numpy as jnp
import numpy as np
from jax import lax
from jax.experimental import pallas as pl
from jax.experimental.pallas import tpu as pltpu

D_MODEL = 1024
HEAD_DIM = 64
HEADS_PER_GROUP = 4
GROUPS = ((128, 1), (512, 4), (2048, 16))
N_GROUPS = 3
GROUP_W = HEADS_PER_GROUP * HEAD_DIM
QKV_W = 3 * GROUP_W
ATT_W = N_GROUPS * GROUP_W
CONV_W = D_MODEL // 2
D_FF = 4 * D_MODEL
N_BUCKETS = 32
MAX_DISTANCE = 2048
KEYS_PER_QUERY = 128
BLOCK = 128
LANES = 128
SUBLANES = 8
EPS = 1e-6
SCALE = HEAD_DIM ** -0.5
LOG2E = float(np.log2(np.e))
NEG = -1e30

OFF_CONV = 3 * ATT_W
OFF_GA = OFF_CONV + 3 * CONV_W
OFF_GB = OFF_GA + D_MODEL
PROJ_W = OFF_GB + D_MODEL

VMEM_LIMIT_V7X = 56 * 1024 * 1024
F32 = jnp.float32
BF16 = jnp.bfloat16
_NT = (((1,), (1,)), ((), ()))


def _params(n_axes):
    return pltpu.CompilerParams(
        dimension_semantics=("arbitrary",) * n_axes, vmem_limit_bytes=VMEM_LIMIT_V7X)


def _resident(shape):
    return pl.BlockSpec(shape, lambda *_: (0,) * len(shape), pipeline_mode=pl.Buffered(1))


def _rmsnorm(x, g):
    y = x * lax.rsqrt(jnp.mean(x * x, axis=-1, keepdims=True) + EPS)
    return y * g


def _head_of_lane(shape):
    return lax.broadcasted_iota(jnp.int32, shape, len(shape) - 1) // HEAD_DIM


def _per_head_rows(x):
    head = _head_of_lane(x.shape)
    return jnp.concatenate([jnp.where(head == h, x, 0.0) for h in range(HEADS_PER_GROUP)], axis=0)


def _head_diagonal(x, t):
    head = _head_of_lane((t, GROUP_W))
    out = x[0:t, :]
    for h in range(1, HEADS_PER_GROUP):
        out = jnp.where(head == h, x[h * t:(h + 1) * t, :], out)
    return out


MAX_SUBLANE_STRIDE = 4


def _rows_by_residue(slab_ref, stage_ref, dil):
    t = slab_ref.shape[0]
    if dil <= MAX_SUBLANE_STRIDE:
        return [(r, slab_ref[pl.ds(r, t // dil, stride=dil), :]) for r in range(dil)]
    s1, s2 = MAX_SUBLANE_STRIDE, dil // MAX_SUBLANE_STRIDE
    assert s1 * s2 == dil and s2 <= MAX_SUBLANE_STRIDE
    for r1 in range(s1):
        stage_ref[pl.ds(r1 * (t // s1), t // s1), :] = slab_ref[pl.ds(r1, t // s1, stride=s1), :]
    out = {}
    for r1 in range(s1):
        for k in range(s2):
            out[s1 * k + r1] = stage_ref[pl.ds(r1 * (t // s1) + k, t // dil, stride=s2), :]
    return sorted(out.items())


def _inproj_body(*refs, sample):
    if sample:
        (x_ref, win_ref, wco_ref, nm_ref, cw_ref, cb_ref, p0_ref, p1_ref,
         qkv_ref, sga_ref, mb_ref, u_ref) = refs
    else:
        (x_ref, win_ref, wco_ref, nm_ref, cw_ref, cb_ref,
         sqkv_ref, c0_ref, c1_ref, c2_ref, tc0_ref, tc1_ref, tc2_ref, tn_ref,
         qkv0_ref, qkv1_ref, qkv2_ref, kvt0_ref, kvt1_ref, kvt2_ref, sga_ref, mb_ref, u_ref,
         so0_ref, so1_ref, so2_ref, sl0_ref, sl1_ref, sl2_ref,
         carry_ref, perm_ref, stage_ref) = refs
        qkv_refs = (qkv0_ref, qkv1_ref, qkv2_ref)
        kvt_refs = (kvt0_ref, kvt1_ref, kvt2_ref)
        cache_refs = (c0_ref, c1_ref, c2_ref)
    if not sample:
        sample_scores = _sample_scores(sqkv_ref, cache_refs, (tc0_ref, tc1_ref, tc2_ref), tn_ref)
    x = x_ref[...]
    tm = x.shape[0]
    xn = _rmsnorm(x, nm_ref[...]).astype(BF16)

    def proj(lo, hi):
        return jnp.dot(xn, win_ref[:, lo:hi], preferred_element_type=F32)

    pc = proj(OFF_CONV, OFF_GA)
    ga = proj(OFF_GA, OFF_GB)
    sga_ref[...] = jax.nn.sigmoid(ga).astype(sga_ref.dtype)
    sgb = jax.nn.sigmoid(proj(OFF_GB, PROJ_W))
    if not sample:
        _sample_outputs(sample_scores, sqkv_ref, cache_refs, (so0_ref, so1_ref, so2_ref),
                        (sl0_ref, sl1_ref, sl2_ref))
    dilated = {which: proj(which * ATT_W + GROUP_W, (which + 1) * ATT_W) for which in (0, 1)}
    cb, cc, ch = pc[:, :CONV_W], pc[:, CONV_W:2 * CONV_W], pc[:, 2 * CONV_W:]
    u = cc * ch
    row = lax.broadcasted_iota(jnp.int32, (tm, 1), 0)
    if sample:
        p0, p1 = p0_ref[...], p1_ref[...]
        rs = jnp.bitwise_and(row, DEC_T - 1)
        u_ref[...] = u
    else:
        seq_start = pl.program_id(1) == 0
        p0 = jnp.where(seq_start, 0.0, carry_ref[SUBLANES - 2:SUBLANES - 1, :])
        p1 = jnp.where(seq_start, 0.0, carry_ref[SUBLANES - 1:SUBLANES, :])
        rs = row
        u_ref[...] = u[tm - SUBLANES:, :]
    um1 = jnp.where(rs == 0, p1, pltpu.roll(u, 1, 0))
    um2 = jnp.where(rs == 0, p0, jnp.where(rs == 1, p1, pltpu.roll(u, 2, 0)))
    if not sample:
        carry_ref[...] = u[tm - SUBLANES:, :]
    z = cb_ref[...] + cw_ref[0:1, :] * um2 + cw_ref[1:2, :] * um1 + cw_ref[2:3, :] * u
    branch_b = jnp.dot((cb * z).astype(BF16), wco_ref[...], preferred_element_type=F32)
    mb_ref[...] = (sgb * branch_b).astype(mb_ref.dtype)
    dilated[2] = proj(2 * ATT_W + GROUP_W, 3 * ATT_W)

    slab = 0
    for g, (_, dil) in reversed(list(enumerate(GROUPS))):
        for which in (1, 2, 0):
            if dil == 1:
                p = proj(which * ATT_W, which * ATT_W + GROUP_W)
            else:
                p = dilated[which][:, (g - 1) * GROUP_W:g * GROUP_W]
            if which == 0:
                p = p * (SCALE * LOG2E)
            elif not sample:
                keep = kvt_refs[g].shape[-1]
                kvt_refs[g][which - 1] = p[tm - keep:, :].T
            dst = which * GROUP_W
            if sample:
                qkv_ref[:, g * QKV_W + dst:g * QKV_W + dst + GROUP_W] = p
            elif dil == 1:
                qkv_refs[g][0, :, dst:dst + GROUP_W] = p.astype(BF16)
            else:
                words = pltpu.bitcast(p.astype(BF16), jnp.uint32)
                for half in range(GROUP_W // LANES):
                    perm_ref[slab] = words[:, half * LANES:(half + 1) * LANES]
                    lo = dst + half * LANES
                    for j, w in _rows_by_residue(perm_ref.at[slab], stage_ref.at[slab], dil // 2):
                        even = pltpu.bitcast(w << 16, F32)
                        odd = pltpu.bitcast(w & jnp.uint32(0xFFFF0000), F32)
                        qkv_refs[g][2 * j, :, lo:lo + LANES] = even.astype(BF16)
                        qkv_refs[g][2 * j + 1, :, lo:lo + LANES] = odd.astype(BF16)
                    slab += 1


def _inproj_prompt(x, w_in, w_conv_out, norm_mix, conv_w, conv_b, sample_qkv, caches_t, tcs, tn, tm):
    b, s, _ = x.shape
    nt = s // tm
    n_sample = sample_qkv.shape[0]
    seqs = n_sample // DEC_T // (b * nt)
    assert seqs >= 1 and seqs * b * nt * DEC_T == n_sample
    seq_block = lambda bi, i: bi * nt + i
    sample_in = (
        [pl.BlockSpec((seqs * DEC_T, N_GROUPS * QKV_W), lambda bi, i: (seq_block(bi, i), 0))]
        + [pl.BlockSpec((seqs,) + c.shape[1:], lambda bi, i: (seq_block(bi, i), 0, 0, 0))
           for c in caches_t]
        + [_resident(tc.shape) for tc in tcs] + [_resident(tn.shape)])
    sample_out_spec = pl.BlockSpec((seqs * DEC_T, GROUP_W), lambda bi, i: (seq_block(bi, i), 0))
    sample_out = jax.ShapeDtypeStruct((n_sample, GROUP_W), F32)
    tok = lambda w: pl.BlockSpec((None, tm, w), lambda bi, i: (bi, i, 0))
    qkv_specs = [pl.BlockSpec((None, dil, tm // dil, QKV_W), lambda bi, i: (bi, 0, i, 0))
                 for _, dil in GROUPS]
    n_slabs = sum(3 * GROUP_W // LANES for _, dil in GROUPS if dil > 1)

    def kvt_spec(win):
        keep = min(win, tm)
        first = (s - win) // tm
        return pl.BlockSpec((None, 2, GROUP_W, keep),
                            lambda bi, i: (bi, 0, 0, jnp.maximum(i - first, 0)))

    out_shape = (
        [jax.ShapeDtypeStruct((b, dil, s // dil, QKV_W), BF16) for _, dil in GROUPS]
        + [jax.ShapeDtypeStruct((b, 2, GROUP_W, win), F32) for win, _ in GROUPS]
        + [jax.ShapeDtypeStruct((b, s, D_MODEL), BF16),
           jax.ShapeDtypeStruct((b, s, D_MODEL), BF16),
           jax.ShapeDtypeStruct((b, SUBLANES, CONV_W), F32)]
        + [sample_out] * (2 * N_GROUPS))
    return pl.pallas_call(
        functools.partial(_inproj_body, sample=False),
        grid=(b, nt),
        in_specs=[tok(D_MODEL), _resident(w_in.shape), _resident(w_conv_out.shape),
                  _resident(norm_mix.shape), _resident(conv_w.shape), _resident(conv_b.shape)]
        + sample_in,
        out_specs=qkv_specs + [kvt_spec(win) for win, _ in GROUPS]
        + [tok(D_MODEL), tok(D_MODEL),
           pl.BlockSpec((None, SUBLANES, CONV_W), lambda bi, i: (bi, 0, 0))]
        + [sample_out_spec] * (2 * N_GROUPS),
        out_shape=out_shape,
        scratch_shapes=[pltpu.VMEM((SUBLANES, CONV_W), F32),
                        pltpu.VMEM((n_slabs, tm // 2, LANES), jnp.uint32),
                        pltpu.VMEM((n_slabs, tm // 2, LANES), jnp.uint32)],
        compiler_params=_params(2),
        name="inproj_prompt",
    )(x, w_in, w_conv_out, norm_mix, conv_w, conv_b, sample_qkv, *caches_t, *tcs, tn)


def _inproj_sample(x, w_in, w_conv_out, norm_mix, conv_w, conv_b, p0, p1, tm):
    n = x.shape[0]
    tok = lambda w: pl.BlockSpec((tm, w), lambda i: (i, 0))
    out_shape = (
        jax.ShapeDtypeStruct((n, N_GROUPS * QKV_W), F32),
        jax.ShapeDtypeStruct((n, D_MODEL), BF16),
        jax.ShapeDtypeStruct((n, D_MODEL), BF16),
        jax.ShapeDtypeStruct((n, CONV_W), F32),
    )
    return pl.pallas_call(
        functools.partial(_inproj_body, sample=True),
        grid=(n // tm,),
        in_specs=[tok(D_MODEL), _resident(w_in.shape), _resident(w_conv_out.shape),
                  _resident(norm_mix.shape), _resident(conv_w.shape), _resident(conv_b.shape),
                  tok(CONV_W), tok(CONV_W)],
        out_specs=[tok(N_GROUPS * QKV_W), tok(D_MODEL), tok(D_MODEL), tok(CONV_W)],
        out_shape=out_shape,
        compiler_params=_params(1),
        name="inproj_sample",
    )(x, w_in, w_conv_out, norm_mix, conv_w, conv_b, p0, p1)


def _attn_prompt_body(q_ref, kc_ref, kp_ref, vc_ref, vp_ref, tab_ref, o_ref, l_ref):
    i = pl.program_id(2)
    first_tile = jnp.where(i == 0, 0, 1)
    head = _head_of_lane((BLOCK, GROUP_W))
    low_head = lax.broadcasted_iota(jnp.int32, (BLOCK, LANES), 1) < HEAD_DIM
    ones = jnp.ones((2 * BLOCK, LANES), BF16)
    n_cls, tq, _ = q_ref.shape
    for c, j in [(c, j) for c in range(n_cls) for j in range(tq // BLOCK)]:
        q = q_ref[c, j * BLOCK:(j + 1) * BLOCK, :]
        qm = jnp.concatenate(
            [jnp.where(head == h, q, jnp.zeros_like(q)) for h in range(HEADS_PER_GROUP)], axis=0)
        if j == 0:
            k = jnp.concatenate([kp_ref[c], kc_ref[c, 0:BLOCK, :]], axis=0)
            v = jnp.concatenate([vp_ref[c], vc_ref[c, 0:BLOCK, :]], axis=0)
            tab = tab_ref[first_tile]
        else:
            k = kc_ref[c, (j - 1) * BLOCK:(j + 1) * BLOCK, :]
            v = vc_ref[c, (j - 1) * BLOCK:(j + 1) * BLOCK, :]
            tab = tab_ref[1]
        s = lax.dot_general(qm, k, _NT, preferred_element_type=F32) + tab
        m = jnp.max(s, axis=-1, keepdims=True)
        p = jnp.exp2(s - m).astype(BF16)
        halves_o, halves_l = [], []
        for pair in range(HEADS_PER_GROUP // 2):
            w = jnp.concatenate([v[:, pair * LANES:(pair + 1) * LANES], ones], axis=1)
            r = jnp.dot(p[2 * pair * BLOCK:2 * (pair + 1) * BLOCK, :], w,
                        preferred_element_type=F32)
            m0 = jnp.broadcast_to(m[2 * pair * BLOCK:(2 * pair + 1) * BLOCK], (BLOCK, LANES))
            m1 = jnp.broadcast_to(m[(2 * pair + 1) * BLOCK:2 * (pair + 1) * BLOCK], (BLOCK, LANES))
            acc = jnp.where(low_head, r[:BLOCK, :LANES], r[BLOCK:, :LANES])
            l = jnp.where(low_head, r[:BLOCK, LANES:], r[BLOCK:, LANES:])
            halves_o.append(acc / l)
            halves_l.append(jnp.where(low_head, m0, m1) + jnp.log2(l))
        o_ref[c, j * BLOCK:(j + 1) * BLOCK, :] = (
            jnp.concatenate(halves_o, axis=1).astype(o_ref.dtype))
        l_ref[c, j * BLOCK:(j + 1) * BLOCK, :] = jnp.concatenate(halves_l, axis=1)


def _attn_prompt(qkv, tab, rows_per_step, name):
    b, dil, L, _ = qkv.shape
    tq = min(rows_per_step, L)
    n_cls = min(rows_per_step // tq, dil)
    cur_spec = lambda which: pl.BlockSpec(
        (None, n_cls, tq, GROUP_W), lambda bi, r, i: (bi, r, i, which))
    prev_spec = lambda which: pl.BlockSpec(
        (None, n_cls, BLOCK, GROUP_W),
        lambda bi, r, i: (bi, r, jnp.maximum(i * (tq // BLOCK) - 1, 0), which))
    out_spec = pl.BlockSpec((None, n_cls, tq, GROUP_W), lambda bi, r, i: (bi, r, i, 0))
    return pl.pallas_call(
        _attn_prompt_body,
        grid=(b, dil // n_cls, L // tq),
        in_specs=[cur_spec(0), cur_spec(1), prev_spec(1), cur_spec(2), prev_spec(2),
                  _resident(tab.shape)],
        out_specs=[out_spec, out_spec],
        out_shape=(jax.ShapeDtypeStruct((b, dil, L, GROUP_W), BF16),
                   jax.ShapeDtypeStruct((b, dil, L, GROUP_W), F32)),
        compiler_params=_params(3),
        name=name,
    )(qkv, qkv, qkv, qkv, qkv, tab)


DEC_T = 8
NEW_PAD = 16


def _sample_scores(qkv_ref, cache_refs, tc_refs, tn_ref):
    pad = jnp.zeros((NEW_PAD - DEC_T, GROUP_W), F32)
    scores = []
    for i in range(cache_refs[0].shape[0]):
        rows = slice(i * DEC_T, (i + 1) * DEC_T)
        for g in range(N_GROUPS):
            base = g * QKV_W
            q = qkv_ref[rows, base:base + GROUP_W]
            kn = jnp.concatenate([qkv_ref[rows, base + GROUP_W:base + 2 * GROUP_W], pad], axis=0)
            qm = _per_head_rows(q).astype(BF16)
            s_n = (lax.dot_general(qm, kn.astype(BF16), _NT, preferred_element_type=F32)
                   + tn_ref[g])
            s_c = (jnp.dot(qm, cache_refs[g][i, 0].astype(BF16), preferred_element_type=F32)
                   + tc_refs[g][...])
            scores.append((s_c, s_n))
    return scores


def _sample_outputs(scores, qkv_ref, cache_refs, o_refs, l_refs):
    pad = jnp.zeros((NEW_PAD - DEC_T, GROUP_W), F32)
    scores = iter(scores)
    for i in range(cache_refs[0].shape[0]):
        rows = slice(i * DEC_T, (i + 1) * DEC_T)
        for g in range(N_GROUPS):
            s_c, s_n = next(scores)
            base = g * QKV_W
            vn = jnp.concatenate([qkv_ref[rows, base + 2 * GROUP_W:base + QKV_W], pad], axis=0)
            m = jnp.maximum(jnp.max(s_c, axis=-1, keepdims=True),
                            jnp.max(s_n, axis=-1, keepdims=True))
            p_c = jnp.exp2(s_c - m)
            p_n = jnp.exp2(s_n - m)
            l = jnp.sum(p_c, axis=-1, keepdims=True) + jnp.sum(p_n, axis=-1, keepdims=True)
            acc = jnp.dot(p_n.astype(BF16), vn.astype(BF16), preferred_element_type=F32)
            acc = acc + lax.dot_general(p_c.astype(BF16), cache_refs[g][i, 1].astype(BF16), _NT,
                                        preferred_element_type=F32)
            res = acc / l
            lse = jnp.broadcast_to(m + jnp.log2(l), res.shape)
            o_refs[g][rows, :] = _head_diagonal(res, DEC_T)
            l_refs[g][rows, :] = _head_diagonal(lse, DEC_T)


FF_CHUNK = 1024
DOWN_ROW_GROUPS = 2


def _natural_rows(ref, natural_ref, stage_ref):
    dil, rows, _ = ref.shape
    if dil == 1:
        return ref[0].astype(F32)
    t = dil * rows
    halves = []
    for half in range(GROUP_W // LANES):
        piece = lambda r: ref[r, :, half * LANES:(half + 1) * LANES].astype(F32)
        if dil <= MAX_SUBLANE_STRIDE:
            for r in range(dil):
                natural_ref[half, pl.ds(r, rows, stride=dil), :] = piece(r)
        else:
            s1, s2 = MAX_SUBLANE_STRIDE, dil // MAX_SUBLANE_STRIDE
            assert s1 * s2 == dil and s2 <= MAX_SUBLANE_STRIDE
            for r1 in range(s1):
                for k in range(s2):
                    stage_ref[half, pl.ds(r1 * (t // s1) + k, rows, stride=s2), :] = (
                        piece(s1 * k + r1))
            for r1 in range(s1):
                natural_ref[half, pl.ds(r1, t // s1, stride=s1), :] = (
                    stage_ref[half, pl.ds(r1 * (t // s1), t // s1), :])
        halves.append(natural_ref[half])
    return jnp.concatenate(halves, axis=1)


def _merge_groups(os_, ls_):
    (o0, o1, o2), (l0, l1, l2) = os_, ls_
    m = jnp.maximum(l0, jnp.maximum(l1, l2))
    e0, e1, e2 = jnp.exp2(l0 - m), jnp.exp2(l1 - m), jnp.exp2(l2 - m)
    return ((e0 * o0 + e1 * o1 + e2 * o2) / (e0 + e1 + e2)).astype(BF16)


def _outproj_body(x_ref, o0_ref, o1_ref, o2_ref, l0_ref, l1_ref, l2_ref, sga_ref, mb_ref,
                  wao_ref, wo_ref, w1_ref, w2_ref, nmlp_ref, nfin_ref, y_ref, *scratch):
    if scratch:
        natural_ref, stage_ref = scratch
        os_ = [_natural_rows(ref, natural_ref.at[k], stage_ref.at[k]) for k, ref in
               enumerate((o0_ref, o1_ref, o2_ref))]
        ls_ = [_natural_rows(ref, natural_ref.at[N_GROUPS + k], stage_ref.at[N_GROUPS + k])
               for k, ref in enumerate((l0_ref, l1_ref, l2_ref))]
    else:
        os_ = [ref[...] for ref in (o0_ref, o1_ref, o2_ref)]
        ls_ = [ref[...] for ref in (l0_ref, l1_ref, l2_ref)]
    branch_a = jnp.dot(_merge_groups(os_, ls_), wao_ref[...], preferred_element_type=F32)
    mix = sga_ref[...].astype(F32) * branch_a + mb_ref[...].astype(F32)
    x1 = x_ref[...] + jnp.dot(mix.astype(BF16), wo_ref[...], preferred_element_type=F32)
    h = _rmsnorm(x1, nmlp_ref[...]).astype(BF16)
    acts = []
    for c in range(D_FF // FF_CHUNK):
        a = jnp.dot(h, w1_ref[:, c * FF_CHUNK:(c + 1) * FF_CHUNK], preferred_element_type=F32)
        acts.append(jnp.square(jnp.maximum(a, 0.0)).astype(BF16))
    rows = x1.shape[0] // DOWN_ROW_GROUPS
    for k in range(DOWN_ROW_GROUPS):
        r = slice(k * rows, (k + 1) * rows)
        acc = x1[r]
        for c, a in enumerate(acts):
            acc = acc + jnp.dot(a[r], w2_ref[c * FF_CHUNK:(c + 1) * FF_CHUNK, :],
                                preferred_element_type=F32)
        y_ref[r, :] = _rmsnorm(acc, nfin_ref[...])


def _outproj_prompt(x, os_, ls_, sga, mb, weights, tm):
    b, s, _ = x.shape
    tok = pl.BlockSpec((None, tm, D_MODEL), lambda bi, i: (bi, i, 0))
    grouped = [pl.BlockSpec((None, dil, tm // dil, GROUP_W), lambda bi, i: (bi, 0, i, 0))
               for _, dil in GROUPS]
    return pl.pallas_call(
        _outproj_body,
        grid=(b, s // tm),
        in_specs=[tok] + grouped + grouped + [tok, tok] + [_resident(w.shape) for w in weights],
        out_specs=tok,
        out_shape=jax.ShapeDtypeStruct((b, s, D_MODEL), F32),
        scratch_shapes=[pltpu.VMEM((2 * N_GROUPS, GROUP_W // LANES, tm, LANES), F32)] * 2,
        compiler_params=_params(2),
        name="outproj_ffn_prompt",
    )(x, *os_, *ls_, sga, mb, *weights)


def _outproj_sample(x, os_, ls_, sga, mb, weights, tm):
    n = x.shape[0]
    tok = lambda w: pl.BlockSpec((tm, w), lambda i: (i, 0))
    return pl.pallas_call(
        _outproj_body,
        grid=(n // tm,),
        in_specs=[tok(D_MODEL)] + [tok(GROUP_W)] * 6 + [tok(D_MODEL), tok(D_MODEL)]
        + [_resident(w.shape) for w in weights],
        out_specs=tok(D_MODEL),
        out_shape=jax.ShapeDtypeStruct((n, D_MODEL), F32),
        compiler_params=_params(1),
        name="outproj_ffn_sample",
    )(x, *os_, *ls_, sga, mb, *weights)


def _t5_bucket(dist):
    n = np.asarray(dist)
    max_exact = N_BUCKETS // 2
    large = max_exact + (np.log(np.maximum(n, 1) / max_exact) / np.log(MAX_DISTANCE / max_exact)
                         * (N_BUCKETS - max_exact)).astype(np.int32)
    large = np.minimum(large, N_BUCKETS - 1)
    return np.where(n < max_exact, n, large).astype(np.int32)


def _bias_table(rel_bias, g, dist):
    attended = dist >= 0
    bucket = _t5_bucket(np.maximum(dist, 0)).reshape(1, -1)
    onehot = (bucket == np.arange(N_BUCKETS)[:, None]) & attended.reshape(1, -1)
    heads = rel_bias[:, g * HEADS_PER_GROUP:(g + 1) * HEADS_PER_GROUP].astype(F32).T * LOG2E
    tab = jnp.dot(heads, onehot.astype(np.float32), precision=lax.Precision.HIGHEST)
    mask = np.tile(np.where(attended, 0.0, NEG).astype(np.float32), (HEADS_PER_GROUP, 1))
    return tab.reshape(HEADS_PER_GROUP * dist.shape[0], dist.shape[1]) + mask


def _strides_to_dist(strides, dil):
    return np.where((strides >= 0) & (strides <= KEYS_PER_QUERY), strides * dil, -1)


def _prompt_table(rel_bias, g):
    a = np.arange(BLOCK)[:, None]
    c = np.arange(2 * BLOCK)[None, :]
    tab = _bias_table(rel_bias, g, _strides_to_dist(BLOCK + a - c, GROUPS[g][1]))
    return jnp.stack([jnp.where(c >= BLOCK, tab, NEG), tab])


def _sample_tables(rel_bias, g):
    win, dil = GROUPS[g]
    t = np.arange(DEC_T)[:, None]

    def dist(back):
        return np.where(back % dil == 0, _strides_to_dist(back // dil, dil), -1)

    new = np.arange(NEW_PAD)[None, :]
    tn_dist = np.where(new < DEC_T, dist(t - new), -1)
    return (_bias_table(rel_bias, g, dist(win + t - np.arange(win)[None, :])),
            _bias_table(rel_bias, g, tn_dist))


def _cache_transposed(cache):
    bd, wb = cache.shape[:2]
    return jnp.transpose(cache, (0, 2, 3, 4, 1)).reshape(bd, 2, GROUP_W, wb)


def _new_cache_rows(kv, bd):
    t = kv.shape[0] // bd
    kvt = jnp.transpose(kv.reshape(bd, t, 2 * GROUP_W), (1, 2, 0))
    kvt = kvt.reshape(t, 2, HEADS_PER_GROUP, HEAD_DIM, bd)
    return jnp.transpose(kvt, (4, 0, 1, 2, 3))[None]


def _cache_rows(kvt):
    b, _, _, win = kvt.shape
    kvt = kvt.reshape(b, 2, HEADS_PER_GROUP, HEAD_DIM, win)
    return jnp.transpose(kvt, (0, 4, 1, 2, 3))[None]


TM_INPROJ = 512
TM_OUTPROJ = 512
TQ_ATTN = 4096


def kernel(x_prompt, x_sample, cache_kv_g0, cache_kv_g1, cache_kv_g2, state_conv, w_in, w_att_out,
           w_conv_out, w_o, conv_w, conv_b, rel_bias, norm_mix, norm_mlp, w_ff1, w_ff2, norm_final):
    assert w_in.shape[0] == 1, "one layer"
    b, s, _ = x_prompt.shape
    bd, t, _ = x_sample.shape
    assert t == DEC_T
    wp = w_in[0].astype(BF16)
    wco = w_conv_out[0].astype(BF16)
    wao, wo = w_att_out[0].astype(BF16), w_o[0].astype(BF16)
    w1, w2 = w_ff1[0].astype(BF16), w_ff2[0].astype(BF16)
    nmix, nmlp, nfin = norm_mix[0][None], norm_mlp[0][None], norm_final[None]
    cw, cb = conv_w[0], conv_b[0][None]

    weights = (wao, wo, w1, w2, nmlp, nfin)

    n = bd * t
    st = state_conv[0]
    p0 = jnp.repeat(st[:, 0], t, axis=0)
    p1 = jnp.repeat(st[:, 1], t, axis=0)
    qkv_s, sga_s, mb_s, u_s = _inproj_sample(
        x_sample.reshape(n, D_MODEL), wp, wco, nmix, cw, cb, p0, p1, min(TM_INPROJ, n))
    tabs = [_sample_tables(rel_bias, g) for g in range(N_GROUPS)]
    caches_t = [_cache_transposed(c[0]) for c in (cache_kv_g0, cache_kv_g1, cache_kv_g2)]

    (qkv0, qkv1, qkv2, kvt0, kvt1, kvt2, sga, mb, utail, *sample_att) = _inproj_prompt(
        x_prompt, wp, wco, nmix, cw, cb, qkv_s, caches_t, [tb[0] for tb in tabs],
        jnp.stack([tb[1] for tb in tabs]), TM_INPROJ)
    os_s, ls_s = sample_att[:N_GROUPS], sample_att[N_GROUPS:]
    os_, ls_ = [], []
    for g, qkv in enumerate((qkv0, qkv1, qkv2)):
        o, lse = _attn_prompt(qkv, _prompt_table(rel_bias, g), TQ_ATTN, f"attn_prompt_g{g}")
        os_.append(o)
        ls_.append(lse)
    y_prompt = _outproj_prompt(x_prompt, os_, ls_, sga, mb, weights, TM_OUTPROJ)
    kv_prompt = [_cache_rows(kvt) for kvt in (kvt0, kvt1, kvt2)]
    conv_prompt = utail[:, SUBLANES - 2:][None]

    y_sample = _outproj_sample(x_sample.reshape(n, D_MODEL), os_s, ls_s, sga_s, mb_s, weights,
                               min(TM_OUTPROJ, n))
    kv_sample = [_new_cache_rows(qkv_s[:, g * QKV_W + GROUP_W:(g + 1) * QKV_W], bd)
                 for g in range(N_GROUPS)]
    conv_sample = u_s.reshape(bd, t, CONV_W)[:, t - 2:][None]

    return (y_prompt, y_sample.reshape(bd, t, D_MODEL),
            kv_prompt[0], kv_prompt[1], kv_prompt[2], conv_prompt,
            kv_sample[0], kv_sample[1], kv_sample[2], conv_sample)
```

```python
import functools

import jax
import jax.numpy as jnp
import numpy as np
from jax import lax
from jax.experimental import pallas as pl
from jax.experimental.pallas import tpu as pltpu

D_MODEL = 1024
HEAD_DIM = 64
HEADS_PER_GROUP = 4
GROUPS = ((128, 1), (512, 4), (2048, 16))
N_GROUPS = 3
GROUP_W = HEADS_PER_GROUP * HEAD_DIM
QKV_W = 3 * GROUP_W
ATT_W = N_GROUPS * GROUP_W
CONV_W = D_MODEL // 2
D_FF = 4 * D_MODEL
N_BUCKETS = 32
MAX_DISTANCE = 2048
KEYS_PER_QUERY = 128
BLOCK = 128
LANES = 128
SUBLANES = 8
EPS = 1e-6
SCALE = HEAD_DIM ** -0.5
LOG2E = float(np.log2(np.e))
NEG = -1e30

OFF_CONV = 3 * ATT_W
OFF_GA = OFF_CONV + 3 * CONV_W
OFF_GB = OFF_GA + D_MODEL
PROJ_W = OFF_GB + D_MODEL

VMEM_LIMIT_V7X = 56 * 1024 * 1024
F32 = jnp.float32
BF16 = jnp.bfloat16
_NT = (((1,), (1,)), ((), ()))


def _params(n_axes):
    return pltpu.CompilerParams(
        dimension_semantics=("arbitrary",) * n_axes, vmem_limit_bytes=VMEM_LIMIT_V7X)


def _resident(shape):
    return pl.BlockSpec(shape, lambda *_: (0,) * len(shape), pipeline_mode=pl.Buffered(1))


def _rmsnorm(x, g):
    y = x * lax.rsqrt(jnp.mean(x * x, axis=-1, keepdims=True) + EPS)
    return y * g


def _head_of_lane(shape):
    return lax.broadcasted_iota(jnp.int32, shape, len(shape) - 1) // HEAD_DIM


def _per_head_rows(x):
    head = _head_of_lane(x.shape)
    return jnp.concatenate([jnp.where(head == h, x, 0.0) for h in range(HEADS_PER_GROUP)], axis=0)


def _head_diagonal(x, t):
    head = _head_of_lane((t, GROUP_W))
    out = x[0:t, :]
    for h in range(1, HEADS_PER_GROUP):
        out = jnp.where(head == h, x[h * t:(h + 1) * t, :], out)
    return out


MAX_SUBLANE_STRIDE = 4


def _rows_by_residue(slab_ref, stage_ref, dil):
    t = slab_ref.shape[0]
    if dil <= MAX_SUBLANE_STRIDE:
        return [(r, slab_ref[pl.ds(r, t // dil, stride=dil), :]) for r in range(dil)]
    s1, s2 = MAX_SUBLANE_STRIDE, dil // MAX_SUBLANE_STRIDE
    assert s1 * s2 == dil and s2 <= MAX_SUBLANE_STRIDE
    for r1 in range(s1):
        stage_ref[pl.ds(r1 * (t // s1), t // s1), :] = slab_ref[pl.ds(r1, t // s1, stride=s1), :]
    out = {}
    for r1 in range(s1):
        for k in range(s2):
            out[s1 * k + r1] = stage_ref[pl.ds(r1 * (t // s1) + k, t // dil, stride=s2), :]
    return sorted(out.items())


def _inproj_body(*refs, sample):
    if sample:
        (x_ref, win_ref, wco_ref, nm_ref, cw_ref, cb_ref, p0_ref, p1_ref,
         qkv_ref, sga_ref, mb_ref, u_ref) = refs
    else:
        (x_ref, win_ref, wco_ref, nm_ref, cw_ref, cb_ref,
         sqkv_ref, c0_ref, c1_ref, c2_ref, tc0_ref, tc1_ref, tc2_ref, tn_ref,
         qkv0_ref, qkv1_ref, qkv2_ref, kvt0_ref, kvt1_ref, kvt2_ref, sga_ref, mb_ref, u_ref,
         so0_ref, so1_ref, so2_ref, sl0_ref, sl1_ref, sl2_ref,
         carry_ref, perm_ref, stage_ref) = refs
        qkv_refs = (qkv0_ref, qkv1_ref, qkv2_ref)
        kvt_refs = (kvt0_ref, kvt1_ref, kvt2_ref)
        cache_refs = (c0_ref, c1_ref, c2_ref)
    if not sample:
        sample_scores = _sample_scores(sqkv_ref, cache_refs, (tc0_ref, tc1_ref, tc2_ref), tn_ref)
    x = x_ref[...]
    tm = x.shape[0]
    xn = _rmsnorm(x, nm_ref[...]).astype(BF16)

    def proj(lo, hi):
        return jnp.dot(xn, win_ref[:, lo:hi], preferred_element_type=F32)

    pc = proj(OFF_CONV, OFF_GA)
    ga = proj(OFF_GA, OFF_GB)
    sga_ref[...] = jax.nn.sigmoid(ga).astype(sga_ref.dtype)
    sgb = jax.nn.sigmoid(proj(OFF_GB, PROJ_W))
    if not sample:
        _sample_outputs(sample_scores, sqkv_ref, cache_refs, (so0_ref, so1_ref, so2_ref),
                        (sl0_ref, sl1_ref, sl2_ref))
    dilated = {which: proj(which * ATT_W + GROUP_W, (which + 1) * ATT_W) for which in (0, 1)}
    cb, cc, ch = pc[:, :CONV_W], pc[:, CONV_W:2 * CONV_W], pc[:, 2 * CONV_W:]
    u = cc * ch
    row = lax.broadcasted_iota(jnp.int32, (tm, 1), 0)
    if sample:
        p0, p1 = p0_ref[...], p1_ref[...]
        rs = jnp.bitwise_and(row, DEC_T - 1)
        u_ref[...] = u
    else:
        seq_start = pl.program_id(1) == 0
        p0 = jnp.where(seq_start, 0.0, carry_ref[SUBLANES - 2:SUBLANES - 1, :])
        p1 = jnp.where(seq_start, 0.0, carry_ref[SUBLANES - 1:SUBLANES, :])
        rs = row
        u_ref[...] = u[tm - SUBLANES:, :]
    um1 = jnp.where(rs == 0, p1, pltpu.roll(u, 1, 0))
    um2 = jnp.where(rs == 0, p0, jnp.where(rs == 1, p1, pltpu.roll(u, 2, 0)))
    if not sample:
        carry_ref[...] = u[tm - SUBLANES:, :]
    z = cb_ref[...] + cw_ref[0:1, :] * um2 + cw_ref[1:2, :] * um1 + cw_ref[2:3, :] * u
    branch_b = jnp.dot((cb * z).astype(BF16), wco_ref[...], preferred_element_type=F32)
    mb_ref[...] = (sgb * branch_b).astype(mb_ref.dtype)
    dilated[2] = proj(2 * ATT_W + GROUP_W, 3 * ATT_W)

    slab = 0
    for g, (_, dil) in reversed(list(enumerate(GROUPS))):
        for which in (1, 2, 0):
            if dil == 1:
                p = proj(which * ATT_W, which * ATT_W + GROUP_W)
            else:
                p = dilated[which][:, (g - 1) * GROUP_W:g * GROUP_W]
            if which == 0:
                p = p * (SCALE * LOG2E)
            elif not sample:
                keep = kvt_refs[g].shape[-1]
                kvt_refs[g][which - 1] = p[tm - keep:, :].T
            dst = which * GROUP_W
            if sample:
                qkv_ref[:, g * QKV_W + dst:g * QKV_W + dst + GROUP_W] = p
            elif dil == 1:
                qkv_refs[g][0, :, dst:dst + GROUP_W] = p.astype(BF16)
            else:
                words = pltpu.bitcast(p.astype(BF16), jnp.uint32)
                for half in range(GROUP_W // LANES):
                    perm_ref[slab] = words[:, half * LANES:(half + 1) * LANES]
                    lo = dst + half * LANES
                    for j, w in _rows_by_residue(perm_ref.at[slab], stage_ref.at[slab], dil // 2):
                        even = pltpu.bitcast(w << 16, F32)
                        odd = pltpu.bitcast(w & jnp.uint32(0xFFFF0000), F32)
                        qkv_refs[g][2 * j, :, lo:lo + LANES] = even.astype(BF16)
                        qkv_refs[g][2 * j + 1, :, lo:lo + LANES] = odd.astype(BF16)
                    slab += 1


def _inproj_prompt(x, w_in, w_conv_out, norm_mix, conv_w, conv_b, sample_qkv, caches_t, tcs, tn, tm):
    b, s, _ = x.shape
    nt = s // tm
    n_sample = sample_qkv.shape[0]
    seqs = n_sample // DEC_T // (b * nt)
    assert seqs >= 1 and seqs * b * nt * DEC_T == n_sample
    seq_block = lambda bi, i: bi * nt + i
    sample_in = (
        [pl.BlockSpec((seqs * DEC_T, N_GROUPS * QKV_W), lambda bi, i: (seq_block(bi, i), 0))]
        + [pl.BlockSpec((seqs,) + c.shape[1:], lambda bi, i: (seq_block(bi, i), 0, 0, 0))
           for c in caches_t]
        + [_resident(tc.shape) for tc in tcs] + [_resident(tn.shape)])
    sample_out_spec = pl.BlockSpec((seqs * DEC_T, GROUP_W), lambda bi, i: (seq_block(bi, i), 0))
    sample_out = jax.ShapeDtypeStruct((n_sample, GROUP_W), F32)
    tok = lambda w: pl.BlockSpec((None, tm, w), lambda bi, i: (bi, i, 0))
    qkv_specs = [pl.BlockSpec((None, dil, tm // dil, QKV_W), lambda bi, i: (bi, 0, i, 0))
                 for _, dil in GROUPS]
    n_slabs = sum(3 * GROUP_W // LANES for _, dil in GROUPS if dil > 1)

    def kvt_spec(win):
        keep = min(win, tm)
        first = (s - win) // tm
        return pl.BlockSpec((None, 2, GROUP_W, keep),
                            lambda bi, i: (bi, 0, 0, jnp.maximum(i - first, 0)))

    out_shape = (
        [jax.ShapeDtypeStruct((b, dil, s // dil, QKV_W), BF16) for _, dil in GROUPS]
        + [jax.ShapeDtypeStruct((b, 2, GROUP_W, win), F32) for win, _ in GROUPS]
        + [jax.ShapeDtypeStruct((b, s, D_MODEL), BF16),
           jax.ShapeDtypeStruct((b, s, D_MODEL), BF16),
           jax.ShapeDtypeStruct((b, SUBLANES, CONV_W), F32)]
        + [sample_out] * (2 * N_GROUPS))
    return pl.pallas_call(
        functools.partial(_inproj_body, sample=False),
        grid=(b, nt),
        in_specs=[tok(D_MODEL), _resident(w_in.shape), _resident(w_conv_out.shape),
                  _resident(norm_mix.shape), _resident(conv_w.shape), _resident(conv_b.shape)]
        + sample_in,
        out_specs=qkv_specs + [kvt_spec(win) for win, _ in GROUPS]
        + [tok(D_MODEL), tok(D_MODEL),
           pl.BlockSpec((None, SUBLANES, CONV_W), lambda bi, i: (bi, 0, 0))]
        + [sample_out_spec] * (2 * N_GROUPS),
        out_shape=out_shape,
        scratch_shapes=[pltpu.VMEM((SUBLANES, CONV_W), F32),
                        pltpu.VMEM((n_slabs, tm // 2, LANES), jnp.uint32),
                        pltpu.VMEM((n_slabs, tm // 2, LANES), jnp.uint32)],
        compiler_params=_params(2),
        name="inproj_prompt",
    )(x, w_in, w_conv_out, norm_mix, conv_w, conv_b, sample_qkv, *caches_t, *tcs, tn)


def _inproj_sample(x, w_in, w_conv_out, norm_mix, conv_w, conv_b, p0, p1, tm):
    n = x.shape[0]
    tok = lambda w: pl.BlockSpec((tm, w), lambda i: (i, 0))
    out_shape = (
        jax.ShapeDtypeStruct((n, N_GROUPS * QKV_W), F32),
        jax.ShapeDtypeStruct((n, D_MODEL), BF16),
        jax.ShapeDtypeStruct((n, D_MODEL), BF16),
        jax.ShapeDtypeStruct((n, CONV_W), F32),
    )
    return pl.pallas_call(
        functools.partial(_inproj_body, sample=True),
        grid=(n // tm,),
        in_specs=[tok(D_MODEL), _resident(w_in.shape), _resident(w_conv_out.shape),
                  _resident(norm_mix.shape), _resident(conv_w.shape), _resident(conv_b.shape),
                  tok(CONV_W), tok(CONV_W)],
        out_specs=[tok(N_GROUPS * QKV_W), tok(D_MODEL), tok(D_MODEL), tok(CONV_W)],
        out_shape=out_shape,
        compiler_params=_params(1),
        name="inproj_sample",
    )(x, w_in, w_conv_out, norm_mix, conv_w, conv_b, p0, p1)


def _attn_prompt_body(q_ref, kc_ref, kp_ref, vc_ref, vp_ref, tab_ref, o_ref, l_ref):
    i = pl.program_id(2)
    first_tile = jnp.where(i == 0, 0, 1)
    head = _head_of_lane((BLOCK, GROUP_W))
    low_head = lax.broadcasted_iota(jnp.int32, (BLOCK, LANES), 1) < HEAD_DIM
    ones = jnp.ones((2 * BLOCK, LANES), BF16)
    n_cls, tq, _ = q_ref.shape
    for c, j in [(c, j) for c in range(n_cls) for j in range(tq // BLOCK)]:
        q = q_ref[c, j * BLOCK:(j + 1) * BLOCK, :]
        qm = jnp.concatenate(
            [jnp.where(head == h, q, jnp.zeros_like(q)) for h in range(HEADS_PER_GROUP)], axis=0)
        if j == 0:
            k = jnp.concatenate([kp_ref[c], kc_ref[c, 0:BLOCK, :]], axis=0)
            v = jnp.concatenate([vp_ref[c], vc_ref[c, 0:BLOCK, :]], axis=0)
            tab = tab_ref[first_tile]
        else:
            k = kc_ref[c, (j - 1) * BLOCK:(j + 1) * BLOCK, :]
            v = vc_ref[c, (j - 1) * BLOCK:(j + 1) * BLOCK, :]
            tab = tab_ref[1]
        s = lax.dot_general(qm, k, _NT, preferred_element_type=F32) + tab
        halves_o, halves_l = [], []
        for pair in range(HEADS_PER_GROUP // 2):
            s_pair = s[2 * pair * BLOCK:2 * (pair + 1) * BLOCK, :]
            m = jnp.max(s_pair, axis=-1, keepdims=True)
            p = jnp.exp2(s_pair - m).astype(BF16)
            w = jnp.concatenate([v[:, pair * LANES:(pair + 1) * LANES], ones], axis=1)
            r = jnp.dot(p, w, preferred_element_type=F32)
            m0 = jnp.broadcast_to(m[:BLOCK], (BLOCK, LANES))
            m1 = jnp.broadcast_to(m[BLOCK:], (BLOCK, LANES))
            acc = jnp.where(low_head, r[:BLOCK, :LANES], r[BLOCK:, :LANES])
            l = jnp.where(low_head, r[:BLOCK, LANES:], r[BLOCK:, LANES:])
            halves_o.append(acc / l)
            halves_l.append(jnp.where(low_head, m0, m1) + jnp.log2(l))
        o_ref[c, j * BLOCK:(j + 1) * BLOCK, :] = (
            jnp.concatenate(halves_o, axis=1).astype(o_ref.dtype))
        l_ref[c, j * BLOCK:(j + 1) * BLOCK, :] = jnp.concatenate(halves_l, axis=1)


def _attn_prompt(qkv, tab, rows_per_step, name):
    b, dil, L, _ = qkv.shape
    tq = min(rows_per_step, L)
    n_cls = min(rows_per_step // tq, dil)
    cur_spec = lambda which: pl.BlockSpec(
        (None, n_cls, tq, GROUP_W), lambda bi, r, i: (bi, r, i, which))
    prev_spec = lambda which: pl.BlockSpec(
        (None, n_cls, BLOCK, GROUP_W),
        lambda bi, r, i: (bi, r, jnp.maximum(i * (tq // BLOCK) - 1, 0), which))
    out_spec = pl.BlockSpec((None, n_cls, tq, GROUP_W), lambda bi, r, i: (bi, r, i, 0))
    return pl.pallas_call(
        _attn_prompt_body,
        grid=(b, dil // n_cls, L // tq),
        in_specs=[cur_spec(0), cur_spec(1), prev_spec(1), cur_spec(2), prev_spec(2),
                  _resident(tab.shape)],
        out_specs=[out_spec, out_spec],
        out_shape=(jax.ShapeDtypeStruct((b, dil, L, GROUP_W), BF16),
                   jax.ShapeDtypeStruct((b, dil, L, GROUP_W), F32)),
        compiler_params=_params(3),
        name=name,
    )(qkv, qkv, qkv, qkv, qkv, tab)


DEC_T = 8
NEW_PAD = 16


def _sample_scores(qkv_ref, cache_refs, tc_refs, tn_ref):
    pad = jnp.zeros((NEW_PAD - DEC_T, GROUP_W), F32)
    scores = []
    for i in range(cache_refs[0].shape[0]):
        rows = slice(i * DEC_T, (i + 1) * DEC_T)
        for g in range(N_GROUPS):
            base = g * QKV_W
            q = qkv_ref[rows, base:base + GROUP_W]
            kn = jnp.concatenate([qkv_ref[rows, base + GROUP_W:base + 2 * GROUP_W], pad], axis=0)
            qm = _per_head_rows(q).astype(BF16)
            s_n = (lax.dot_general(qm, kn.astype(BF16), _NT, preferred_element_type=F32)
                   + tn_ref[g])
            s_c = (jnp.dot(qm, cache_refs[g][i, 0].astype(BF16), preferred_element_type=F32)
                   + tc_refs[g][...])
            scores.append((s_c, s_n))
    return scores


def _sample_outputs(scores, qkv_ref, cache_refs, o_refs, l_refs):
    pad = jnp.zeros((NEW_PAD - DEC_T, GROUP_W), F32)
    scores = iter(scores)
    for i in range(cache_refs[0].shape[0]):
        rows = slice(i * DEC_T, (i + 1) * DEC_T)
        for g in range(N_GROUPS):
            s_c, s_n = next(scores)
            base = g * QKV_W
            vn = jnp.concatenate([qkv_ref[rows, base + 2 * GROUP_W:base + QKV_W], pad], axis=0)
            m = jnp.maximum(jnp.max(s_c, axis=-1, keepdims=True),
                            jnp.max(s_n, axis=-1, keepdims=True))
            p_c = jnp.exp2(s_c - m)
            p_n = jnp.exp2(s_n - m)
            l = jnp.sum(p_c, axis=-1, keepdims=True) + jnp.sum(p_n, axis=-1, keepdims=True)
            acc = jnp.dot(p_n.astype(BF16), vn.astype(BF16), preferred_element_type=F32)
            acc = acc + lax.dot_general(p_c.astype(BF16), cache_refs[g][i, 1].astype(BF16), _NT,
                                        preferred_element_type=F32)
            res = acc / l
            lse = jnp.broadcast_to(m + jnp.log2(l), res.shape)
            o_refs[g][rows, :] = _head_diagonal(res, DEC_T)
            l_refs[g][rows, :] = _head_diagonal(lse, DEC_T)


FF_CHUNK = 1024
DOWN_ROW_GROUPS = 2


def _natural_rows(ref, natural_ref, stage_ref):
    dil, rows, _ = ref.shape
    if dil == 1:
        return ref[0].astype(F32)
    t = dil * rows
    halves = []
    for half in range(GROUP_W // LANES):
        piece = lambda r: ref[r, :, half * LANES:(half + 1) * LANES].astype(F32)
        if dil <= MAX_SUBLANE_STRIDE:
            for r in range(dil):
                natural_ref[half, pl.ds(r, rows, stride=dil), :] = piece(r)
        else:
            s1, s2 = MAX_SUBLANE_STRIDE, dil // MAX_SUBLANE_STRIDE
            assert s1 * s2 == dil and s2 <= MAX_SUBLANE_STRIDE
            for r1 in range(s1):
                for k in range(s2):
                    stage_ref[half, pl.ds(r1 * (t // s1) + k, rows, stride=s2), :] = (
                        piece(s1 * k + r1))
            for r1 in range(s1):
                natural_ref[half, pl.ds(r1, t // s1, stride=s1), :] = (
                    stage_ref[half, pl.ds(r1 * (t // s1), t // s1), :])
        halves.append(natural_ref[half])
    return jnp.concatenate(halves, axis=1)


def _merge_groups(os_, ls_):
    (o0, o1, o2), (l0, l1, l2) = os_, ls_
    m = jnp.maximum(l0, jnp.maximum(l1, l2))
    e0, e1, e2 = jnp.exp2(l0 - m), jnp.exp2(l1 - m), jnp.exp2(l2 - m)
    return ((e0 * o0 + e1 * o1 + e2 * o2) / (e0 + e1 + e2)).astype(BF16)


def _outproj_body(x_ref, o0_ref, o1_ref, o2_ref, l0_ref, l1_ref, l2_ref, sga_ref, mb_ref,
                  wao_ref, wo_ref, w1_ref, w2_ref, nmlp_ref, nfin_ref, y_ref, *scratch):
    if scratch:
        natural_ref, stage_ref = scratch
        os_ = [_natural_rows(ref, natural_ref.at[k], stage_ref.at[k]) for k, ref in
               enumerate((o0_ref, o1_ref, o2_ref))]
        ls_ = [_natural_rows(ref, natural_ref.at[N_GROUPS + k], stage_ref.at[N_GROUPS + k])
               for k, ref in enumerate((l0_ref, l1_ref, l2_ref))]
    else:
        os_ = [ref[...] for ref in (o0_ref, o1_ref, o2_ref)]
        ls_ = [ref[...] for ref in (l0_ref, l1_ref, l2_ref)]
    branch_a = jnp.dot(_merge_groups(os_, ls_), wao_ref[...], preferred_element_type=F32)
    mix = sga_ref[...].astype(F32) * branch_a + mb_ref[...].astype(F32)
    x1 = x_ref[...] + jnp.dot(mix.astype(BF16), wo_ref[...], preferred_element_type=F32)
    h = _rmsnorm(x1, nmlp_ref[...]).astype(BF16)
    acts = []
    for c in range(D_FF // FF_CHUNK):
        a = jnp.dot(h, w1_ref[:, c * FF_CHUNK:(c + 1) * FF_CHUNK], preferred_element_type=F32)
        acts.append(jnp.square(jnp.maximum(a, 0.0)).astype(BF16))
    rows = x1.shape[0] // DOWN_ROW_GROUPS
    for k in range(DOWN_ROW_GROUPS):
        r = slice(k * rows, (k + 1) * rows)
        acc = x1[r]
        for c, a in enumerate(acts):
            acc = acc + jnp.dot(a[r], w2_ref[c * FF_CHUNK:(c + 1) * FF_CHUNK, :],
                                preferred_element_type=F32)
        y_ref[r, :] = _rmsnorm(acc, nfin_ref[...])


def _outproj_prompt(x, os_, ls_, sga, mb, weights, tm):
    b, s, _ = x.shape
    tok = pl.BlockSpec((None, tm, D_MODEL), lambda bi, i: (bi, i, 0))
    grouped = [pl.BlockSpec((None, dil, tm // dil, GROUP_W), lambda bi, i: (bi, 0, i, 0))
               for _, dil in GROUPS]
    return pl.pallas_call(
        _outproj_body,
        grid=(b, s // tm),
        in_specs=[tok] + grouped + grouped + [tok, tok] + [_resident(w.shape) for w in weights],
        out_specs=tok,
        out_shape=jax.ShapeDtypeStruct((b, s, D_MODEL), F32),
        scratch_shapes=[pltpu.VMEM((2 * N_GROUPS, GROUP_W // LANES, tm, LANES), F32)] * 2,
        compiler_params=_params(2),
        name="outproj_ffn_prompt",
    )(x, *os_, *ls_, sga, mb, *weights)


def _outproj_sample(x, os_, ls_, sga, mb, weights, tm):
    n = x.shape[0]
    tok = lambda w: pl.BlockSpec((tm, w), lambda i: (i, 0))
    return pl.pallas_call(
        _outproj_body,
        grid=(n // tm,),
        in_specs=[tok(D_MODEL)] + [tok(GROUP_W)] * 6 + [tok(D_MODEL), tok(D_MODEL)]
        + [_resident(w.shape) for w in weights],
        out_specs=tok(D_MODEL),
        out_shape=jax.ShapeDtypeStruct((n, D_MODEL), F32),
        compiler_params=_params(1),
        name="outproj_ffn_sample",
    )(x, *os_, *ls_, sga, mb, *weights)


def _t5_bucket(dist):
    n = np.asarray(dist)
    max_exact = N_BUCKETS // 2
    large = max_exact + (np.log(np.maximum(n, 1) / max_exact) / np.log(MAX_DISTANCE / max_exact)
                         * (N_BUCKETS - max_exact)).astype(np.int32)
    large = np.minimum(large, N_BUCKETS - 1)
    return np.where(n < max_exact, n, large).astype(np.int32)


def _bias_table(rel_bias, g, dist):
    attended = dist >= 0
    bucket = _t5_bucket(np.maximum(dist, 0)).reshape(1, -1)
    onehot = (bucket == np.arange(N_BUCKETS)[:, None]) & attended.reshape(1, -1)
    heads = rel_bias[:, g * HEADS_PER_GROUP:(g + 1) * HEADS_PER_GROUP].astype(F32).T * LOG2E
    tab = jnp.dot(heads, onehot.astype(np.float32), precision=lax.Precision.HIGHEST)
    mask = np.tile(np.where(attended, 0.0, NEG).astype(np.float32), (HEADS_PER_GROUP, 1))
    return tab.reshape(HEADS_PER_GROUP * dist.shape[0], dist.shape[1]) + mask


def _strides_to_dist(strides, dil):
    return np.where((strides >= 0) & (strides <= KEYS_PER_QUERY), strides * dil, -1)


def _prompt_table(rel_bias, g):
    a = np.arange(BLOCK)[:, None]
    c = np.arange(2 * BLOCK)[None, :]
    tab = _bias_table(rel_bias, g, _strides_to_dist(BLOCK + a - c, GROUPS[g][1]))
    return jnp.stack([jnp.where(c >= BLOCK, tab, NEG), tab])


def _sample_tables(rel_bias, g):
    win, dil = GROUPS[g]
    t = np.arange(DEC_T)[:, None]

    def dist(back):
        return np.where(back % dil == 0, _strides_to_dist(back // dil, dil), -1)

    new = np.arange(NEW_PAD)[None, :]
    tn_dist = np.where(new < DEC_T, dist(t - new), -1)
    return (_bias_table(rel_bias, g, dist(win + t - np.arange(win)[None, :])),
            _bias_table(rel_bias, g, tn_dist))


def _cache_transposed(cache):
    bd, wb = cache.shape[:2]
    return jnp.transpose(cache, (0, 2, 3, 4, 1)).reshape(bd, 2, GROUP_W, wb)


def _new_cache_rows(kv, bd):
    t = kv.shape[0] // bd
    kvt = jnp.transpose(kv.reshape(bd, t, 2 * GROUP_W), (1, 2, 0))
    kvt = kvt.reshape(t, 2, HEADS_PER_GROUP, HEAD_DIM, bd)
    return jnp.transpose(kvt, (4, 0, 1, 2, 3))[None]


def _cache_rows(kvt):
    b, _, _, win = kvt.shape
    kvt = kvt.reshape(b, 2, HEADS_PER_GROUP, HEAD_DIM, win)
    return jnp.transpose(kvt, (0, 4, 1, 2, 3))[None]


TM_INPROJ = 512
TM_OUTPROJ = 512
TQ_ATTN = 4096


def kernel(x_prompt, x_sample, cache_kv_g0, cache_kv_g1, cache_kv_g2, state_conv, w_in, w_att_out,
           w_conv_out, w_o, conv_w, conv_b, rel_bias, norm_mix, norm_mlp, w_ff1, w_ff2, norm_final):
    assert w_in.shape[0] == 1, "one layer"
    b, s, _ = x_prompt.shape
    bd, t, _ = x_sample.shape
    assert t == DEC_T
    wp = w_in[0].astype(BF16)
    wco = w_conv_out[0].astype(BF16)
    wao, wo = w_att_out[0].astype(BF16), w_o[0].astype(BF16)
    w1, w2 = w_ff1[0].astype(BF16), w_ff2[0].astype(BF16)
    nmix, nmlp, nfin = norm_mix[0][None], norm_mlp[0][None], norm_final[None]
    cw, cb = conv_w[0], conv_b[0][None]

    weights = (wao, wo, w1, w2, nmlp, nfin)

    n = bd * t
    st = state_conv[0]
    p0 = jnp.repeat(st[:, 0], t, axis=0)
    p1 = jnp.repeat(st[:, 1], t, axis=0)
    qkv_s, sga_s, mb_s, u_s = _inproj_sample(
        x_sample.reshape(n, D_MODEL), wp, wco, nmix, cw, cb, p0, p1, min(TM_INPROJ, n))
    tabs = [_sample_tables(rel_bias, g) for g in range(N_GROUPS)]
    caches_t = [_cache_transposed(c[0]) for c in (cache_kv_g0, cache_kv_g1, cache_kv_g2)]

    (qkv0, qkv1, qkv2, kvt0, kvt1, kvt2, sga, mb, utail, *sample_att) = _inproj_prompt(
        x_prompt, wp, wco, nmix, cw, cb, qkv_s, caches_t, [tb[0] for tb in tabs],
        jnp.stack([tb[1] for tb in tabs]), TM_INPROJ)
    os_s, ls_s = sample_att[:N_GROUPS], sample_att[N_GROUPS:]
    os_, ls_ = [], []
    for g, qkv in enumerate((qkv0, qkv1, qkv2)):
        o, lse = _attn_prompt(qkv, _prompt_table(rel_bias, g), TQ_ATTN, f"attn_prompt_g{g}")
        os_.append(o)
        ls_.append(lse)
    y_prompt = _outproj_prompt(x_prompt, os_, ls_, sga, mb, weights, TM_OUTPROJ)
    kv_prompt = [_cache_rows(kvt) for kvt in (kvt0, kvt1, kvt2)]
    conv_prompt = utail[:, SUBLANES - 2:][None]

    y_sample = _outproj_sample(x_sample.reshape(n, D_MODEL), os_s, ls_s, sga_s, mb_s, weights,
                               min(TM_OUTPROJ, n))
    kv_sample = [_new_cache_rows(qkv_s[:, g * QKV_W + GROUP_W:(g + 1) * QKV_W], bd)
                 for g in range(N_GROUPS)]
    conv_sample = u_s.reshape(bd, t, CONV_W)[:, t - 2:][None]

    return (y_prompt, y_sample.reshape(bd, t, D_MODEL),
            kv_prompt[0], kv_prompt[1], kv_prompt[2], conv_prompt,
            kv_sample[0], kv_sample[1], kv_sample[2], conv_sample)
```
